```python
import jax, jax.numpy as jnp
from jax import lax
import numpy as np

D_MODEL = 1024
BATCH = 16
SEQ = 4096
DEPTH = 4
DEC_BATCH = 32
DEC_SEQ = 2048
PAST_LEN = 128

MIX_W = 1024
GLA_HEADS = 4
GLA_DK = 32
GLA_DV = 64
GLA_W = 256
GLA_GATE_RANK = 16
GLA_GATE_NORMALIZER = 16.0
GLA_CHUNK = 32
FNET_GROUPS = 4
FNET_GROUP_DIM = 64
FNET_W = 256
MLA_HEADS = 8
MLA_Q_LORA = 256
MLA_KV_LORA = 128
MLA_NOPE = 64
MLA_ROPE = 32
MLA_V = 64
MLA_QK = 96
MLA_W = 512
ROPE_THETA = 10000.0
Q_BLOCK = 128
N_GROUPS = 4
EXPERTS_PER_GROUP = 8
N_EXPERTS = 32
TOP_K = 2
D_EXPERT = 512
MOE_BLOCK = 128
EPS = 1e-6
IN_SIZES = (GLA_HEADS * GLA_DK, GLA_HEADS * GLA_DK, GLA_W, GLA_W, GLA_GATE_RANK, GLA_GATE_RANK, FNET_W, MLA_Q_LORA, MLA_KV_LORA, MLA_ROPE)
P_IN = 1472

kernel_name = 'hybrid_bidir_gla_fnet_mla_hmoe'


def rms_norm(x, g):
    xf = x.astype(jnp.float32)
    y = xf * lax.rsqrt(jnp.mean(xf * xf, axis=-1, keepdims=True) + EPS)
    return (y * g.astype(jnp.float32)).astype(x.dtype)


def split_cols(u, sizes):
    idx = np.cumsum(np.array(sizes))[:-1].tolist()
    return jnp.split(u, idx, axis=-1)


def to_heads(t, n_heads):
    B, S, _ = t.shape
    return t.reshape(B, S, n_heads, -1).transpose(0, 2, 1, 3)


def apply_rope(x, pos):
    half = MLA_ROPE // 2
    freqs = ROPE_THETA ** (-jnp.arange(half, dtype=jnp.float32) / half)
    ang = pos.astype(jnp.float32)[:, None] * freqs[None, :]
    cos = jnp.cos(ang).astype(x.dtype)
    sin = jnp.sin(ang).astype(x.dtype)
    x1, x2 = x[..., :half], x[..., half:]
    return jnp.concatenate([x1 * cos - x2 * sin, x1 * sin + x2 * cos], axis=-1)


def gla_chunked(q, k, v, g):
    B, H, S, DK = q.shape
    DV = v.shape[-1]
    C = GLA_CHUNK
    N = S // C
    q = q.reshape(B, H, N, C, DK)
    k = k.reshape(B, H, N, C, DK)
    v = v.reshape(B, H, N, C, DV)
    b = jnp.cumsum(g.reshape(B, H, N, C, DK), axis=3)
    b_last = b[:, :, :, -1:, :]
    q_dec = q * jnp.exp(b)
    k_inv = k * jnp.exp(-b)
    k_end = k * jnp.exp(b_last - b)
    mask = jnp.tril(jnp.ones((C, C), dtype=bool))
    att = jnp.where(mask, jnp.einsum('bhnid,bhnjd->bhnij', q_dec, k_inv), 0.0)
    o_intra = jnp.einsum('bhnij,bhnje->bhnie', att, v)
    d_state = jnp.einsum('bhncd,bhnce->bhnde', k_end, v)
    decay = jnp.exp(b_last[:, :, :, 0, :])

    def step(state, inp):
        dec, ds = inp
        return state * dec[..., None] + ds, state

    init = jnp.zeros((B, H, DK, DV), jnp.float32)
    _, states = lax.scan(step, init, (jnp.moveaxis(decay, 2, 0), jnp.moveaxis(d_state, 2, 0)))
    states = jnp.moveaxis(states, 0, 2)
    o_inter = jnp.einsum('bhncd,bhnde->bhnce', q_dec, states)
    return (o_intra + o_inter).reshape(B, H, S, DV)


def gla_branch(u_q, u_k, u_v, u_og, u_gf, u_gb, up_f, bias_f, up_b, bias_b, out_norm_g):
    B, S, _ = u_q.shape
    f32 = jnp.float32
    q = to_heads(u_q, GLA_HEADS).astype(f32) * (GLA_DK ** -0.5)
    k = to_heads(u_k, GLA_HEADS).astype(f32)
    v = to_heads(u_v, GLA_HEADS).astype(f32)
    g_f = to_heads(jax.nn.log_sigmoid((u_gf @ up_f + bias_f).astype(f32)) / GLA_GATE_NORMALIZER, GLA_HEADS)
    g_b = to_heads(jax.nn.log_sigmoid((u_gb @ up_b + bias_b).astype(f32)) / GLA_GATE_NORMALIZER, GLA_HEADS)
    flip = lambda t: jnp.flip(t, axis=2)
    o_f = gla_chunked(q, k, v, g_f)
    o_b = flip(gla_chunked(flip(q), flip(k), flip(v), flip(g_b)))
    o = o_f + o_b - jnp.sum(q * k, axis=-1, keepdims=True) * v
    o = rms_norm(o, out_norm_g)
    o = o.transpose(0, 2, 1, 3).reshape(B, S, GLA_W).astype(u_og.dtype)
    return o * jax.nn.silu(u_og)


def fourier_branch(u_f):
    B, S, _ = u_f.shape
    xf = u_f.astype(jnp.float32).reshape(B, S, FNET_GROUPS, FNET_GROUP_DIM)
    y = jnp.fft.fft2(xf, axes=(1, 3), norm='ortho').real
    return y.reshape(B, S, FNET_W).astype(u_f.dtype)


def block_attention(q, k, v):
    B, H, S, Dq = q.shape
    Dv = v.shape[-1]
    nb = S // Q_BLOCK
    qb = jnp.moveaxis(q.reshape(B, H, nb, Q_BLOCK, Dq), 2, 0)
    scale = Dq ** -0.5

    def one_block(qi):
        s = jnp.einsum('bhqd,bhkd->bhqk', qi, k).astype(jnp.float32) * scale
        p = jax.nn.softmax(s, axis=-1)
        return jnp.einsum('bhqk,bhkd->bhqd', p.astype(v.dtype), v)

    o = lax.map(one_block, qb)
    return jnp.moveaxis(o, 0, 2).reshape(B, H, S, Dv)


def mla_branch(u_cq, u_ckv, u_kr, pos, q_lora_g, w_uq, kv_lora_g, w_ukv, q_norm_g, k_norm_g):
    B, S, _ = u_cq.shape
    q = to_heads(rms_norm(u_cq, q_lora_g) @ w_uq, MLA_HEADS)
    kv = to_heads(rms_norm(u_ckv, kv_lora_g) @ w_ukv, MLA_HEADS)
    k_nope, v = kv[..., :MLA_NOPE], kv[..., MLA_NOPE:]
    k_rope = jnp.broadcast_to(u_kr[:, None, :, :], (B, MLA_HEADS, S, MLA_ROPE))
    k = jnp.concatenate([k_nope, k_rope], axis=-1)
    q = rms_norm(q, q_norm_g)
    k = rms_norm(k, k_norm_g)
    q = jnp.concatenate([q[..., :MLA_NOPE], apply_rope(q[..., MLA_NOPE:], pos)], axis=-1)
    k = jnp.concatenate([k[..., :MLA_NOPE], apply_rope(k[..., MLA_NOPE:], pos)], axis=-1)
    o = block_attention(q, k, v)
    return o.transpose(0, 2, 1, 3).reshape(B, S, MLA_W)


def hier_moe(h, rg_w, rg_b, re_w, re_b, w1, w3, w2):
    B, S, D = h.shape
    T = B * S
    f32 = jnp.float32
    x = h.reshape(T, D)
    g_prob = jax.nn.softmax((x @ rg_w).astype(f32) + rg_b.astype(f32), axis=-1)
    g_top = jnp.argmax(g_prob, axis=-1)
    g_w = jnp.take_along_axis(g_prob, g_top[:, None], axis=-1)
    e_logits = ((x @ re_w).astype(f32) + re_b.astype(f32)).reshape(T, N_GROUPS, EXPERTS_PER_GROUP)
    e_in = jnp.take_along_axis(e_logits, g_top[:, None, None], axis=1)[:, 0]
    top_l, top_i = lax.top_k(e_in, TOP_K)
    e_w = jax.nn.softmax(top_l, axis=-1) * g_w
    e_id = g_top[:, None].astype(jnp.int32) * EXPERTS_PER_GROUP + top_i.astype(jnp.int32)
    R = T * TOP_K
    flat_e = e_id.reshape(R)
    flat_w = e_w.reshape(R)
    flat_tok = jnp.repeat(jnp.arange(T, dtype=jnp.int32), TOP_K)
    order = jnp.argsort(flat_e)
    se = flat_e[order]
    counts = jnp.bincount(flat_e, length=N_EXPERTS)
    padded = (counts + MOE_BLOCK - 1) // MOE_BLOCK * MOE_BLOCK
    pend = jnp.cumsum(padded)
    pstart = pend - padded
    start = jnp.cumsum(counts) - counts
    dest = pstart[se] + jnp.arange(R, dtype=jnp.int32) - start[se]
    P = (R + MOE_BLOCK - 1) // MOE_BLOCK * MOE_BLOCK + N_EXPERTS * MOE_BLOCK
    n_blk = P // MOE_BLOCK
    row_tok = jnp.full((P,), T, jnp.int32).at[dest].set(flat_tok[order])
    row_w = jnp.zeros((P,), f32).at[dest].set(flat_w[order])
    blk_e = jnp.minimum(jnp.searchsorted(pend, jnp.arange(n_blk, dtype=jnp.int32) * MOE_BLOCK, side='right'), N_EXPERTS - 1)
    x_pad = jnp.concatenate([x, jnp.zeros((1, D), x.dtype)], axis=0)
    xb = x_pad[row_tok].reshape(n_blk, MOE_BLOCK, D)

    def expert_block(args):
        xi, e = args
        hm = jax.nn.silu(xi @ w1[e]) * (xi @ w3[e])
        return hm @ w2[e]

    yb = lax.map(expert_block, (xb, blk_e)).reshape(P, D)
    y = jnp.zeros((T + 1, D), x.dtype).at[row_tok].add(yb * row_w[:, None].astype(x.dtype))[:T]
    return y.reshape(B, S, D)


def trunk(x, c, ada_w, ada_b, norm1_g, norm2_g, w_in, w_out, gla_gate_up_f, gla_gate_bias_f, gla_gate_up_b, gla_gate_bias_b, gla_out_norm_g, mla_q_lora_norm_g, mla_w_uq, mla_kv_lora_norm_g, mla_w_ukv, mla_q_norm_g, mla_k_norm_g, router_group_w, router_group_b, router_expert_w, router_expert_b, expert_w1, expert_w3, expert_w2):
    B, S, D = x.shape
    pos = jnp.arange(S, dtype=jnp.int32)
    c_act = jax.nn.silu(c)
    for l in range(DEPTH):
        mod = (c_act @ ada_w[l] + ada_b[l])[:, None, :]
        sh1, sc1, g1, sh2, sc2, g2 = jnp.split(mod, 6, axis=-1)
        h = rms_norm(x, norm1_g[l]) * (1.0 + sc1) + sh1
        u = h @ w_in[l]
        u_q, u_k, u_v, u_og, u_gf, u_gb, u_f, u_cq, u_ckv, u_kr = split_cols(u, IN_SIZES)
        o_gla = gla_branch(u_q, u_k, u_v, u_og, u_gf, u_gb, gla_gate_up_f[l], gla_gate_bias_f[l], gla_gate_up_b[l], gla_gate_bias_b[l], gla_out_norm_g[l])
        o_fft = fourier_branch(u_f)
        o_mla = mla_branch(u_cq, u_ckv, u_kr, pos, mla_q_lora_norm_g[l], mla_w_uq[l], mla_kv_lora_norm_g[l], mla_w_ukv[l], mla_q_norm_g[l], mla_k_norm_g[l])
        mixed = jnp.concatenate([o_gla, o_fft, o_mla], axis=-1)
        x = x + g1 * (mixed @ w_out[l])
        h = rms_norm(x, norm2_g[l]) * (1.0 + sc2) + sh2
        x = x + g2 * hier_moe(h, router_group_w[l], router_group_b[l], router_expert_w[l], router_expert_b[l], expert_w1[l], expert_w3[l], expert_w2[l])
    return x


def setup_inputs(seed: int = 0) -> dict:
    key = jax.random.key(seed)
    ks = jax.random.split(key, 32)
    f32 = jnp.float32
    D = D_MODEL
    nrm = lambda k, shape, s: jax.random.normal(k, shape, f32) * s
    gain = lambda k, shape: 1.0 + 0.05 * jax.random.normal(k, shape, f32)
    return {
        'x_prompt': nrm(ks[0], (BATCH, SEQ, D), 1.0),
        'x_sample': nrm(ks[1], (DEC_BATCH, DEC_SEQ, D), 1.0),
        'c_prompt': nrm(ks[2], (BATCH, D), 1.0),
        'c_sample': nrm(ks[3], (DEC_BATCH, D), 1.0),
        'ada_w': nrm(ks[4], (DEPTH, D, 6 * D), 0.5 * D ** -0.5),
        'ada_b': nrm(ks[5], (DEPTH, 6 * D), 0.01),
        'norm1_g': gain(ks[6], (DEPTH, D)),
        'norm2_g': gain(ks[7], (DEPTH, D)),
        'w_in': nrm(ks[8], (DEPTH, D, P_IN), D ** -0.5),
        'w_out': nrm(ks[9], (DEPTH, MIX_W, D), MIX_W ** -0.5),
        'gla_gate_up_f': nrm(ks[10], (DEPTH, GLA_GATE_RANK, GLA_HEADS * GLA_DK), GLA_GATE_RANK ** -0.5),
        'gla_gate_bias_f': nrm(ks[11], (DEPTH, GLA_HEADS * GLA_DK), 0.1),
        'gla_gate_up_b': nrm(ks[12], (DEPTH, GLA_GATE_RANK, GLA_HEADS * GLA_DK), GLA_GATE_RANK ** -0.5),
        'gla_gate_bias_b': nrm(ks[13], (DEPTH, GLA_HEADS * GLA_DK), 0.1),
        'gla_out_norm_g': gain(ks[14], (DEPTH, GLA_DV)),
        'mla_q_lora_norm_g': gain(ks[15], (DEPTH, MLA_Q_LORA)),
        'mla_w_uq': nrm(ks[16], (DEPTH, MLA_Q_LORA, MLA_HEADS * MLA_QK), MLA_Q_LORA ** -0.5),
        'mla_kv_lora_norm_g': gain(ks[17], (DEPTH, MLA_KV_LORA)),
        'mla_w_ukv': nrm(ks[18], (DEPTH, MLA_KV_LORA, MLA_HEADS * (MLA_NOPE + MLA_V)), MLA_KV_LORA ** -0.5),
        'mla_q_norm_g': gain(ks[19], (DEPTH, MLA_QK)),
        'mla_k_norm_g': gain(ks[20], (DEPTH, MLA_QK)),
        'router_group_w': nrm(ks[21], (DEPTH, D, N_GROUPS), D ** -0.5),
        'router_group_b': nrm(ks[22], (DEPTH, N_GROUPS), 0.01),
        'router_expert_w': nrm(ks[23], (DEPTH, D, N_EXPERTS), D ** -0.5),
        'router_expert_b': nrm(ks[24], (DEPTH, N_EXPERTS), 0.01),
        'expert_w1': nrm(ks[25], (DEPTH, N_EXPERTS, D, D_EXPERT), D ** -0.5),
        'expert_w3': nrm(ks[26], (DEPTH, N_EXPERTS, D, D_EXPERT), D ** -0.5),
        'expert_w2': nrm(ks[27], (DEPTH, N_EXPERTS, D_EXPERT, D), D_EXPERT ** -0.5),
    }


def reference(x_prompt, x_sample, c_prompt, c_sample, ada_w, ada_b, norm1_g, norm2_g, w_in, w_out, gla_gate_up_f, gla_gate_bias_f, gla_gate_up_b, gla_gate_bias_b, gla_out_norm_g, mla_q_lora_norm_g, mla_w_uq, mla_kv_lora_norm_g, mla_w_ukv, mla_q_norm_g, mla_k_norm_g, router_group_w, router_group_b, router_expert_w, router_expert_b, expert_w1, expert_w3, expert_w2):
    y_prompt = trunk(x_prompt, c_prompt, ada_w, ada_b, norm1_g, norm2_g, w_in, w_out, gla_gate_up_f, gla_gate_bias_f, gla_gate_up_b, gla_gate_bias_b, gla_out_norm_g, mla_q_lora_norm_g, mla_w_uq, mla_kv_lora_norm_g, mla_w_ukv, mla_q_norm_g, mla_k_norm_g, router_group_w, router_group_b, router_expert_w, router_expert_b, expert_w1, expert_w3, expert_w2)
    y_sample = trunk(x_sample, c_sample, ada_w, ada_b, norm1_g, norm2_g, w_in, w_out, gla_gate_up_f, gla_gate_bias_f, gla_gate_up_b, gla_gate_bias_b, gla_out_norm_g, mla_q_lora_norm_g, mla_w_uq, mla_kv_lora_norm_g, mla_w_ukv, mla_q_norm_g, mla_k_norm_g, router_group_w, router_group_b, router_expert_w, router_expert_b, expert_w1, expert_w3, expert_w2)
    return (y_prompt, y_sample)
```

```python
import functools
import math

import jax
import jax.numpy as jnp
import numpy as np
from jax import lax
from jax.experimental import pallas as pl
from jax.experimental.pallas import tpu as pltpu

F32 = jnp.float32
BF16 = jnp.bfloat16

D_MODEL = 1024
DEPTH = 4
EPS = 1e-6

GLA_HEADS = 4
GLA_DK = 32
GLA_DV = 64
GLA_QK_W = GLA_HEADS * GLA_DK
GLA_W = GLA_HEADS * GLA_DV
GLA_GATE_RANK = 16
GLA_GATE_NORMALIZER = 16.0
GLA_CHUNK = 32
GLA_BLOCK = 128

FNET_GROUPS = 4
FNET_GROUP_DIM = 64
FNET_W = 256

MLA_HEADS = 8
MLA_Q_LORA = 256
MLA_KV_LORA = 128
MLA_NOPE = 64
MLA_ROPE = 32
MLA_V = 64
MLA_QK = 96
MLA_W = 512
ROPE_THETA = 10000.0
HEAD_PAD = 128

N_GROUPS = 4
EXPERTS_PER_GROUP = 8
N_EXPERTS = 32
TOP_K = 2
D_EXPERT = 512
ROUTE_LANES = 128

C_Q, C_K, C_V, C_OG, C_F, C_CQ, C_CKV, C_KR, C_GATE = 0, 128, 256, 512, 768, 1024, 1280, 1408, 1536
P_IN_PAD = 1664

VMEM_LIMIT_BYTES = 56 * 1024 * 1024


def _cparams(sem, vmem=None):
    return pltpu.CompilerParams(dimension_semantics=sem, vmem_limit_bytes=vmem or VMEM_LIMIT_BYTES)


def _silu(x):
    return x * (1.0 / (1.0 + jnp.exp(-x)))


def _log_sigmoid(x):
    return -(jnp.maximum(-x, 0.0) + jnp.log1p(jnp.exp(-jnp.abs(x))))


def _dot(a, b):
    return jnp.dot(a, b, preferred_element_type=F32)


def _dot_nt(a, b):
    return lax.dot_general(a, b, (((1,), (1,)), ((), ())), preferred_element_type=F32)


def _split_bf16(x):
    hi = x.astype(BF16)
    lo = (x - hi.astype(F32)).astype(BF16)
    return hi, lo


def _ada_kernel(c_ref, w_ref, b_ref, o_ref):
    c = _silu(c_ref[...]).astype(BF16)
    o_ref[0] = _dot(c, w_ref[0].astype(BF16)) + b_ref[0]


def ada_modulation(c_all, ada_w, ada_b):
    nb = c_all.shape[0]
    tn = 1536
    n = ada_w.shape[-1]
    return pl.pallas_call(
        _ada_kernel,
        grid=(DEPTH, n // tn),
        in_specs=[
            pl.BlockSpec((nb, D_MODEL), lambda l, j: (0, 0)),
            pl.BlockSpec((1, D_MODEL, tn), lambda l, j: (l, 0, j)),
            pl.BlockSpec((1, 1, tn), lambda l, j: (l, 0, j)),
        ],
        out_specs=pl.BlockSpec((1, nb, tn), lambda l, j: (l, 0, j)),
        out_shape=jax.ShapeDtypeStruct((DEPTH, nb, n), F32),
        compiler_params=_cparams(("parallel", "parallel")),
        name="ada_modulation",
    )(c_all, ada_w, ada_b.reshape(DEPTH, 1, n))


def _in_proj_kernel(x_ref, mod_ref, g_ref, w_ref, dft_ref, up_ref, gb_ref, lg_ref,
                    qkv_ref, og_ref, gate_ref, xcs_ref, cq_ref, ckv_ref, kr_ref):
    x = x_ref[0]
    sh = mod_ref[0, 0:1, :]
    sc = mod_ref[0, 1:2, :]
    r = lax.rsqrt(jnp.mean(x * x, axis=-1, keepdims=True) + EPS)
    h = (x * r * g_ref[...]) * (1.0 + sc) + sh
    u = _dot(h.astype(BF16), w_ref[...])

    q = u[:, C_Q:C_K] * (GLA_DK ** -0.5)
    qkv_ref[0, :, 0:128] = q.astype(BF16)
    qkv_ref[0, :, 128:512] = u[:, C_K:C_OG].astype(BF16)
    og_ref[0] = u[:, C_OG:C_F].astype(BF16)

    ug = u[:, C_GATE:C_GATE + 128].astype(BF16)
    gl = _dot(ug, up_ref[...]) + gb_ref[...]
    gate_ref[0] = _log_sigmoid(gl) * (1.0 / GLA_GATE_NORMALIZER)

    uf = u[:, C_F:C_CQ].astype(BF16)
    xcs_ref[0] = _dot(uf, dft_ref[0]).astype(BF16)
    xcs_ref[1] = _dot(uf, dft_ref[1]).astype(BF16)

    cq = u[:, C_CQ:C_CKV]
    rq = lax.rsqrt(jnp.mean(cq * cq, axis=-1, keepdims=True) + EPS)
    cq_ref[0] = (cq * rq * lg_ref[:, 0:256]).astype(BF16)
    ckv = u[:, C_CKV:C_KR]
    rkv = lax.rsqrt(jnp.mean(ckv * ckv, axis=-1, keepdims=True) + EPS)
    ckv_ref[0] = (ckv * rkv * lg_ref[:, 256:384]).astype(BF16)
    kr_ref[0] = u[:, C_KR:C_GATE].astype(BF16)


def in_proj(x, mod, norm_g, w_in, dft64, gate_up, gate_bias, lora_g, tm):
    B, S, D = x.shape
    ns = S // tm
    row = lambda b, s: (b, s, 0)
    const2 = lambda b, s: (0, 0)
    out_shapes = (
        jax.ShapeDtypeStruct((B, S, 512), BF16),
        jax.ShapeDtypeStruct((B, S, GLA_W), BF16),
        jax.ShapeDtypeStruct((B, S, 256), F32),
        jax.ShapeDtypeStruct((2, S, B * FNET_W), BF16),
        jax.ShapeDtypeStruct((B, S, MLA_Q_LORA), BF16),
        jax.ShapeDtypeStruct((B, S, MLA_KV_LORA), BF16),
        jax.ShapeDtypeStruct((B, S, HEAD_PAD), BF16),
    )
    return pl.pallas_call(
        _in_proj_kernel,
        grid=(B, ns),
        in_specs=[
            pl.BlockSpec((1, tm, D), row),
            pl.BlockSpec((1, 6, D), lambda b, s: (b, 0, 0)),
            pl.BlockSpec((1, D), const2),
            pl.BlockSpec((D, P_IN_PAD), const2),
            pl.BlockSpec((2, FNET_W, FNET_W), lambda b, s: (0, 0, 0)),
            pl.BlockSpec((128, 256), const2),
            pl.BlockSpec((1, 256), const2),
            pl.BlockSpec((1, 384), const2),
        ],
        out_specs=(
            pl.BlockSpec((1, tm, 512), row),
            pl.BlockSpec((1, tm, GLA_W), row),
            pl.BlockSpec((1, tm, 256), row),
            pl.BlockSpec((2, tm, FNET_W), lambda b, s: (0, s, b)),
            pl.BlockSpec((1, tm, MLA_Q_LORA), row),
            pl.BlockSpec((1, tm, MLA_KV_LORA), row),
            pl.BlockSpec((1, tm, HEAD_PAD), row),
        ),
        out_shape=out_shapes,
        compiler_params=_cparams(("parallel", "parallel")),
        name="in_proj",
    )(x, mod, norm_g, w_in, dft64, gate_up, gate_bias, lora_g)


def _gla_kernel(qkv_ref, og_ref, gate_ref, ng_ref, o_ref, of_s, st_s, *, seq):
    R, C = GLA_BLOCK, GLA_CHUNK
    n_sub = R // C
    nblk = seq // R
    shift_c = int(math.log2(C))

    row = lax.broadcasted_iota(jnp.int32, (R, R), 0)
    col = lax.broadcasted_iota(jnp.int32, (R, R), 1)
    same = (row >> shift_c) == (col >> shift_c)
    ones_blk = jnp.where(same, 1.0, 0.0)
    lane_qk = lax.broadcasted_iota(jnp.int32, (1, GLA_QK_W), 1)
    lane_v = lax.broadcasted_iota(jnp.int32, (1, GLA_W), 1)
    head_qk = [(lane_qk >> 5) == h for h in range(GLA_HEADS)]
    head_v = [(lane_v >> 6) == h for h in range(GLA_HEADS)]
    st_row = lax.broadcasted_iota(jnp.int32, (GLA_W, GLA_QK_W), 0)
    st_col = lax.broadcasted_iota(jnp.int32, (GLA_W, GLA_QK_W), 1)
    st_mask = (st_row >> 6) == (st_col >> 5)
    sub_row = lax.broadcasted_iota(jnp.int32, (R, 1), 0) >> shift_c
    nrow = lax.broadcasted_iota(jnp.int32, (GLA_W, GLA_W), 0)
    ncol = lax.broadcasted_iota(jnp.int32, (GLA_W, GLA_W), 1)
    head_mean = jnp.where((nrow >> 6) == (ncol >> 6), 1.0 / GLA_DV, 0.0).astype(BF16)

    def run_direction(backward):
        if backward:
            tri = same & (col >= row)
            att_ok = same & (col > row)
            gate_off = GLA_QK_W
            order = list(range(n_sub - 1, -1, -1))
        else:
            tri = same & (col <= row)
            att_ok = same & (col <= row)
            gate_off = 0
            order = list(range(n_sub))
        cum_lhs = jnp.concatenate([jnp.where(tri, 1.0, 0.0), ones_blk], axis=0).astype(BF16)
        att_ok4 = jnp.concatenate([att_ok] * GLA_HEADS, axis=1)

        st_s[...] = jnp.zeros_like(st_s)

        def body(i, carry):
            j = (nblk - 1 - i) if backward else i
            r0 = pl.multiple_of(j * R, R)
            qkv = qkv_ref[0, pl.ds(r0, R), :]
            q = qkv[:, 0:128].astype(F32)
            k = qkv[:, 128:256].astype(F32)
            v = qkv[:, 256:512]
            g = gate_ref[0, pl.ds(r0, R), gate_off:gate_off + GLA_QK_W]
            g_hi, g_lo = _split_bf16(g)
            cs = _dot(cum_lhs, jnp.concatenate([g_hi, g_lo], axis=1))
            b = cs[0:R, 0:128] + cs[0:R, 128:256]
            bl = cs[R:2 * R, 0:128] + cs[R:2 * R, 128:256]
            q_dec = q * jnp.exp(b)
            k_inv = (k * jnp.exp(-b)).astype(BF16)
            k_end = k * jnp.exp(bl - b)
            decay = jnp.exp(bl)

            att = jnp.concatenate(
                [_dot_nt(jnp.where(head_qk[h], q_dec, 0.0).astype(BF16), k_inv) for h in range(GLA_HEADS)],
                axis=1)
            att = jnp.where(att_ok4, att, 0.0).astype(BF16)
            v_heads = jnp.concatenate([jnp.where(head_v[h], v, jnp.zeros_like(v)) for h in range(GLA_HEADS)], axis=0)
            o_blk = _dot(att, v_heads)

            v_t = v.astype(F32).T.astype(BF16)
            q_dec_b = q_dec.astype(BF16)
            inter = [None] * n_sub
            for c in order:
                st = st_s[...]
                inter[c] = _dot_nt(q_dec_b[c * C:(c + 1) * C, :], st.astype(BF16))
                k_c = jnp.where(sub_row == c, k_end, 0.0).astype(BF16)
                d_st = _dot(v_t, k_c)
                st_s[...] = st * decay[c * C:c * C + 1, :] + jnp.where(st_mask, d_st, 0.0)
            o_blk = o_blk + jnp.concatenate(inter, axis=0)

            if not backward:
                of_s[pl.ds(r0, R), :] = o_blk
            else:
                o = of_s[pl.ds(r0, R), :] + o_blk
                sq_hi, sq_lo = _split_bf16(o * o)
                ms = _dot(sq_hi, head_mean) + _dot(sq_lo, head_mean)
                y = o * lax.rsqrt(ms + EPS) * ng_ref[...]
                og = og_ref[0, pl.ds(r0, R), :].astype(F32)
                o_ref[0, pl.ds(r0, R), :] = (y * _silu(og)).astype(o_ref.dtype)
            return carry

        lax.fori_loop(0, nblk, body, 0)

    run_direction(False)
    run_direction(True)


def gla(qkv, og, gate, norm_g4):
    B, S, _ = qkv.shape
    full = lambda b: (b, 0, 0)
    return pl.pallas_call(
        functools.partial(_gla_kernel, seq=S),
        grid=(B,),
        in_specs=[
            pl.BlockSpec((1, S, 512), full),
            pl.BlockSpec((1, S, GLA_W), full),
            pl.BlockSpec((1, S, 256), full),
            pl.BlockSpec((1, GLA_W), lambda b: (0, 0)),
        ],
        out_specs=pl.BlockSpec((1, S, GLA_W), full),
        out_shape=jax.ShapeDtypeStruct((B, S, GLA_W), BF16),
        scratch_shapes=[pltpu.VMEM((S, GLA_W), F32), pltpu.VMEM((GLA_W, GLA_QK_W), F32)],
        compiler_params=_cparams(("parallel",)),
        name="gla",
    )(qkv, og, gate, norm_g4)


def _dft_kernel(a_ref, b_ref, o_ref, acc_ref, *, scale):
    kk = pl.program_id(2)

    @pl.when(kk == 0)
    def _():
        acc_ref[...] = jnp.zeros_like(acc_ref)

    acc_ref[...] += _dot(a_ref[...], b_ref[...])

    @pl.when(kk == pl.num_programs(2) - 1)
    def _():
        o_ref[...] = (acc_ref[...] * scale).astype(o_ref.dtype)


def dft_seq(a_dft, xcs, scale):
    m, k = a_dft.shape
    n = xcs.shape[1]
    tm, tn, tk = min(m, 1024), min(n, 1024), min(k, 2048)
    return pl.pallas_call(
        functools.partial(_dft_kernel, scale=scale),
        grid=(m // tm, n // tn, k // tk),
        in_specs=[pl.BlockSpec((tm, tk), lambda i, j, kk: (i, kk)),
                  pl.BlockSpec((tk, tn), lambda i, j, kk: (kk, j))],
        out_specs=pl.BlockSpec((tm, tn), lambda i, j, kk: (i, j)),
        out_shape=jax.ShapeDtypeStruct((m, n), BF16),
        scratch_shapes=[pltpu.VMEM((tm, tn), F32)],
        compiler_params=_cparams(("parallel", "parallel", "arbitrary")),
        name="dft_seq",
    )(a_dft, xcs)


def _rope(x, cos, sin_up, sin_dn):
    return x * cos + pltpu.roll(x, 16, 1) * sin_up + pltpu.roll(x, 112, 1) * sin_dn


def _mla_kernel(cq_ref, ckv_ref, kr_ref, rope_ref, wq_ref, wk_ref, wv_ref, qg_ref, kg_ref, vone_ref,
                o_ref, k_s, v_s, *, seq, tq, tk):
    qi = pl.program_id(1)
    inv_qk = 1.0 / MLA_QK

    @pl.when(qi == 0)
    def _():
        rc = min(seq, 512)

        def rows(i, carry):
            r0 = pl.multiple_of(i * rc, rc)
            ckv = ckv_ref[0, pl.ds(r0, rc), :]
            kr = kr_ref[0, pl.ds(r0, rc), :].astype(F32)
            cos = rope_ref[0, pl.ds(r0, rc), :]
            sin_up = rope_ref[1, pl.ds(r0, rc), :]
            sin_dn = rope_ref[2, pl.ds(r0, rc), :]
            for h in range(MLA_HEADS):
                kp = _dot(ckv, wk_ref[h]) + kr
                r = lax.rsqrt(jnp.sum(kp * kp, axis=-1, keepdims=True) * inv_qk + EPS)
                kn = _rope(kp * r * kg_ref[...], cos, sin_up, sin_dn)
                k_s[h, pl.ds(r0, rc), :] = kn.astype(BF16)
                v_s[h, pl.ds(r0, rc), :] = (_dot(ckv, wv_ref[h]) + vone_ref[h]).astype(BF16)
            return carry

        lax.fori_loop(0, seq // rc, rows, 0)

    q0 = pl.multiple_of(qi * tq, tq)
    cq = cq_ref[0]
    cos = rope_ref[0, pl.ds(q0, tq), :]
    sin_up = rope_ref[1, pl.ds(q0, tq), :]
    sin_dn = rope_ref[2, pl.ds(q0, tq), :]
    lane = lax.broadcasted_iota(jnp.int32, (1, HEAD_PAD), 1)

    for pair in range(MLA_HEADS // 2):
        outs = []
        for h in (2 * pair, 2 * pair + 1):
            qp = _dot(cq, wq_ref[h])
            r = lax.rsqrt(jnp.sum(qp * qp, axis=-1, keepdims=True) * inv_qk + EPS)
            qn = _rope(qp * r * qg_ref[...], cos, sin_up, sin_dn).astype(BF16)

            def kv_step(j, carry, h=h, qn=qn):
                m, acc = carry
                k0 = pl.multiple_of(j * tk, tk)
                s = _dot_nt(qn, k_s[h, pl.ds(k0, tk), :])
                m_new = jnp.maximum(m, jnp.max(s, axis=-1, keepdims=True))
                p = jnp.exp(s - m_new).astype(BF16)
                acc = jnp.exp(m - m_new) * acc + _dot(p, v_s[h, pl.ds(k0, tk), :])
                return m_new, acc

            m0 = jnp.full((tq, 1), -jnp.inf, F32)
            acc0 = jnp.zeros((tq, HEAD_PAD), F32)
            _, acc = lax.fori_loop(0, seq // tk, kv_step, (m0, acc0))
            den_lane = MLA_V if h % 2 == 0 else 0
            den = jnp.sum(jnp.where(lane == den_lane, acc, 0.0), axis=-1, keepdims=True)
            outs.append(acc * (1.0 / den))
        o_ref[0, :, pair * HEAD_PAD:(pair + 1) * HEAD_PAD] = jnp.where(lane < MLA_V, outs[0], outs[1]).astype(o_ref.dtype)


def mla(cq, ckv, kr, rope, wq, wk, wv, qg, kg, vone, tq, tk):
    B, S, _ = cq.shape
    c3 = lambda b, q: (0, 0, 0)
    c2 = lambda b, q: (0, 0)
    return pl.pallas_call(
        functools.partial(_mla_kernel, seq=S, tq=tq, tk=tk),
        grid=(B, S // tq),
        in_specs=[
            pl.BlockSpec((1, tq, MLA_Q_LORA), lambda b, q: (b, q, 0)),
            pl.BlockSpec((1, S, MLA_KV_LORA), lambda b, q: (b, 0, 0)),
            pl.BlockSpec((1, S, HEAD_PAD), lambda b, q: (b, 0, 0)),
            pl.BlockSpec((3, S, HEAD_PAD), c3),
            pl.BlockSpec((MLA_HEADS, MLA_Q_LORA, HEAD_PAD), c3),
            pl.BlockSpec((MLA_HEADS, MLA_KV_LORA, HEAD_PAD), c3),
            pl.BlockSpec((MLA_HEADS, MLA_KV_LORA, HEAD_PAD), c3),
            pl.BlockSpec((1, HEAD_PAD), c2),
            pl.BlockSpec((1, HEAD_PAD), c2),
            pl.BlockSpec((MLA_HEADS, 1, HEAD_PAD), c3),
        ],
        out_specs=pl.BlockSpec((1, tq, MLA_W), lambda b, q: (b, q, 0)),
        out_shape=jax.ShapeDtypeStruct((B, S, MLA_W), BF16),
        scratch_shapes=[pltpu.VMEM((MLA_HEADS, S, HEAD_PAD), BF16), pltpu.VMEM((MLA_HEADS, S, HEAD_PAD), BF16)],
        compiler_params=_cparams(("parallel", "arbitrary")),
        name="mla",
    )(cq, ckv, kr, rope, wq, wk, wv, qg, kg, vone)


def _out_proj_kernel(x_ref, gla_ref, fft_ref, mla_ref, mod_ref, w_ref, g_ref, rw_ref, rb_ref,
                     xo_ref, h_ref, route_ref):
    mix = (_dot(gla_ref[0], w_ref[0:256, :]) + _dot(fft_ref[...], w_ref[256:512, :])
           + _dot(mla_ref[0], w_ref[512:1024, :]))
    g1 = mod_ref[0, 2:3, :]
    sh = mod_ref[0, 3:4, :]
    sc = mod_ref[0, 4:5, :]
    x = x_ref[0] + g1 * mix
    xo_ref[0] = x
    r = lax.rsqrt(jnp.mean(x * x, axis=-1, keepdims=True) + EPS)
    h = (x * r * g_ref[...]) * (1.0 + sc) + sh
    h_hi, h_lo = _split_bf16(h)
    h_ref[0] = h_hi

    logit = _dot(h_hi, rw_ref[0]) + _dot(h_lo, rw_ref[0]) + _dot(h_hi, rw_ref[1]) + rb_ref[...]
    lane = lax.broadcasted_iota(jnp.int32, (1, ROUTE_LANES), 1)
    lane_f = lane.astype(F32)
    neg = -1e30
    is_g = lane < N_GROUPS
    is_e = (lane >= N_GROUPS) & (lane < N_GROUPS + N_EXPERTS)
    lg = jnp.where(is_g, logit, neg)
    g_max = jnp.max(lg, axis=-1, keepdims=True)
    g_den = jnp.sum(jnp.where(is_g, jnp.exp(lg - g_max), 0.0), axis=-1, keepdims=True)
    g_w = 1.0 / g_den
    g_top = jnp.min(jnp.where(is_g & (logit == g_max), lane_f, 1e9), axis=-1, keepdims=True)
    e_grp = ((lane - N_GROUPS) >> 3).astype(F32)
    in_grp = is_e & (e_grp == g_top)
    le = jnp.where(in_grp, logit, neg)
    t1 = jnp.max(le, axis=-1, keepdims=True)
    i1 = jnp.min(jnp.where(in_grp & (le == t1), lane_f, 1e9), axis=-1, keepdims=True)
    le2 = jnp.where(lane_f == i1, neg, le)
    t2 = jnp.max(le2, axis=-1, keepdims=True)
    i2 = jnp.min(jnp.where(in_grp & (le2 == t2), lane_f, 1e9), axis=-1, keepdims=True)
    e21 = jnp.exp(t2 - t1)
    w1 = g_w / (1.0 + e21)
    w2 = w1 * e21
    route = jnp.where(lane == 0, i1 - N_GROUPS,
                      jnp.where(lane == 1, i2 - N_GROUPS,
                                jnp.where(lane == 2, w1, jnp.where(lane == 3, w2, 0.0))))
    route_ref[0] = route


def out_proj(x, o_gla, o_fft, o_mla, mod, w_out, norm_g, rw, rb, tm):
    B, S, D = x.shape
    row = lambda b, s: (b, s, 0)
    c2 = lambda b, s: (0, 0)
    return pl.pallas_call(
        _out_proj_kernel,
        grid=(B, S // tm),
        in_specs=[
            pl.BlockSpec((1, tm, D), row),
            pl.BlockSpec((1, tm, GLA_W), row),
            pl.BlockSpec((tm, FNET_W), lambda b, s: (s, b)),
            pl.BlockSpec((1, tm, MLA_W), row),
            pl.BlockSpec((1, 6, D), lambda b, s: (b, 0, 0)),
            pl.BlockSpec((D, D), c2),
            pl.BlockSpec((1, D), c2),
            pl.BlockSpec((2, D, ROUTE_LANES), lambda b, s: (0, 0, 0)),
            pl.BlockSpec((1, ROUTE_LANES), c2),
        ],
        out_specs=(pl.BlockSpec((1, tm, D), row), pl.BlockSpec((1, tm, D), row),
                   pl.BlockSpec((1, tm, ROUTE_LANES), row)),
        out_shape=(jax.ShapeDtypeStruct((B, S, D), F32), jax.ShapeDtypeStruct((B, S, D), BF16),
                   jax.ShapeDtypeStruct((B, S, ROUTE_LANES), F32)),
        compiler_params=_cparams(("parallel", "parallel")),
        name="out_proj",
    )(x, o_gla, o_fft, o_mla, mod, w_out, norm_g, rw, rb)


MOE_ROWS = 512


def _moe_kernel(blk_e_ref, n_used_ref, x_ref, w1_ref, w3_ref, w2_ref, o_ref):
    i = pl.program_id(0)

    @pl.when(i < n_used_ref[0])
    def _():
        x = x_ref[...]
        a = _dot(x, w1_ref[0])
        b = _dot(x, w3_ref[0])
        hm = (_silu(a) * b).astype(BF16)
        o_ref[...] = _dot(hm, w2_ref[0]).astype(o_ref.dtype)

    @pl.when(i >= n_used_ref[0])
    def _():
        o_ref[...] = jnp.zeros_like(o_ref)


def moe_mlp(xb, blk_e, n_used, w1, w3, w2):
    P, D = xb.shape
    n_blk = P // MOE_ROWS
    wmap = lambda i, be, nu: (be[i], 0, 0)
    grid_spec = pltpu.PrefetchScalarGridSpec(
        num_scalar_prefetch=2,
        grid=(n_blk,),
        in_specs=[
            pl.BlockSpec((MOE_ROWS, D), lambda i, be, nu: (i, 0)),
            pl.BlockSpec((1, D, D_EXPERT), wmap),
            pl.BlockSpec((1, D, D_EXPERT), wmap),
            pl.BlockSpec((1, D_EXPERT, D), wmap),
        ],
        out_specs=pl.BlockSpec((MOE_ROWS, D), lambda i, be, nu: (i, 0)),
    )
    return pl.pallas_call(
        _moe_kernel,
        grid_spec=grid_spec,
        out_shape=jax.ShapeDtypeStruct((P, D), BF16),
        compiler_params=_cparams(("arbitrary",)),
        name="moe_mlp",
    )(blk_e, n_used, xb, w1, w3, w2)


def _combine_kernel(x_ref, ya_ref, yb_ref, route_ref, mod_ref, o_ref):
    g2 = mod_ref[0, 5:6, :]
    wa = route_ref[0, :, 2:3]
    wb = route_ref[0, :, 3:4]
    y = wa * ya_ref[0].astype(F32) + wb * yb_ref[0].astype(F32)
    o_ref[0] = x_ref[0] + g2 * y


def combine(x, ya, yb, route, mod, tm):
    B, S, D = x.shape
    row = lambda b, s: (b, s, 0)
    return pl.pallas_call(
        _combine_kernel,
        grid=(B, S // tm),
        in_specs=[
            pl.BlockSpec((1, tm, D), row),
            pl.BlockSpec((1, tm, D), row),
            pl.BlockSpec((1, tm, D), row),
            pl.BlockSpec((1, tm, ROUTE_LANES), row),
            pl.BlockSpec((1, 6, D), lambda b, s: (b, 0, 0)),
        ],
        out_specs=pl.BlockSpec((1, tm, D), row),
        out_shape=jax.ShapeDtypeStruct((B, S, D), F32),
        compiler_params=_cparams(("parallel", "parallel")),
        name="combine",
    )(x, ya, yb, route, mod)


def _prep_params(p):
    f = {}
    w_in = p["w_in"]
    L = w_in.shape[0]
    z = lambda n: jnp.zeros((L, D_MODEL, n), F32)
    f["w_in"] = jnp.concatenate(
        [w_in[:, :, 0:768], w_in[:, :, 800:1056], w_in[:, :, 1056:1312], w_in[:, :, 1312:1440],
         z(MLA_NOPE), w_in[:, :, 1440:1472], z(HEAD_PAD - MLA_QK),
         w_in[:, :, 768:800], z(128 - 2 * GLA_GATE_RANK)], axis=-1).astype(BF16)
    up = jnp.zeros((L, 128, 256), F32)
    up = up.at[:, 0:16, 0:128].set(p["gla_gate_up_f"]).at[:, 16:32, 128:256].set(p["gla_gate_up_b"])
    f["gate_up"] = up.astype(BF16)
    f["gate_bias"] = jnp.concatenate([p["gla_gate_bias_f"], p["gla_gate_bias_b"]], axis=-1)[:, None, :]
    f["lora_g"] = jnp.concatenate([p["mla_q_lora_norm_g"], p["mla_kv_lora_norm_g"]], axis=-1)[:, None, :]
    f["gla_norm_g"] = jnp.tile(p["gla_out_norm_g"], (1, GLA_HEADS))[:, None, :]
    f["norm1_g"] = p["norm1_g"][:, None, :]
    f["norm2_g"] = p["norm2_g"][:, None, :]
    wq = p["mla_w_uq"].reshape(L, MLA_Q_LORA, MLA_HEADS, MLA_QK).transpose(0, 2, 1, 3)
    f["wq"] = jnp.pad(wq, ((0, 0), (0, 0), (0, 0), (0, HEAD_PAD - MLA_QK))).astype(BF16)
    wkv = p["mla_w_ukv"].reshape(L, MLA_KV_LORA, MLA_HEADS, MLA_NOPE + MLA_V).transpose(0, 2, 1, 3)
    f["wk"] = jnp.pad(wkv[..., :MLA_NOPE], ((0, 0), (0, 0), (0, 0), (0, HEAD_PAD - MLA_NOPE))).astype(BF16)
    wv = wkv[..., MLA_NOPE:]
    zv = jnp.zeros_like(wv)
    even = (jnp.arange(MLA_HEADS) % 2 == 0)[None, :, None, None]
    f["wv"] = jnp.where(even, jnp.concatenate([wv, zv], -1), jnp.concatenate([zv, wv], -1)).astype(BF16)
    pad_qk = ((0, 0), (0, HEAD_PAD - MLA_QK))
    f["qg"] = (jnp.pad(p["mla_q_norm_g"], pad_qk) * (MLA_QK ** -0.5))[:, None, :]
    f["kg"] = jnp.pad(p["mla_k_norm_g"], pad_qk)[:, None, :]
    f["w_out"] = p["w_out"].astype(BF16)
    rw = jnp.concatenate([p["router_group_w"], p["router_expert_w"]], axis=-1)
    rw = jnp.pad(rw, ((0, 0), (0, 0), (0, ROUTE_LANES - N_GROUPS - N_EXPERTS)))
    rw_hi = rw.astype(BF16)
    rw_lo = (rw - rw_hi.astype(F32)).astype(BF16)
    f["rw"] = jnp.stack([rw_hi, rw_lo], axis=1)
    rb = jnp.concatenate([p["router_group_b"], p["router_expert_b"]], axis=-1)
    f["rb"] = jnp.pad(rb, ((0, 0), (0, ROUTE_LANES - N_GROUPS - N_EXPERTS)))[:, None, :]
    f["w1"] = p["expert_w1"].astype(BF16)
    f["w3"] = p["expert_w3"].astype(BF16)
    f["w2"] = p["expert_w2"].astype(BF16)
    return f


def _const_tables():
    c = np.arange(FNET_W)
    same = (c[:, None] // FNET_GROUP_DIM) == (c[None, :] // FNET_GROUP_DIM)
    ang = 2.0 * np.pi * ((c[:, None] % FNET_GROUP_DIM) * (c[None, :] % FNET_GROUP_DIM) % FNET_GROUP_DIM) / FNET_GROUP_DIM
    dft64 = np.stack([np.where(same, np.cos(ang), 0.0), np.where(same, np.sin(ang), 0.0)]).astype(np.float32)
    vone = np.zeros((MLA_HEADS, 1, HEAD_PAD), np.float32)
    vone[0::2, 0, MLA_V] = 1.0
    vone[1::2, 0, 0] = 1.0
    return jnp.asarray(dft64, BF16), jnp.asarray(vone)


def _seq_tables(S):
    j = lax.broadcasted_iota(jnp.int32, (S, S), 0)
    k = lax.broadcasted_iota(jnp.int32, (S, S), 1)
    ang = ((j * k) % S).astype(F32) * (2.0 * math.pi / S)
    a_dft = jnp.concatenate([jnp.cos(ang), -jnp.sin(ang)], axis=1).astype(BF16)
    half = MLA_ROPE // 2
    freqs = ROPE_THETA ** (-jnp.arange(half, dtype=F32) / half)
    ra = jnp.arange(S, dtype=F32)[:, None] * freqs[None, :]
    cos, sin = jnp.cos(ra), jnp.sin(ra)
    one = lambda n: jnp.ones((S, n), F32)
    zero = lambda n: jnp.zeros((S, n), F32)
    tail = HEAD_PAD - MLA_QK
    rope = jnp.stack([
        jnp.concatenate([one(MLA_NOPE), cos, cos, one(tail)], axis=1),
        jnp.concatenate([zero(MLA_NOPE), zero(half), sin, zero(tail)], axis=1),
        jnp.concatenate([zero(MLA_NOPE), -sin, zero(half), zero(tail)], axis=1),
    ])
    return a_dft, rope


def _dispatch_plan(e_id):
    T = e_id.shape[0]
    R = T * TOP_K
    flat_e = e_id.reshape(R)
    order = jnp.argsort(flat_e)
    se = flat_e[order]
    counts = jnp.bincount(flat_e, length=N_EXPERTS).astype(jnp.int32)
    padded = (counts + MOE_ROWS - 1) // MOE_ROWS * MOE_ROWS
    pend = jnp.cumsum(padded)
    pstart = pend - padded
    start = jnp.cumsum(counts) - counts
    dest_sorted = pstart[se] + jnp.arange(R, dtype=jnp.int32) - start[se]
    P = R + N_EXPERTS * MOE_ROWS
    n_blk = P // MOE_ROWS
    row_tok = jnp.zeros((P,), jnp.int32).at[dest_sorted].set((order // TOP_K).astype(jnp.int32))
    dest = jnp.zeros((R,), jnp.int32).at[order].set(dest_sorted).reshape(T, TOP_K)
    blk_e = jnp.minimum(jnp.searchsorted(pend, jnp.arange(n_blk, dtype=jnp.int32) * MOE_ROWS, side="right"),
                        N_EXPERTS - 1).astype(jnp.int32)
    n_used = (pend[-1] // MOE_ROWS).astype(jnp.int32).reshape(1)
    return row_tok, dest, blk_e, n_used


def _trunk(x, mod_all, f, consts, tables):
    B, S, D = x.shape
    T = B * S
    dft64, vone = consts
    a_dft, rope = tables
    tm = min(S, 512)
    tq = min(S, 256)
    tk = min(S, 512)
    dft_scale = 1.0 / math.sqrt(FNET_GROUP_DIM * S)
    for l in range(DEPTH):
        mod = mod_all[l].reshape(B, 6, D)
        qkv, og, gate, xcs, cq, ckv, kr = in_proj(
            x, mod, f["norm1_g"][l], f["w_in"][l], dft64, f["gate_up"][l], f["gate_bias"][l], f["lora_g"][l], tm)
        o_gla = gla(qkv, og, gate, f["gla_norm_g"][l])
        o_fft = dft_seq(a_dft, xcs.reshape(2 * S, B * FNET_W), dft_scale)
        o_mla = mla(cq, ckv, kr, rope, f["wq"][l], f["wk"][l], f["wv"][l], f["qg"][l], f["kg"][l], vone, tq, tk)
        x, h, route = out_proj(x, o_gla, o_fft, o_mla, mod, f["w_out"][l], f["norm2_g"][l], f["rw"][l], f["rb"][l], tm)
        e_id = route[:, :, 0:2].reshape(T, TOP_K).astype(jnp.int32)
        row_tok, dest, blk_e, n_used = _dispatch_plan(e_id)
        xb = jnp.take(h.reshape(T, D), row_tok, axis=0)
        yb = moe_mlp(xb, blk_e, n_used, f["w1"][l], f["w3"][l], f["w2"][l])
        ya = jnp.take(yb, dest[:, 0], axis=0).reshape(B, S, D)
        yc = jnp.take(yb, dest[:, 1], axis=0).reshape(B, S, D)
        x = combine(x, ya, yc, route, mod, tm)
    return x


def kernel(x_prompt, x_sample, c_prompt, c_sample, ada_w, ada_b, norm1_g, norm2_g, w_in, w_out, gla_gate_up_f, gla_gate_bias_f, gla_gate_up_b, gla_gate_bias_b, gla_out_norm_g, mla_q_lora_norm_g, mla_w_uq, mla_kv_lora_norm_g, mla_w_ukv, mla_q_norm_g, mla_k_norm_g, router_group_w, router_group_b, router_expert_w, router_expert_b, expert_w1, expert_w3, expert_w2):
    p = dict(norm1_g=norm1_g, norm2_g=norm2_g, w_in=w_in, w_out=w_out, gla_gate_up_f=gla_gate_up_f,
             gla_gate_bias_f=gla_gate_bias_f, gla_gate_up_b=gla_gate_up_b, gla_gate_bias_b=gla_gate_bias_b,
             gla_out_norm_g=gla_out_norm_g, mla_q_lora_norm_g=mla_q_lora_norm_g, mla_w_uq=mla_w_uq,
             mla_kv_lora_norm_g=mla_kv_lora_norm_g, mla_w_ukv=mla_w_ukv, mla_q_norm_g=mla_q_norm_g,
             mla_k_norm_g=mla_k_norm_g, router_group_w=router_group_w, router_group_b=router_group_b,
             router_expert_w=router_expert_w, router_expert_b=router_expert_b, expert_w1=expert_w1,
             expert_w3=expert_w3, expert_w2=expert_w2)
    f = _prep_params(p)
    consts = _const_tables()
    nb_p = c_prompt.shape[0]
    mod_all = ada_modulation(jnp.concatenate([c_prompt, c_sample], axis=0), ada_w, ada_b)
    y_prompt = _trunk(x_prompt, mod_all[:, :nb_p], f, consts, _seq_tables(x_prompt.shape[1]))
    y_sample = _trunk(x_sample, mod_all[:, nb_p:], f, consts, _seq_tables(x_sample.shape[1]))
    return (y_prompt, y_sample)
```

```python
import functools
import math

import jax
import jax.numpy as jnp
import numpy as np
from jax import lax
from jax.experimental import pallas as pl
from jax.experimental.pallas import tpu as pltpu

F32 = jnp.float32
BF16 = jnp.bfloat16

D_MODEL = 1024
DEPTH = 4
EPS = 1e-6

GLA_HEADS = 4
GLA_DK = 32
GLA_DV = 64
GLA_QK_W = GLA_HEADS * GLA_DK
GLA_W = GLA_HEADS * GLA_DV
GLA_GATE_RANK = 16
GLA_GATE_NORMALIZER = 16.0
GLA_CHUNK = 32
GLA_BLOCK = 128

FNET_GROUPS = 4
FNET_GROUP_DIM = 64
FNET_W = 256

MLA_HEADS = 8
MLA_Q_LORA = 256
MLA_KV_LORA = 128
MLA_NOPE = 64
MLA_ROPE = 32
MLA_V = 64
MLA_QK = 96
MLA_W = 512
ROPE_THETA = 10000.0
HEAD_PAD = 128

N_GROUPS = 4
EXPERTS_PER_GROUP = 8
N_EXPERTS = 32
TOP_K = 2
D_EXPERT = 512
ROUTE_LANES = 128

C_Q, C_K, C_V, C_OG, C_F, C_CQ, C_CKV, C_KR, C_GATE = 0, 128, 256, 512, 768, 1024, 1280, 1408, 1536
P_IN_PAD = 1664

VMEM_LIMIT_BYTES = 56 * 1024 * 1024


def _cparams(sem, vmem=None):
    return pltpu.CompilerParams(dimension_semantics=sem, vmem_limit_bytes=vmem or VMEM_LIMIT_BYTES)


def _silu(x):
    return x * (1.0 / (1.0 + jnp.exp(-x)))


def _log_sigmoid(x):
    return -(jnp.maximum(-x, 0.0) + jnp.log1p(jnp.exp(-jnp.abs(x))))


def _dot(a, b):
    return jnp.dot(a, b, preferred_element_type=F32)


def _dot_nt(a, b):
    return lax.dot_general(a, b, (((1,), (1,)), ((), ())), preferred_element_type=F32)


def _split_bf16(x):
    hi = x.astype(BF16)
    lo = (x - hi.astype(F32)).astype(BF16)
    return hi, lo


def _pack_halves(x):
    n = x.shape[-1] // 2
    lo = lax.shift_right_logical(lax.bitcast_convert_type(x[:, :n], jnp.int32), 16)
    hi = lax.bitcast_convert_type(x[:, n:], jnp.int32) & jnp.int32(-65536)
    return hi | lo


def _unpack_halves(w):
    lo = lax.bitcast_convert_type(lax.shift_left(w, 16), F32)
    hi = lax.bitcast_convert_type(w & jnp.int32(-65536), F32)
    return lo, hi


def _ada_kernel(c_ref, w_ref, b_ref, o_ref):
    c = _silu(c_ref[...]).astype(BF16)
    o_ref[0] = _dot(c, w_ref[0].astype(BF16)) + b_ref[0]


def ada_modulation(c_all, ada_w, ada_b):
    nb = c_all.shape[0]
    tn = 1536
    n = ada_w.shape[-1]
    return pl.pallas_call(
        _ada_kernel,
        grid=(DEPTH, n // tn),
        in_specs=[
            pl.BlockSpec((nb, D_MODEL), lambda l, j: (0, 0)),
            pl.BlockSpec((1, D_MODEL, tn), lambda l, j: (l, 0, j)),
            pl.BlockSpec((1, 1, tn), lambda l, j: (l, 0, j)),
        ],
        out_specs=pl.BlockSpec((1, nb, tn), lambda l, j: (l, 0, j)),
        out_shape=jax.ShapeDtypeStruct((DEPTH, nb, n), F32),
        compiler_params=_cparams(("parallel", "parallel")),
        name="ada_modulation",
    )(c_all, ada_w, ada_b.reshape(DEPTH, 1, n))


def _in_proj_kernel(x_ref, mod_ref, g_ref, w_ref, dft_ref, up_ref, gb_ref, lg_ref,
                    qkv_ref, og_ref, gate_ref, xcs_ref, cq_ref, ckv_ref, kr_ref):
    x = x_ref[0]
    sh = mod_ref[0, 0:1, :]
    sc = mod_ref[0, 1:2, :]
    r = lax.rsqrt(jnp.mean(x * x, axis=-1, keepdims=True) + EPS)
    h = (x * r * g_ref[...]) * (1.0 + sc) + sh
    u = _dot(h.astype(BF16), w_ref[...])

    q = u[:, C_Q:C_K] * (GLA_DK ** -0.5)
    qkv_ref[0, :, 0:128] = q.astype(BF16)
    qkv_ref[0, :, 128:512] = u[:, C_K:C_OG].astype(BF16)
    og_ref[0] = u[:, C_OG:C_F].astype(BF16)

    ug = u[:, C_GATE:C_GATE + 128].astype(BF16)
    gl = _dot(ug, up_ref[...]) + gb_ref[...]
    gate_ref[0] = _log_sigmoid(gl) * (1.0 / GLA_GATE_NORMALIZER)

    uf = u[:, C_F:C_CQ].astype(BF16)
    xcs_ref[0] = _dot(uf, dft_ref[0]).astype(BF16)
    xcs_ref[1] = _dot(uf, dft_ref[1]).astype(BF16)

    cq = u[:, C_CQ:C_CKV]
    rq = lax.rsqrt(jnp.mean(cq * cq, axis=-1, keepdims=True) + EPS)
    cq_ref[0] = (cq * rq * lg_ref[:, 0:256]).astype(BF16)
    ckv = u[:, C_CKV:C_KR]
    rkv = lax.rsqrt(jnp.mean(ckv * ckv, axis=-1, keepdims=True) + EPS)
    ckv_ref[0] = (ckv * rkv * lg_ref[:, 256:384]).astype(BF16)
    kr_ref[0] = u[:, C_KR:C_GATE].astype(BF16)


def in_proj(x, mod, norm_g, w_in, dft64, gate_up, gate_bias, lora_g, tm):
    B, S, D = x.shape
    ns = S // tm
    row = lambda b, s: (b, s, 0)
    const2 = lambda b, s: (0, 0)
    out_shapes = (
        jax.ShapeDtypeStruct((B, S, 512), BF16),
        jax.ShapeDtypeStruct((B, S, GLA_W), BF16),
        jax.ShapeDtypeStruct((B, S, 256), F32),
        jax.ShapeDtypeStruct((2, S, B * FNET_W), BF16),
        jax.ShapeDtypeStruct((B, S, MLA_Q_LORA), BF16),
        jax.ShapeDtypeStruct((B, S, MLA_KV_LORA), BF16),
        jax.ShapeDtypeStruct((B, S, HEAD_PAD), BF16),
    )
    return pl.pallas_call(
        _in_proj_kernel,
        grid=(B, ns),
        in_specs=[
            pl.BlockSpec((1, tm, D), row),
            pl.BlockSpec((1, 6, D), lambda b, s: (b, 0, 0)),
            pl.BlockSpec((1, D), const2),
            pl.BlockSpec((D, P_IN_PAD), const2),
            pl.BlockSpec((2, FNET_W, FNET_W), lambda b, s: (0, 0, 0)),
            pl.BlockSpec((128, 256), const2),
            pl.BlockSpec((1, 256), const2),
            pl.BlockSpec((1, 384), const2),
        ],
        out_specs=(
            pl.BlockSpec((1, tm, 512), row),
            pl.BlockSpec((1, tm, GLA_W), row),
            pl.BlockSpec((1, tm, 256), row),
            pl.BlockSpec((2, tm, FNET_W), lambda b, s: (0, s, b)),
            pl.BlockSpec((1, tm, MLA_Q_LORA), row),
            pl.BlockSpec((1, tm, MLA_KV_LORA), row),
            pl.BlockSpec((1, tm, HEAD_PAD), row),
        ),
        out_shape=out_shapes,
        compiler_params=_cparams(("parallel", "parallel")),
        name="in_proj",
    )(x, mod, norm_g, w_in, dft64, gate_up, gate_bias, lora_g)


def _gla_kernel(qkv_ref, og_ref, gate_ref, ng_ref, o_ref, of_s, st_s, *, seq):
    R, C = GLA_BLOCK, GLA_CHUNK
    n_sub = R // C
    nblk = seq // R
    shift_c = int(math.log2(C))

    row = lax.broadcasted_iota(jnp.int32, (R, R), 0)
    col = lax.broadcasted_iota(jnp.int32, (R, R), 1)
    same = (row >> shift_c) == (col >> shift_c)
    ones_blk = jnp.where(same, 1.0, 0.0)
    lane_qk = lax.broadcasted_iota(jnp.int32, (1, GLA_QK_W), 1)
    lane_v = lax.broadcasted_iota(jnp.int32, (1, GLA_W), 1)
    head_qk = [(lane_qk >> 5) == h for h in range(GLA_HEADS)]
    head_v = [(lane_v >> 6) == h for h in range(GLA_HEADS)]
    st_row = lax.broadcasted_iota(jnp.int32, (GLA_W, GLA_QK_W), 0)
    st_col = lax.broadcasted_iota(jnp.int32, (GLA_W, GLA_QK_W), 1)
    st_mask = (st_row >> 6) == (st_col >> 5)
    sub_row = lax.broadcasted_iota(jnp.int32, (R, 1), 0) >> shift_c
    nrow = lax.broadcasted_iota(jnp.int32, (GLA_W, GLA_W), 0)
    ncol = lax.broadcasted_iota(jnp.int32, (GLA_W, GLA_W), 1)
    head_mean = jnp.where((nrow >> 6) == (ncol >> 6), 1.0 / GLA_DV, 0.0).astype(BF16)

    def run_direction(backward):
        if backward:
            tri = same & (col >= row)
            att_ok = same & (col > row)
            gate_off = GLA_QK_W
            order = list(range(n_sub - 1, -1, -1))
        else:
            tri = same & (col <= row)
            att_ok = same & (col <= row)
            gate_off = 0
            order = list(range(n_sub))
        cum_lhs = jnp.concatenate([jnp.where(tri, 1.0, 0.0), ones_blk], axis=0).astype(BF16)
        att_ok4 = jnp.concatenate([att_ok] * GLA_HEADS, axis=1)

        st_s[...] = jnp.zeros_like(st_s)

        def body(i, carry):
            j = (nblk - 1 - i) if backward else i
            r0 = pl.multiple_of(j * R, R)
            qkv = qkv_ref[0, pl.ds(r0, R), :]
            q = qkv[:, 0:128].astype(F32)
            k = qkv[:, 128:256].astype(F32)
            v = qkv[:, 256:512]
            g = gate_ref[0, pl.ds(r0, R), gate_off:gate_off + GLA_QK_W]
            g_hi, g_lo = _split_bf16(g)
            cs = _dot(cum_lhs, jnp.concatenate([g_hi, g_lo], axis=1))
            b = cs[0:R, 0:128] + cs[0:R, 128:256]
            bl = cs[R:2 * R, 0:128] + cs[R:2 * R, 128:256]
            q_dec = q * jnp.exp(b)
            k_inv = (k * jnp.exp(-b)).astype(BF16)
            k_end = k * jnp.exp(bl - b)
            decay = jnp.exp(bl)

            att = jnp.concatenate(
                [_dot_nt(jnp.where(head_qk[h], q_dec, 0.0).astype(BF16), k_inv) for h in range(GLA_HEADS)],
                axis=1)
            att = jnp.where(att_ok4, att, 0.0).astype(BF16)
            v_heads = jnp.concatenate([jnp.where(head_v[h], v, jnp.zeros_like(v)) for h in range(GLA_HEADS)], axis=0)
            o_blk = _dot(att, v_heads)

            v_t = v.astype(F32).T.astype(BF16)
            q_dec_b = q_dec.astype(BF16)
            inter = [None] * n_sub
            for c in order:
                st = st_s[...]
                inter[c] = _dot_nt(q_dec_b[c * C:(c + 1) * C, :], st.astype(BF16))
                k_c = jnp.where(sub_row == c, k_end, 0.0).astype(BF16)
                d_st = _dot(v_t, k_c)
                st_s[...] = st * decay[c * C:c * C + 1, :] + jnp.where(st_mask, d_st, 0.0)
            o_blk = o_blk + jnp.concatenate(inter, axis=0)

            if not backward:
                of_s[pl.ds(r0, R), :] = o_blk
            else:
                o = of_s[pl.ds(r0, R), :] + o_blk
                sq_hi, sq_lo = _split_bf16(o * o)
                ms = _dot(sq_hi, head_mean) + _dot(sq_lo, head_mean)
                y = o * lax.rsqrt(ms + EPS) * ng_ref[...]
                og = og_ref[0, pl.ds(r0, R), :].astype(F32)
                o_ref[0, pl.ds(r0, R), :] = (y * _silu(og)).astype(o_ref.dtype)
            return carry

        lax.fori_loop(0, nblk, body, 0)

    run_direction(False)
    run_direction(True)


def gla(qkv, og, gate, norm_g4):
    B, S, _ = qkv.shape
    full = lambda b: (b, 0, 0)
    return pl.pallas_call(
        functools.partial(_gla_kernel, seq=S),
        grid=(B,),
        in_specs=[
            pl.BlockSpec((1, S, 512), full),
            pl.BlockSpec((1, S, GLA_W), full),
            pl.BlockSpec((1, S, 256), full),
            pl.BlockSpec((1, GLA_W), lambda b: (0, 0)),
        ],
        out_specs=pl.BlockSpec((1, S, GLA_W), full),
        out_shape=jax.ShapeDtypeStruct((B, S, GLA_W), BF16),
        scratch_shapes=[pltpu.VMEM((S, GLA_W), F32), pltpu.VMEM((GLA_W, GLA_QK_W), F32)],
        compiler_params=_cparams(("parallel",)),
        name="gla",
    )(qkv, og, gate, norm_g4)


def _dft_kernel(a_ref, b_ref, o_ref, acc_ref, *, scale):
    kk = pl.program_id(2)

    @pl.when(kk == 0)
    def _():
        acc_ref[...] = jnp.zeros_like(acc_ref)

    acc_ref[...] += _dot(a_ref[...], b_ref[...])

    @pl.when(kk == pl.num_programs(2) - 1)
    def _():
        o_ref[...] = (acc_ref[...] * scale).astype(o_ref.dtype)


def dft_seq(a_dft, xcs, scale):
    m, k = a_dft.shape
    n = xcs.shape[1]
    tm, tn, tk = min(m, 1024), min(n, 1024), min(k, 2048)
    return pl.pallas_call(
        functools.partial(_dft_kernel, scale=scale),
        grid=(m // tm, n // tn, k // tk),
        in_specs=[pl.BlockSpec((tm, tk), lambda i, j, kk: (i, kk)),
                  pl.BlockSpec((tk, tn), lambda i, j, kk: (kk, j))],
        out_specs=pl.BlockSpec((tm, tn), lambda i, j, kk: (i, j)),
        out_shape=jax.ShapeDtypeStruct((m, n), BF16),
        scratch_shapes=[pltpu.VMEM((tm, tn), F32)],
        compiler_params=_cparams(("parallel", "parallel", "arbitrary")),
        name="dft_seq",
    )(a_dft, xcs)


def _rope(x, cos, sin_up, sin_dn):
    return x * cos + pltpu.roll(x, 16, 1) * sin_up + pltpu.roll(x, 112, 1) * sin_dn


SHIFT_LANE = HEAD_PAD - 1
MAX_STATIC_SHIFT = 40.0
LOG2E = math.log2(math.e)


def _mla_kernel(flag_ref, cq_ref, ckv_ref, kr_ref, rope_ref, wq_ref, wk_ref, wv_ref, qg_ref, kg_ref, vone_ref,
                shift_ref, o_ref, kt_s, v_s, q_s, acc_s, *, seq, tq, tk):
    qi = pl.program_id(1)
    inv_qk = 1.0 / MLA_QK
    lane = lax.broadcasted_iota(jnp.int32, (1, HEAD_PAD), 1)

    @pl.when(qi == 0)
    def _():
        one_lane = jnp.where(lane == SHIFT_LANE, 1.0, 0.0)

        def rows(i, carry):
            r0 = pl.multiple_of(i * tk, tk)
            ckv = ckv_ref[0, pl.ds(r0, tk), :]
            kr = kr_ref[0, pl.ds(r0, tk), :].astype(F32)
            cos = rope_ref[0, pl.ds(r0, tk), :]
            sin_up = rope_ref[1, pl.ds(r0, tk), :]
            sin_dn = rope_ref[2, pl.ds(r0, tk), :]
            for h in range(MLA_HEADS):
                kp = _dot(ckv, wk_ref[h]) + kr
                r = lax.rsqrt(jnp.sum(kp * kp, axis=-1, keepdims=True) * inv_qk + EPS)
                kn = _rope(kp * r * kg_ref[...], cos, sin_up, sin_dn) + one_lane
                kt_s[h, i] = kn.T.astype(BF16)
                v_s[h, pl.ds(r0, tk), :] = (_dot(ckv, wv_ref[h]) + vone_ref[h]).astype(BF16)
            return carry

        lax.fori_loop(0, seq // tk, rows, 0)

    q0 = pl.multiple_of(qi * tq, tq)
    cos = rope_ref[0, pl.ds(q0, tq), :]
    sin_up = rope_ref[1, pl.ds(q0, tq), :]
    sin_dn = rope_ref[2, pl.ds(q0, tq), :]
    for h in range(MLA_HEADS):
        qp = _dot(cq_ref[0], wq_ref[h])
        r = lax.rsqrt(jnp.sum(qp * qp, axis=-1, keepdims=True) * inv_qk + EPS)
        q_s[h] = (_rope(qp * r * qg_ref[...], cos, sin_up, sin_dn) + shift_ref[...]).astype(BF16)

    def finish():
        for pair in range(MLA_HEADS // 2):
            outs = []
            for h in (2 * pair, 2 * pair + 1):
                acc = acc_s[h]
                den_lane = MLA_V if h % 2 == 0 else 0
                den = jnp.sum(jnp.where(lane == den_lane, acc, 0.0), axis=-1, keepdims=True)
                outs.append(acc * (1.0 / den))
            both = jnp.where(lane < MLA_V, outs[0], outs[1])
            o_ref[0, :, pair * HEAD_PAD:(pair + 1) * HEAD_PAD] = both.astype(o_ref.dtype)

    @pl.when(flag_ref[0] == 1)
    def _():
        acc_s[...] = jnp.zeros_like(acc_s)

        def kv_step(j, carry):
            k0 = pl.multiple_of(j * tk, tk)
            for h in range(MLA_HEADS):
                p = jnp.exp2(_dot(q_s[h], kt_s[h, j])).astype(BF16)
                acc_s[h] += _dot(p, v_s[h, pl.ds(k0, tk), :])
            return carry

        lax.fori_loop(0, seq // tk, kv_step, 0)
        finish()

    @pl.when(flag_ref[0] == 0)
    def _():
        for h in range(MLA_HEADS):
            def kv_step(j, carry, h=h):
                m, acc = carry
                k0 = pl.multiple_of(j * tk, tk)
                s = _dot(q_s[h], kt_s[h, j])
                m_new = jnp.maximum(m, jnp.max(s, axis=-1, keepdims=True))
                p = jnp.exp2(s - m_new).astype(BF16)
                return m_new, jnp.exp2(m - m_new) * acc + _dot(p, v_s[h, pl.ds(k0, tk), :])

            m0 = jnp.full((tq, 1), -jnp.inf, F32)
            _, acc = lax.fori_loop(0, seq // tk, kv_step, (m0, jnp.zeros((tq, HEAD_PAD), F32)))
            acc_s[h] = acc
        finish()


def mla(flag, cq, ckv, kr, rope, wq, wk, wv, qg, kg, vone, shift, tq, tk):
    B, S, _ = cq.shape
    c3 = lambda b, q: (0, 0, 0)
    c2 = lambda b, q: (0, 0)
    return pl.pallas_call(
        functools.partial(_mla_kernel, seq=S, tq=tq, tk=tk),
        grid=(B, S // tq),
        in_specs=[
            pl.BlockSpec(memory_space=pltpu.SMEM),
            pl.BlockSpec((1, tq, MLA_Q_LORA), lambda b, q: (b, q, 0)),
            pl.BlockSpec((1, S, MLA_KV_LORA), lambda b, q: (b, 0, 0)),
            pl.BlockSpec((1, S, HEAD_PAD), lambda b, q: (b, 0, 0)),
            pl.BlockSpec((3, S, HEAD_PAD), c3),
            pl.BlockSpec((MLA_HEADS, MLA_Q_LORA, HEAD_PAD), c3),
            pl.BlockSpec((MLA_HEADS, MLA_KV_LORA, HEAD_PAD), c3),
            pl.BlockSpec((MLA_HEADS, MLA_KV_LORA, HEAD_PAD), c3),
            pl.BlockSpec((1, HEAD_PAD), c2),
            pl.BlockSpec((1, HEAD_PAD), c2),
            pl.BlockSpec((MLA_HEADS, 1, HEAD_PAD), c3),
            pl.BlockSpec((1, HEAD_PAD), c2),
        ],
        out_specs=pl.BlockSpec((1, tq, MLA_W), lambda b, q: (b, q, 0)),
        out_shape=jax.ShapeDtypeStruct((B, S, MLA_W), BF16),
        scratch_shapes=[pltpu.VMEM((MLA_HEADS, S // tk, HEAD_PAD, tk), BF16),
                        pltpu.VMEM((MLA_HEADS, S, HEAD_PAD), BF16),
                        pltpu.VMEM((MLA_HEADS, tq, HEAD_PAD), BF16),
                        pltpu.VMEM((MLA_HEADS, tq, HEAD_PAD), F32)],
        compiler_params=_cparams(("parallel", "arbitrary")),
        name="mla",
    )(flag, cq, ckv, kr, rope, wq, wk, wv, qg, kg, vone, shift)


def _out_proj_kernel(x_ref, gla_ref, fft_ref, mla_ref, mod_ref, w_ref, g_ref, rw_ref, rb_ref,
                     xo_ref, h_ref, route_ref, cnt_ref):
    mix = (_dot(gla_ref[0], w_ref[0:256, :]) + _dot(fft_ref[...], w_ref[256:512, :])
           + _dot(mla_ref[0], w_ref[512:1024, :]))
    g1 = mod_ref[0, 2:3, :]
    sh = mod_ref[0, 3:4, :]
    sc = mod_ref[0, 4:5, :]
    x = x_ref[0] + g1 * mix
    xo_ref[0] = x
    r = lax.rsqrt(jnp.mean(x * x, axis=-1, keepdims=True) + EPS)
    h = (x * r * g_ref[...]) * (1.0 + sc) + sh
    h_hi, h_lo = _split_bf16(h)
    h_ref[0] = _pack_halves(h_hi.astype(F32))

    logit = _dot(h_hi, rw_ref[0]) + _dot(h_lo, rw_ref[0]) + _dot(h_hi, rw_ref[1]) + rb_ref[...]
    lane = lax.broadcasted_iota(jnp.int32, (1, ROUTE_LANES), 1)
    lane_f = lane.astype(F32)
    neg = -1e30
    is_g = lane < N_GROUPS
    is_e = (lane >= N_GROUPS) & (lane < N_GROUPS + N_EXPERTS)
    lg = jnp.where(is_g, logit, neg)
    g_max = jnp.max(lg, axis=-1, keepdims=True)
    g_den = jnp.sum(jnp.where(is_g, jnp.exp(lg - g_max), 0.0), axis=-1, keepdims=True)
    g_w = 1.0 / g_den
    g_top = jnp.min(jnp.where(is_g & (logit == g_max), lane_f, 1e9), axis=-1, keepdims=True)
    e_grp = ((lane - N_GROUPS) >> 3).astype(F32)
    in_grp = is_e & (e_grp == g_top)
    le = jnp.where(in_grp, logit, neg)
    t1 = jnp.max(le, axis=-1, keepdims=True)
    i1 = jnp.min(jnp.where(in_grp & (le == t1), lane_f, 1e9), axis=-1, keepdims=True)
    le2 = jnp.where(lane_f == i1, neg, le)
    t2 = jnp.max(le2, axis=-1, keepdims=True)
    i2 = jnp.min(jnp.where(in_grp & (le2 == t2), lane_f, 1e9), axis=-1, keepdims=True)
    e21 = jnp.exp(t2 - t1)
    w1 = g_w / (1.0 + e21)
    w2 = w1 * e21
    route = jnp.where(lane == 0, i1 - N_GROUPS,
                      jnp.where(lane == 1, i2 - N_GROUPS,
                                jnp.where(lane == 2, w1, jnp.where(lane == 3, w2, 0.0))))
    route_ref[0] = route

    @pl.when((pl.program_id(0) == 0) & (pl.program_id(1) == 0))
    def _():
        cnt_ref[...] = jnp.zeros_like(cnt_ref)

    picked = jnp.where((lane_f == i1 - N_GROUPS) | (lane_f == i2 - N_GROUPS), 1.0, 0.0)
    cnt_ref[...] += jnp.sum(picked, axis=0, keepdims=True)


def out_proj(x, o_gla, o_fft, o_mla, mod, w_out, norm_g, rw, rb, tm):
    B, S, D = x.shape
    row = lambda b, s: (b, s, 0)
    c2 = lambda b, s: (0, 0)
    return pl.pallas_call(
        _out_proj_kernel,
        grid=(B, S // tm),
        in_specs=[
            pl.BlockSpec((1, tm, D), row),
            pl.BlockSpec((1, tm, GLA_W), row),
            pl.BlockSpec((tm, FNET_W), lambda b, s: (s, b)),
            pl.BlockSpec((1, tm, MLA_W), row),
            pl.BlockSpec((1, 6, D), lambda b, s: (b, 0, 0)),
            pl.BlockSpec((D, D), c2),
            pl.BlockSpec((1, D), c2),
            pl.BlockSpec((2, D, ROUTE_LANES), lambda b, s: (0, 0, 0)),
            pl.BlockSpec((1, ROUTE_LANES), c2),
        ],
        out_specs=(pl.BlockSpec((1, tm, D), row), pl.BlockSpec((1, tm, D // 2), row),
                   pl.BlockSpec((1, tm, ROUTE_LANES), row), pl.BlockSpec((1, ROUTE_LANES), c2)),
        out_shape=(jax.ShapeDtypeStruct((B, S, D), F32), jax.ShapeDtypeStruct((B, S, D // 2), jnp.int32),
                   jax.ShapeDtypeStruct((B, S, ROUTE_LANES), F32), jax.ShapeDtypeStruct((1, ROUTE_LANES), F32)),
        compiler_params=_cparams(("arbitrary", "arbitrary")),
        name="out_proj",
    )(x, o_gla, o_fft, o_mla, mod, w_out, norm_g, rw, rb)


RANK_ROWS = 512


def _rank_kernel(route_ref, start_ref, dest_ref, carry_ref):
    @pl.when(pl.program_id(0) == 0)
    def _():
        carry_ref[...] = jnp.zeros_like(carry_ref)

    n = route_ref.shape[0]
    lane_f = lax.broadcasted_iota(jnp.int32, (1, ROUTE_LANES), 1).astype(F32)
    e1 = route_ref[:, 0:1]
    e2 = route_ref[:, 1:2]
    oh1 = jnp.where(lane_f == e1, 1.0, 0.0)
    oh2 = jnp.where(lane_f == e2, 1.0, 0.0)
    both = (oh1 + oh2).astype(BF16)
    row = lax.broadcasted_iota(jnp.int32, (n, n), 0)
    col = lax.broadcasted_iota(jnp.int32, (n, n), 1)
    before = jnp.where(col < row, 1.0, 0.0).astype(BF16)
    pos = _dot(before, both) + carry_ref[...] + start_ref[...]
    d1 = jnp.sum(oh1 * pos, axis=-1, keepdims=True)
    d2 = jnp.sum(oh2 * pos, axis=-1, keepdims=True)
    lane = lax.broadcasted_iota(jnp.int32, (1, ROUTE_LANES), 1)
    dest_ref[...] = jnp.where(lane == 0, d1, jnp.where(lane == 1, d2, 0.0)).astype(jnp.int32)
    carry_ref[...] += jnp.sum(oh1 + oh2, axis=0, keepdims=True)


def route_rank(route2d, start):
    T = route2d.shape[0]
    rr = min(T, RANK_ROWS)
    return pl.pallas_call(
        _rank_kernel,
        grid=(T // rr,),
        in_specs=[pl.BlockSpec((rr, ROUTE_LANES), lambda i: (i, 0)),
                  pl.BlockSpec((1, ROUTE_LANES), lambda i: (0, 0))],
        out_specs=pl.BlockSpec((rr, ROUTE_LANES), lambda i: (i, 0)),
        out_shape=jax.ShapeDtypeStruct((T, ROUTE_LANES), jnp.int32),
        scratch_shapes=[pltpu.VMEM((1, ROUTE_LANES), F32)],
        compiler_params=_cparams(("arbitrary",)),
        name="route_rank",
    )(route2d, start)


MOE_ROWS = 512


def _moe_kernel(blk_e_ref, n_used_ref, x_ref, w1_ref, w3_ref, w2_ref, o_ref):
    i = pl.program_id(0)

    @pl.when(i < n_used_ref[0])
    def _():
        half = D_MODEL // 2
        x_lo, x_hi = _unpack_halves(x_ref[...])
        x_lo = x_lo.astype(BF16)
        x_hi = x_hi.astype(BF16)
        a = _dot(x_lo, w1_ref[0, 0:half, :]) + _dot(x_hi, w1_ref[0, half:, :])
        b = _dot(x_lo, w3_ref[0, 0:half, :]) + _dot(x_hi, w3_ref[0, half:, :])
        hm = (_silu(a) * b).astype(BF16)
        y = _dot(hm, w2_ref[0])
        o_ref[...] = _pack_halves(y.astype(BF16).astype(F32))

    @pl.when(i >= n_used_ref[0])
    def _():
        o_ref[...] = jnp.zeros_like(o_ref)


def moe_mlp(xb, blk_e, n_used, w1, w3, w2):
    P, half = xb.shape
    n_blk = P // MOE_ROWS
    wmap = lambda i, be, nu: (be[i], 0, 0)
    grid_spec = pltpu.PrefetchScalarGridSpec(
        num_scalar_prefetch=2,
        grid=(n_blk,),
        in_specs=[
            pl.BlockSpec((MOE_ROWS, half), lambda i, be, nu: (i, 0)),
            pl.BlockSpec((1, D_MODEL, D_EXPERT), wmap),
            pl.BlockSpec((1, D_MODEL, D_EXPERT), wmap),
            pl.BlockSpec((1, D_EXPERT, D_MODEL), wmap),
        ],
        out_specs=pl.BlockSpec((MOE_ROWS, half), lambda i, be, nu: (i, 0)),
    )
    return pl.pallas_call(
        _moe_kernel,
        grid_spec=grid_spec,
        out_shape=jax.ShapeDtypeStruct((P, half), jnp.int32),
        compiler_params=_cparams(("arbitrary",)),
        name="moe_mlp",
    )(blk_e, n_used, xb, w1, w3, w2)


ROW_TILE = 512
ROW_UNROLL = 8


def _dispatch_kernel(dest_ref, h_ref, init_ref, xb_ref, sem):
    del init_ref
    n = h_ref.shape[0]

    def issue(t, carry):
        for u in range(ROW_UNROLL):
            r = t * ROW_UNROLL + u
            for c in range(TOP_K):
                d = dest_ref[0, 0, TOP_K * r + c]
                pltpu.make_async_copy(h_ref.at[pl.ds(r, 1)], xb_ref.at[pl.ds(d, 1)], sem).start()
        return carry

    lax.fori_loop(0, n // ROW_UNROLL, issue, 0)
    for c in range(TOP_K):
        pltpu.make_async_copy(h_ref, xb_ref.at[pl.ds(0, n)], sem).wait()


def dispatch_rows(hp, dest, n_rows):
    T, half = hp.shape
    rt = min(T, ROW_TILE)
    dest2 = dest.reshape(T // rt, 1, TOP_K * rt)
    init = jnp.zeros((n_rows, half), jnp.int32)
    return pl.pallas_call(
        _dispatch_kernel,
        grid=(T // rt,),
        in_specs=[
            pl.BlockSpec((1, 1, TOP_K * rt), lambda i: (i, 0, 0), memory_space=pltpu.SMEM),
            pl.BlockSpec((rt, half), lambda i: (i, 0)),
            pl.BlockSpec(memory_space=pl.ANY),
        ],
        out_specs=pl.BlockSpec(memory_space=pl.ANY),
        out_shape=jax.ShapeDtypeStruct((n_rows, half), jnp.int32),
        scratch_shapes=[pltpu.SemaphoreType.DMA(())],
        input_output_aliases={2: 0},
        compiler_params=_cparams(("arbitrary",)),
        name="dispatch_rows",
    )(dest2, hp, init)


def _combine_kernel(dest_ref, dest_next_ref, x_ref, route_ref, mod_ref, y_ref, o_ref, buf, sem):
    i = pl.program_id(0)
    n_steps = pl.num_programs(0)
    n = x_ref.shape[0]
    slot = i % 2

    def gather(d_ref, s):
        def issue(t, carry):
            for u in range(ROW_UNROLL):
                r = t * ROW_UNROLL + u
                for c in range(TOP_K):
                    d = d_ref[0, 0, TOP_K * r + c]
                    pltpu.make_async_copy(y_ref.at[pl.ds(d, 1)], buf.at[s, c, pl.ds(r, 1)], sem.at[s]).start()
            return carry

        lax.fori_loop(0, n // ROW_UNROLL, issue, 0)

    @pl.when(i == 0)
    def _():
        gather(dest_ref, 0)

    @pl.when(i + 1 < n_steps)
    def _():
        gather(dest_next_ref, 1 - slot)

    for c in range(TOP_K):
        pltpu.make_async_copy(y_ref.at[pl.ds(0, n)], buf.at[slot, c], sem.at[slot]).wait()

    g2 = mod_ref[0, 5:6, :]
    wa = route_ref[:, 2:3]
    wb = route_ref[:, 3:4]
    half = D_MODEL // 2
    a_lo, a_hi = _unpack_halves(buf[slot, 0])
    b_lo, b_hi = _unpack_halves(buf[slot, 1])
    o_ref[:, 0:half] = x_ref[:, 0:half] + g2[:, 0:half] * (wa * a_lo + wb * b_lo)
    o_ref[:, half:] = x_ref[:, half:] + g2[:, half:] * (wa * a_hi + wb * b_hi)


def combine(x, yb, dest, route, mod):
    B, S, D = x.shape
    T = B * S
    rt = min(S, ROW_TILE)
    n_steps = T // rt
    per_b = S // rt
    dest2 = dest.reshape(n_steps, 1, TOP_K * rt)
    out = pl.pallas_call(
        _combine_kernel,
        grid=(n_steps,),
        in_specs=[
            pl.BlockSpec((1, 1, TOP_K * rt), lambda i: (i, 0, 0), memory_space=pltpu.SMEM),
            pl.BlockSpec((1, 1, TOP_K * rt), lambda i: (jnp.minimum(i + 1, n_steps - 1), 0, 0),
                         memory_space=pltpu.SMEM),
            pl.BlockSpec((rt, D), lambda i: (i, 0)),
            pl.BlockSpec((rt, ROUTE_LANES), lambda i: (i, 0)),
            pl.BlockSpec((1, 6, D), lambda i: (i // per_b, 0, 0)),
            pl.BlockSpec(memory_space=pl.ANY),
        ],
        out_specs=pl.BlockSpec((rt, D), lambda i: (i, 0)),
        out_shape=jax.ShapeDtypeStruct((T, D), F32),
        scratch_shapes=[pltpu.VMEM((2, TOP_K, rt, D // 2), jnp.int32), pltpu.SemaphoreType.DMA((2,))],
        compiler_params=_cparams(("arbitrary",)),
        name="combine",
    )(dest2, dest2, x.reshape(T, D), route.reshape(T, ROUTE_LANES), mod, yb)
    return out.reshape(B, S, D)


def _prep_params(p):
    f = {}
    w_in = p["w_in"]
    L = w_in.shape[0]
    z = lambda n: jnp.zeros((L, D_MODEL, n), F32)
    f["w_in"] = jnp.concatenate(
        [w_in[:, :, 0:768], w_in[:, :, 800:1056], w_in[:, :, 1056:1312], w_in[:, :, 1312:1440],
         z(MLA_NOPE), w_in[:, :, 1440:1472], z(HEAD_PAD - MLA_QK),
         w_in[:, :, 768:800], z(128 - 2 * GLA_GATE_RANK)], axis=-1).astype(BF16)
    up = jnp.zeros((L, 128, 256), F32)
    up = up.at[:, 0:16, 0:128].set(p["gla_gate_up_f"]).at[:, 16:32, 128:256].set(p["gla_gate_up_b"])
    f["gate_up"] = up.astype(BF16)
    f["gate_bias"] = jnp.concatenate([p["gla_gate_bias_f"], p["gla_gate_bias_b"]], axis=-1)[:, None, :]
    f["lora_g"] = jnp.concatenate([p["mla_q_lora_norm_g"], p["mla_kv_lora_norm_g"]], axis=-1)[:, None, :]
    f["gla_norm_g"] = jnp.tile(p["gla_out_norm_g"], (1, GLA_HEADS))[:, None, :]
    f["norm1_g"] = p["norm1_g"][:, None, :]
    f["norm2_g"] = p["norm2_g"][:, None, :]
    wq = p["mla_w_uq"].reshape(L, MLA_Q_LORA, MLA_HEADS, MLA_QK).transpose(0, 2, 1, 3)
    f["wq"] = jnp.pad(wq, ((0, 0), (0, 0), (0, 0), (0, HEAD_PAD - MLA_QK))).astype(BF16)
    wkv = p["mla_w_ukv"].reshape(L, MLA_KV_LORA, MLA_HEADS, MLA_NOPE + MLA_V).transpose(0, 2, 1, 3)
    f["wk"] = jnp.pad(wkv[..., :MLA_NOPE], ((0, 0), (0, 0), (0, 0), (0, HEAD_PAD - MLA_NOPE))).astype(BF16)
    wv = wkv[..., MLA_NOPE:]
    zv = jnp.zeros_like(wv)
    even = (jnp.arange(MLA_HEADS) % 2 == 0)[None, :, None, None]
    f["wv"] = jnp.where(even, jnp.concatenate([wv, zv], -1), jnp.concatenate([zv, wv], -1)).astype(BF16)
    pad_qk = ((0, 0), (0, HEAD_PAD - MLA_QK))
    f["qg"] = (jnp.pad(p["mla_q_norm_g"], pad_qk) * (MLA_QK ** -0.5 * LOG2E))[:, None, :]
    f["kg"] = jnp.pad(p["mla_k_norm_g"], pad_qk)[:, None, :]
    bound = (jnp.max(jnp.abs(p["mla_q_norm_g"]), axis=-1) * jnp.max(jnp.abs(p["mla_k_norm_g"]), axis=-1)
             * (math.sqrt(MLA_QK) * 1.01 * LOG2E) + 0.1).astype(BF16).astype(F32)
    use_bound = bound <= MAX_STATIC_SHIFT * LOG2E
    f["mla_flag"] = use_bound.astype(jnp.int32)[:, None]
    f["mla_shift"] = jnp.zeros((L, 1, HEAD_PAD), F32).at[:, 0, SHIFT_LANE].set(jnp.where(use_bound, -bound, 0.0))
    f["w_out"] = p["w_out"].astype(BF16)
    rw = jnp.concatenate([p["router_group_w"], p["router_expert_w"]], axis=-1)
    rw = jnp.pad(rw, ((0, 0), (0, 0), (0, ROUTE_LANES - N_GROUPS - N_EXPERTS)))
    rw_hi = rw.astype(BF16)
    rw_lo = (rw - rw_hi.astype(F32)).astype(BF16)
    f["rw"] = jnp.stack([rw_hi, rw_lo], axis=1)
    rb = jnp.concatenate([p["router_group_b"], p["router_expert_b"]], axis=-1)
    f["rb"] = jnp.pad(rb, ((0, 0), (0, ROUTE_LANES - N_GROUPS - N_EXPERTS)))[:, None, :]
    f["w1"] = p["expert_w1"].astype(BF16)
    f["w3"] = p["expert_w3"].astype(BF16)
    f["w2"] = p["expert_w2"].astype(BF16)
    return f


def _const_tables():
    c = np.arange(FNET_W)
    same = (c[:, None] // FNET_GROUP_DIM) == (c[None, :] // FNET_GROUP_DIM)
    ang = 2.0 * np.pi * ((c[:, None] % FNET_GROUP_DIM) * (c[None, :] % FNET_GROUP_DIM) % FNET_GROUP_DIM) / FNET_GROUP_DIM
    dft64 = np.stack([np.where(same, np.cos(ang), 0.0), np.where(same, np.sin(ang), 0.0)]).astype(np.float32)
    vone = np.zeros((MLA_HEADS, 1, HEAD_PAD), np.float32)
    vone[0::2, 0, MLA_V] = 1.0
    vone[1::2, 0, 0] = 1.0
    return jnp.asarray(dft64, BF16), jnp.asarray(vone)


def _seq_tables(S):
    j = lax.broadcasted_iota(jnp.int32, (S, S), 0)
    k = lax.broadcasted_iota(jnp.int32, (S, S), 1)
    ang = ((j * k) % S).astype(F32) * (2.0 * math.pi / S)
    a_dft = jnp.concatenate([jnp.cos(ang), -jnp.sin(ang)], axis=1).astype(BF16)
    half = MLA_ROPE // 2
    freqs = ROPE_THETA ** (-jnp.arange(half, dtype=F32) / half)
    ra = jnp.arange(S, dtype=F32)[:, None] * freqs[None, :]
    cos, sin = jnp.cos(ra), jnp.sin(ra)
    one = lambda n: jnp.ones((S, n), F32)
    zero = lambda n: jnp.zeros((S, n), F32)
    tail = HEAD_PAD - MLA_QK
    rope = jnp.stack([
        jnp.concatenate([one(MLA_NOPE), cos, cos, one(tail)], axis=1),
        jnp.concatenate([zero(MLA_NOPE), zero(half), sin, zero(tail)], axis=1),
        jnp.concatenate([zero(MLA_NOPE), -sin, zero(half), zero(tail)], axis=1),
    ])
    return a_dft, rope


def _dispatch_plan(route2d, counts):
    T = route2d.shape[0]
    R = T * TOP_K
    cnt = counts[0, :N_EXPERTS].astype(jnp.int32)
    padded = (cnt + MOE_ROWS - 1) // MOE_ROWS * MOE_ROWS
    pend = jnp.cumsum(padded)
    pstart = pend - padded
    start = jnp.zeros((1, ROUTE_LANES), F32).at[0, :N_EXPERTS].set(pstart.astype(F32))
    dest = route_rank(route2d, start)[:, 0:TOP_K]
    P = R + N_EXPERTS * MOE_ROWS
    n_blk = P // MOE_ROWS
    blk_first = jnp.arange(n_blk, dtype=jnp.int32) * MOE_ROWS
    blk_e = jnp.minimum(jnp.sum((pend[None, :] <= blk_first[:, None]).astype(jnp.int32), axis=1), N_EXPERTS - 1)
    n_used = (pend[-1] // MOE_ROWS).astype(jnp.int32).reshape(1)
    return dest, blk_e, n_used, P


def _trunk(x, mod_all, f, consts, tables):
    B, S, D = x.shape
    T = B * S
    dft64, vone = consts
    a_dft, rope = tables
    tm = min(S, 512)
    tq = min(S, 512)
    tk = min(S, 512)
    dft_scale = 1.0 / math.sqrt(FNET_GROUP_DIM * S)
    for l in range(DEPTH):
        mod = mod_all[l].reshape(B, 6, D)
        qkv, og, gate, xcs, cq, ckv, kr = in_proj(
            x, mod, f["norm1_g"][l], f["w_in"][l], dft64, f["gate_up"][l], f["gate_bias"][l], f["lora_g"][l], tm)
        o_gla = gla(qkv, og, gate, f["gla_norm_g"][l])
        o_fft = dft_seq(a_dft, xcs.reshape(2 * S, B * FNET_W), dft_scale)
        o_mla = mla(f["mla_flag"][l], cq, ckv, kr, rope, f["wq"][l], f["wk"][l], f["wv"][l], f["qg"][l], f["kg"][l],
                    vone, f["mla_shift"][l], tq, tk)
        x, h, route, counts = out_proj(x, o_gla, o_fft, o_mla, mod, f["w_out"][l], f["norm2_g"][l], f["rw"][l],
                                       f["rb"][l], tm)
        dest, blk_e, n_used, n_rows = _dispatch_plan(route.reshape(T, ROUTE_LANES), counts)
        xb = dispatch_rows(h.reshape(T, D // 2), dest, n_rows)
        yb = moe_mlp(xb, blk_e, n_used, f["w1"][l], f["w3"][l], f["w2"][l])
        x = combine(x, yb, dest, route, mod)
    return x


def kernel(x_prompt, x_sample, c_prompt, c_sample, ada_w, ada_b, norm1_g, norm2_g, w_in, w_out, gla_gate_up_f, gla_gate_bias_f, gla_gate_up_b, gla_gate_bias_b, gla_out_norm_g, mla_q_lora_norm_g, mla_w_uq, mla_kv_lora_norm_g, mla_w_ukv, mla_q_norm_g, mla_k_norm_g, router_group_w, router_group_b, router_expert_w, router_expert_b, expert_w1, expert_w3, expert_w2):
    p = dict(norm1_g=norm1_g, norm2_g=norm2_g, w_in=w_in, w_out=w_out, gla_gate_up_f=gla_gate_up_f,
             gla_gate_bias_f=gla_gate_bias_f, gla_gate_up_b=gla_gate_up_b, gla_gate_bias_b=gla_gate_bias_b,
             gla_out_norm_g=gla_out_norm_g, mla_q_lora_norm_g=mla_q_lora_norm_g, mla_w_uq=mla_w_uq,
             mla_kv_lora_norm_g=mla_kv_lora_norm_g, mla_w_ukv=mla_w_ukv, mla_q_norm_g=mla_q_norm_g,
             mla_k_norm_g=mla_k_norm_g, router_group_w=router_group_w, router_group_b=router_group_b,
             router_expert_w=router_expert_w, router_expert_b=router_expert_b, expert_w1=expert_w1,
             expert_w3=expert_w3, expert_w2=expert_w2)
    f = _prep_params(p)
    consts = _const_tables()
    nb_p = c_prompt.shape[0]
    mod_all = ada_modulation(jnp.concatenate([c_prompt, c_sample], axis=0), ada_w, ada_b)
    y_prompt = _trunk(x_prompt, mod_all[:, :nb_p], f, consts, _seq_tables(x_prompt.shape[1]))
    y_sample = _trunk(x_sample, mod_all[:, nb_p:], f, consts, _seq_tables(x_sample.shape[1]))
    return (y_prompt, y_sample)
```

```python
import functools
import math

import jax
import jax.numpy as jnp
import numpy as np
from jax import lax
from jax.experimental import pallas as pl
from jax.experimental.pallas import tpu as pltpu

F32 = jnp.float32
BF16 = jnp.bfloat16

D_MODEL = 1024
DEPTH = 4
EPS = 1e-6

GLA_HEADS = 4
GLA_DK = 32
GLA_DV = 64
GLA_QK_W = GLA_HEADS * GLA_DK
GLA_W = GLA_HEADS * GLA_DV
GLA_GATE_RANK = 16
GLA_GATE_NORMALIZER = 16.0
GLA_CHUNK = 32
GLA_BLOCK = 128

FNET_GROUPS = 4
FNET_GROUP_DIM = 64
FNET_W = 256

MLA_HEADS = 8
MLA_Q_LORA = 256
MLA_KV_LORA = 128
MLA_NOPE = 64
MLA_ROPE = 32
MLA_V = 64
MLA_QK = 96
MLA_W = 512
ROPE_THETA = 10000.0
HEAD_PAD = 128

N_GROUPS = 4
EXPERTS_PER_GROUP = 8
N_EXPERTS = 32
TOP_K = 2
D_EXPERT = 512
ROUTE_LANES = 128

C_Q, C_K, C_V, C_OG, C_F, C_CQ, C_CKV, C_KR, C_GATE = 0, 128, 256, 512, 768, 1024, 1280, 1408, 1536
P_IN_PAD = 1664

VMEM_LIMIT_BYTES = 56 * 1024 * 1024


def _cparams(sem, vmem=None):
    return pltpu.CompilerParams(dimension_semantics=sem, vmem_limit_bytes=vmem or VMEM_LIMIT_BYTES)


def _silu(x):
    return x * (1.0 / (1.0 + jnp.exp(-x)))


def _log_sigmoid(x):
    return -(jnp.maximum(-x, 0.0) + jnp.log1p(jnp.exp(-jnp.abs(x))))


def _dot(a, b):
    return jnp.dot(a, b, preferred_element_type=F32)


def _dot_nt(a, b):
    return lax.dot_general(a, b, (((1,), (1,)), ((), ())), preferred_element_type=F32)


def _split_bf16(x):
    hi = x.astype(BF16)
    lo = (x - hi.astype(F32)).astype(BF16)
    return hi, lo


def _pack_halves(x):
    n = x.shape[-1] // 2
    lo = lax.shift_right_logical(lax.bitcast_convert_type(x[:, :n], jnp.int32), 16)
    hi = lax.bitcast_convert_type(x[:, n:], jnp.int32) & jnp.int32(-65536)
    return hi | lo


def _unpack_halves(w):
    lo = lax.bitcast_convert_type(lax.shift_left(w, 16), F32)
    hi = lax.bitcast_convert_type(w & jnp.int32(-65536), F32)
    return lo, hi


def _ada_kernel(c_ref, w_ref, b_ref, o_ref):
    c = _silu(c_ref[...]).astype(BF16)
    o_ref[0] = _dot(c, w_ref[0].astype(BF16)) + b_ref[0]


def ada_modulation(c_all, ada_w, ada_b):
    nb = c_all.shape[0]
    tn = 1536
    n = ada_w.shape[-1]
    return pl.pallas_call(
        _ada_kernel,
        grid=(DEPTH, n // tn),
        in_specs=[
            pl.BlockSpec((nb, D_MODEL), lambda l, j: (0, 0)),
            pl.BlockSpec((1, D_MODEL, tn), lambda l, j: (l, 0, j)),
            pl.BlockSpec((1, 1, tn), lambda l, j: (l, 0, j)),
        ],
        out_specs=pl.BlockSpec((1, nb, tn), lambda l, j: (l, 0, j)),
        out_shape=jax.ShapeDtypeStruct((DEPTH, nb, n), F32),
        compiler_params=_cparams(("parallel", "parallel")),
        name="ada_modulation",
    )(c_all, ada_w, ada_b.reshape(DEPTH, 1, n))


def _in_proj_kernel(x_ref, mod_ref, g_ref, w_ref, dft_ref, up_ref, gb_ref, lg_ref,
                    qkv_ref, og_ref, gate_ref, xcs_ref, cq_ref, ckv_ref, kr_ref):
    x = x_ref[0]
    sh = mod_ref[0, 0:1, :]
    sc = mod_ref[0, 1:2, :]
    r = lax.rsqrt(jnp.mean(x * x, axis=-1, keepdims=True) + EPS)
    h = (x * r * g_ref[...]) * (1.0 + sc) + sh
    u = _dot(h.astype(BF16), w_ref[...])

    q = u[:, C_Q:C_K] * (GLA_DK ** -0.5)
    qkv_ref[0, :, 0:128] = q.astype(BF16)
    qkv_ref[0, :, 128:512] = u[:, C_K:C_OG].astype(BF16)
    og_ref[0] = u[:, C_OG:C_F].astype(BF16)

    ug = u[:, C_GATE:C_GATE + 128].astype(BF16)
    gl = _dot(ug, up_ref[...]) + gb_ref[...]
    gate_ref[0] = _log_sigmoid(gl) * (1.0 / GLA_GATE_NORMALIZER)

    uf = u[:, C_F:C_CQ].astype(BF16)
    xcs_ref[0] = _dot(uf, dft_ref[0]).astype(BF16)
    xcs_ref[1] = _dot(uf, dft_ref[1]).astype(BF16)

    cq = u[:, C_CQ:C_CKV]
    rq = lax.rsqrt(jnp.mean(cq * cq, axis=-1, keepdims=True) + EPS)
    cq_ref[0] = (cq * rq * lg_ref[:, 0:256]).astype(BF16)
    ckv = u[:, C_CKV:C_KR]
    rkv = lax.rsqrt(jnp.mean(ckv * ckv, axis=-1, keepdims=True) + EPS)
    ckv_ref[0] = (ckv * rkv * lg_ref[:, 256:384]).astype(BF16)
    kr_ref[0] = u[:, C_KR:C_GATE].astype(BF16)


def in_proj(x, mod, norm_g, w_in, dft64, gate_up, gate_bias, lora_g, tm):
    B, S, D = x.shape
    ns = S // tm
    row = lambda b, s: (b, s, 0)
    const2 = lambda b, s: (0, 0)
    out_shapes = (
        jax.ShapeDtypeStruct((B, S, 512), BF16),
        jax.ShapeDtypeStruct((B, S, GLA_W), BF16),
        jax.ShapeDtypeStruct((B, S, 256), F32),
        jax.ShapeDtypeStruct((2, S, B * FNET_W), BF16),
        jax.ShapeDtypeStruct((B, S, MLA_Q_LORA), BF16),
        jax.ShapeDtypeStruct((B, S, MLA_KV_LORA), BF16),
        jax.ShapeDtypeStruct((B, S, HEAD_PAD), BF16),
    )
    return pl.pallas_call(
        _in_proj_kernel,
        grid=(B, ns),
        in_specs=[
            pl.BlockSpec((1, tm, D), row),
            pl.BlockSpec((1, 6, D), lambda b, s: (b, 0, 0)),
            pl.BlockSpec((1, D), const2),
            pl.BlockSpec((D, P_IN_PAD), const2),
            pl.BlockSpec((2, FNET_W, FNET_W), lambda b, s: (0, 0, 0)),
            pl.BlockSpec((128, 256), const2),
            pl.BlockSpec((1, 256), const2),
            pl.BlockSpec((1, 384), const2),
        ],
        out_specs=(
            pl.BlockSpec((1, tm, 512), row),
            pl.BlockSpec((1, tm, GLA_W), row),
            pl.BlockSpec((1, tm, 256), row),
            pl.BlockSpec((2, tm, FNET_W), lambda b, s: (0, s, b)),
            pl.BlockSpec((1, tm, MLA_Q_LORA), row),
            pl.BlockSpec((1, tm, MLA_KV_LORA), row),
            pl.BlockSpec((1, tm, HEAD_PAD), row),
        ),
        out_shape=out_shapes,
        compiler_params=_cparams(("parallel", "parallel")),
        name="in_proj",
    )(x, mod, norm_g, w_in, dft64, gate_up, gate_bias, lora_g)


def _gla_kernel(qkv_ref, og_ref, gate_ref, ng_ref, o_ref, o_s, st_s, *, seq):
    R, C = GLA_BLOCK, GLA_CHUNK
    n_sub = R // C
    nblk = seq // R
    shift_c = int(math.log2(C))

    row = lax.broadcasted_iota(jnp.int32, (R, R), 0)
    col = lax.broadcasted_iota(jnp.int32, (R, R), 1)
    same = (row >> shift_c) == (col >> shift_c)
    ones_blk = jnp.where(same, 1.0, 0.0)
    lane_qk = lax.broadcasted_iota(jnp.int32, (1, GLA_QK_W), 1)
    lane_v = lax.broadcasted_iota(jnp.int32, (1, GLA_W), 1)
    head_qk = [(lane_qk >> 5) == h for h in range(GLA_HEADS)]
    head_v = [(lane_v >> 6) == h for h in range(GLA_HEADS)]
    st_row = lax.broadcasted_iota(jnp.int32, (GLA_W, GLA_QK_W), 0)
    st_col = lax.broadcasted_iota(jnp.int32, (GLA_W, GLA_QK_W), 1)
    st_mask = (st_row >> 6) == (st_col >> 5)
    sub_row = lax.broadcasted_iota(jnp.int32, (R, 1), 0) >> shift_c
    nrow = lax.broadcasted_iota(jnp.int32, (GLA_W, GLA_W), 0)
    ncol = lax.broadcasted_iota(jnp.int32, (GLA_W, GLA_W), 1)
    head_mean = jnp.where((nrow >> 6) == (ncol >> 6), 1.0 / GLA_DV, 0.0).astype(BF16)

    def direction(backward):
        if backward:
            tri = same & (col >= row)
            att_ok = same & (col > row)
            order = list(range(n_sub - 1, -1, -1))
        else:
            tri = same & (col <= row)
            att_ok = same & (col <= row)
            order = list(range(n_sub))
        cum_lhs = jnp.concatenate([jnp.where(tri, 1.0, 0.0), ones_blk], axis=0).astype(BF16)
        att_ok4 = jnp.concatenate([att_ok] * GLA_HEADS, axis=1)
        return cum_lhs, att_ok4, order, (GLA_QK_W if backward else 0), (1 if backward else 0)

    def block(j, consts):
        cum_lhs, att_ok4, order, gate_off, slot = consts
        r0 = pl.multiple_of(j * R, R)
        qkv = qkv_ref[0, pl.ds(r0, R), :]
        q = qkv[:, 0:128].astype(F32)
        k = qkv[:, 128:256].astype(F32)
        v = qkv[:, 256:512]
        g = gate_ref[0, pl.ds(r0, R), gate_off:gate_off + GLA_QK_W]
        g_hi, g_lo = _split_bf16(g)
        cs = _dot(cum_lhs, jnp.concatenate([g_hi, g_lo], axis=1))
        b = cs[0:R, 0:128] + cs[0:R, 128:256]
        bl = cs[R:2 * R, 0:128] + cs[R:2 * R, 128:256]
        q_dec = q * jnp.exp(b)
        k_inv = (k * jnp.exp(-b)).astype(BF16)
        k_end = k * jnp.exp(bl - b)
        decay = jnp.exp(bl)

        att = jnp.concatenate(
            [_dot_nt(jnp.where(head_qk[h], q_dec, 0.0).astype(BF16), k_inv) for h in range(GLA_HEADS)],
            axis=1)
        att = jnp.where(att_ok4, att, 0.0).astype(BF16)
        v_heads = jnp.concatenate([jnp.where(head_v[h], v, jnp.zeros_like(v)) for h in range(GLA_HEADS)], axis=0)
        o_blk = _dot(att, v_heads)

        v_t = v.astype(F32).T.astype(BF16)
        q_dec_b = q_dec.astype(BF16)
        inter = [None] * n_sub
        for c in order:
            st = st_s[slot]
            inter[c] = _dot_nt(q_dec_b[c * C:(c + 1) * C, :], st.astype(BF16))
            k_c = jnp.where(sub_row == c, k_end, 0.0).astype(BF16)
            d_st = _dot(v_t, k_c)
            st_s[slot] = st * decay[c * C:c * C + 1, :] + jnp.where(st_mask, d_st, 0.0)
        o_s[slot, pl.ds(r0, R), :] = o_blk + jnp.concatenate(inter, axis=0)

    fwd = direction(False)
    bwd = direction(True)
    st_s[...] = jnp.zeros_like(st_s)

    def scan_step(i, carry):
        block(i, fwd)
        block(nblk - 1 - i, bwd)
        return carry

    lax.fori_loop(0, nblk, scan_step, 0)

    def finish(j, carry):
        r0 = pl.multiple_of(j * R, R)
        o = o_s[0, pl.ds(r0, R), :] + o_s[1, pl.ds(r0, R), :]
        sq_hi, sq_lo = _split_bf16(o * o)
        ms = _dot(sq_hi, head_mean) + _dot(sq_lo, head_mean)
        y = o * lax.rsqrt(ms + EPS) * ng_ref[...]
        og = og_ref[0, pl.ds(r0, R), :].astype(F32)
        o_ref[0, pl.ds(r0, R), :] = (y * _silu(og)).astype(o_ref.dtype)
        return carry

    lax.fori_loop(0, nblk, finish, 0)


def gla(qkv, og, gate, norm_g4):
    B, S, _ = qkv.shape
    full = lambda b: (b, 0, 0)
    return pl.pallas_call(
        functools.partial(_gla_kernel, seq=S),
        grid=(B,),
        in_specs=[
            pl.BlockSpec((1, S, 512), full),
            pl.BlockSpec((1, S, GLA_W), full),
            pl.BlockSpec((1, S, 256), full),
            pl.BlockSpec((1, GLA_W), lambda b: (0, 0)),
        ],
        out_specs=pl.BlockSpec((1, S, GLA_W), full),
        out_shape=jax.ShapeDtypeStruct((B, S, GLA_W), BF16),
        scratch_shapes=[pltpu.VMEM((2, S, GLA_W), F32), pltpu.VMEM((2, GLA_W, GLA_QK_W), F32)],
        compiler_params=_cparams(("parallel",)),
        name="gla",
    )(qkv, og, gate, norm_g4)


def _dft_kernel(a_ref, b_ref, o_ref, acc_ref, *, scale):
    kk = pl.program_id(2)

    @pl.when(kk == 0)
    def _():
        acc_ref[...] = jnp.zeros_like(acc_ref)

    acc_ref[...] += _dot(a_ref[...], b_ref[...])

    @pl.when(kk == pl.num_programs(2) - 1)
    def _():
        o_ref[...] = (acc_ref[...] * scale).astype(o_ref.dtype)


def dft_seq(a_dft, xcs, scale):
    m, k = a_dft.shape
    n = xcs.shape[1]
    tm, tn, tk = min(m, 1024), min(n, 1024), min(k, 2048)
    return pl.pallas_call(
        functools.partial(_dft_kernel, scale=scale),
        grid=(m // tm, n // tn, k // tk),
        in_specs=[pl.BlockSpec((tm, tk), lambda i, j, kk: (i, kk)),
                  pl.BlockSpec((tk, tn), lambda i, j, kk: (kk, j))],
        out_specs=pl.BlockSpec((tm, tn), lambda i, j, kk: (i, j)),
        out_shape=jax.ShapeDtypeStruct((m, n), BF16),
        scratch_shapes=[pltpu.VMEM((tm, tn), F32)],
        compiler_params=_cparams(("parallel", "parallel", "arbitrary")),
        name="dft_seq",
    )(a_dft, xcs)


def _rope(x, cos, sin, swap):
    return x * cos + _dot(x.astype(BF16), swap) * sin


SHIFT_LANE = HEAD_PAD - 1
MAX_STATIC_SHIFT = 40.0
LOG2E = math.log2(math.e)


def _mla_kernel(flag_ref, cq_ref, ckv_ref, kr_ref, rope_ref, wq_ref, wk_ref, wv_ref, qg_ref, kg_ref, vone_ref,
                shift_ref, o_ref, kt_s, v_s, q_s, acc_s, *, seq, tq, tk):
    qi = pl.program_id(1)
    inv_qk = 1.0 / MLA_QK
    lane = lax.broadcasted_iota(jnp.int32, (1, HEAD_PAD), 1)
    src = lax.broadcasted_iota(jnp.int32, (HEAD_PAD, HEAD_PAD), 0)
    dst = lax.broadcasted_iota(jnp.int32, (HEAD_PAD, HEAD_PAD), 1)
    half = MLA_ROPE // 2
    lo_half = (dst >= MLA_NOPE) & (dst < MLA_NOPE + half)
    hi_half = (dst >= MLA_NOPE + half) & (dst < MLA_QK)
    swap = jnp.where((lo_half & (src == dst + half)) | (hi_half & (src == dst - half)), 1.0, 0.0).astype(BF16)

    @pl.when(qi == 0)
    def _():
        one_lane = jnp.where(lane == SHIFT_LANE, 1.0, 0.0)

        def rows(i, carry):
            r0 = pl.multiple_of(i * tk, tk)
            ckv = ckv_ref[0, pl.ds(r0, tk), :]
            kr = kr_ref[0, pl.ds(r0, tk), :].astype(F32)
            cos = rope_ref[0, pl.ds(r0, tk), :]
            sin = rope_ref[1, pl.ds(r0, tk), :]
            for h in range(MLA_HEADS):
                kp = _dot(ckv, wk_ref[h]) + kr
                r = lax.rsqrt(jnp.sum(kp * kp, axis=-1, keepdims=True) * inv_qk + EPS)
                kn = _rope(kp * r * kg_ref[...], cos, sin, swap) + one_lane
                kt_s[h, i] = kn.T.astype(BF16)
                v_s[h, pl.ds(r0, tk), :] = (_dot(ckv, wv_ref[h]) + vone_ref[h]).astype(BF16)
            return carry

        lax.fori_loop(0, seq // tk, rows, 0)

    q0 = pl.multiple_of(qi * tq, tq)
    cos = rope_ref[0, pl.ds(q0, tq), :]
    sin = rope_ref[1, pl.ds(q0, tq), :]
    for h in range(MLA_HEADS):
        qp = _dot(cq_ref[0], wq_ref[h])
        r = lax.rsqrt(jnp.sum(qp * qp, axis=-1, keepdims=True) * inv_qk + EPS)
        q_s[h] = (_rope(qp * r * qg_ref[...], cos, sin, swap) + shift_ref[...]).astype(BF16)

    def finish():
        for pair in range(MLA_HEADS // 2):
            outs = []
            for h in (2 * pair, 2 * pair + 1):
                acc = acc_s[h]
                den_lane = MLA_V if h % 2 == 0 else 0
                den = jnp.sum(jnp.where(lane == den_lane, acc, 0.0), axis=-1, keepdims=True)
                outs.append(acc * (1.0 / den))
            both = jnp.where(lane < MLA_V, outs[0], outs[1])
            o_ref[0, :, pair * HEAD_PAD:(pair + 1) * HEAD_PAD] = both.astype(o_ref.dtype)

    @pl.when(flag_ref[0] == 1)
    def _():
        acc_s[...] = jnp.zeros_like(acc_s)

        def kv_step(j, carry):
            k0 = pl.multiple_of(j * tk, tk)
            for h in range(MLA_HEADS):
                p = jnp.exp2(_dot(q_s[h], kt_s[h, j])).astype(BF16)
                acc_s[h] += _dot(p, v_s[h, pl.ds(k0, tk), :])
            return carry

        lax.fori_loop(0, seq // tk, kv_step, 0)
        finish()

    @pl.when(flag_ref[0] == 0)
    def _():
        for h in range(MLA_HEADS):
            def kv_step(j, carry, h=h):
                m, acc = carry
                k0 = pl.multiple_of(j * tk, tk)
                s = _dot(q_s[h], kt_s[h, j])
                m_new = jnp.maximum(m, jnp.max(s, axis=-1, keepdims=True))
                p = jnp.exp2(s - m_new).astype(BF16)
                return m_new, jnp.exp2(m - m_new) * acc + _dot(p, v_s[h, pl.ds(k0, tk), :])

            m0 = jnp.full((tq, 1), -jnp.inf, F32)
            _, acc = lax.fori_loop(0, seq // tk, kv_step, (m0, jnp.zeros((tq, HEAD_PAD), F32)))
            acc_s[h] = acc
        finish()


def mla(flag, cq, ckv, kr, rope, wq, wk, wv, qg, kg, vone, shift, tq, tk):
    B, S, _ = cq.shape
    c3 = lambda b, q: (0, 0, 0)
    c2 = lambda b, q: (0, 0)
    return pl.pallas_call(
        functools.partial(_mla_kernel, seq=S, tq=tq, tk=tk),
        grid=(B, S // tq),
        in_specs=[
            pl.BlockSpec(memory_space=pltpu.SMEM),
            pl.BlockSpec((1, tq, MLA_Q_LORA), lambda b, q: (b, q, 0)),
            pl.BlockSpec((1, S, MLA_KV_LORA), lambda b, q: (b, 0, 0)),
            pl.BlockSpec((1, S, HEAD_PAD), lambda b, q: (b, 0, 0)),
            pl.BlockSpec((2, S, HEAD_PAD), c3),
            pl.BlockSpec((MLA_HEADS, MLA_Q_LORA, HEAD_PAD), c3),
            pl.BlockSpec((MLA_HEADS, MLA_KV_LORA, HEAD_PAD), c3),
            pl.BlockSpec((MLA_HEADS, MLA_KV_LORA, HEAD_PAD), c3),
            pl.BlockSpec((1, HEAD_PAD), c2),
            pl.BlockSpec((1, HEAD_PAD), c2),
            pl.BlockSpec((MLA_HEADS, 1, HEAD_PAD), c3),
            pl.BlockSpec((1, HEAD_PAD), c2),
        ],
        out_specs=pl.BlockSpec((1, tq, MLA_W), lambda b, q: (b, q, 0)),
        out_shape=jax.ShapeDtypeStruct((B, S, MLA_W), BF16),
        scratch_shapes=[pltpu.VMEM((MLA_HEADS, S // tk, HEAD_PAD, tk), BF16),
                        pltpu.VMEM((MLA_HEADS, S, HEAD_PAD), BF16),
                        pltpu.VMEM((MLA_HEADS, tq, HEAD_PAD), BF16),
                        pltpu.VMEM((MLA_HEADS, tq, HEAD_PAD), F32)],
        compiler_params=_cparams(("parallel", "arbitrary")),
        name="mla",
    )(flag, cq, ckv, kr, rope, wq, wk, wv, qg, kg, vone, shift)


def _out_proj_kernel(x_ref, gla_ref, fft_ref, mla_ref, mod_ref, w_ref, g_ref, rw_ref, rb_ref,
                     xo_ref, h_ref, route_ref, cnt_ref):
    mix = (_dot(gla_ref[0], w_ref[0:256, :]) + _dot(fft_ref[...], w_ref[256:512, :])
           + _dot(mla_ref[0], w_ref[512:1024, :]))
    g1 = mod_ref[0, 2:3, :]
    sh = mod_ref[0, 3:4, :]
    sc = mod_ref[0, 4:5, :]
    x = x_ref[0] + g1 * mix
    xo_ref[0] = x
    r = lax.rsqrt(jnp.mean(x * x, axis=-1, keepdims=True) + EPS)
    h = (x * r * g_ref[...]) * (1.0 + sc) + sh
    h_hi, h_lo = _split_bf16(h)
    h_ref[0] = _pack_halves(h_hi.astype(F32))

    logit = _dot(h_hi, rw_ref[0]) + _dot(h_lo, rw_ref[0]) + _dot(h_hi, rw_ref[1]) + rb_ref[...]
    lane = lax.broadcasted_iota(jnp.int32, (1, ROUTE_LANES), 1)
    lane_f = lane.astype(F32)
    neg = -1e30
    is_g = lane < N_GROUPS
    is_e = (lane >= N_GROUPS) & (lane < N_GROUPS + N_EXPERTS)
    lg = jnp.where(is_g, logit, neg)
    g_max = jnp.max(lg, axis=-1, keepdims=True)
    g_den = jnp.sum(jnp.where(is_g, jnp.exp(lg - g_max), 0.0), axis=-1, keepdims=True)
    g_w = 1.0 / g_den
    g_top = jnp.min(jnp.where(is_g & (logit == g_max), lane_f, 1e9), axis=-1, keepdims=True)
    e_grp = ((lane - N_GROUPS) >> 3).astype(F32)
    in_grp = is_e & (e_grp == g_top)
    le = jnp.where(in_grp, logit, neg)
    t1 = jnp.max(le, axis=-1, keepdims=True)
    i1 = jnp.min(jnp.where(in_grp & (le == t1), lane_f, 1e9), axis=-1, keepdims=True)
    le2 = jnp.where(lane_f == i1, neg, le)
    t2 = jnp.max(le2, axis=-1, keepdims=True)
    i2 = jnp.min(jnp.where(in_grp & (le2 == t2), lane_f, 1e9), axis=-1, keepdims=True)
    e21 = jnp.exp(t2 - t1)
    w1 = g_w / (1.0 + e21)
    w2 = w1 * e21
    route = jnp.where(lane == 0, i1 - N_GROUPS,
                      jnp.where(lane == 1, i2 - N_GROUPS,
                                jnp.where(lane == 2, w1, jnp.where(lane == 3, w2, 0.0))))
    route_ref[0] = route

    @pl.when((pl.program_id(0) == 0) & (pl.program_id(1) == 0))
    def _():
        cnt_ref[...] = jnp.zeros_like(cnt_ref)

    picked = jnp.where((lane_f == i1 - N_GROUPS) | (lane_f == i2 - N_GROUPS), 1.0, 0.0)
    cnt_ref[...] += jnp.sum(picked, axis=0, keepdims=True)


def out_proj(x, o_gla, o_fft, o_mla, mod, w_out, norm_g, rw, rb, tm):
    B, S, D = x.shape
    row = lambda b, s: (b, s, 0)
    c2 = lambda b, s: (0, 0)
    return pl.pallas_call(
        _out_proj_kernel,
        grid=(B, S // tm),
        in_specs=[
            pl.BlockSpec((1, tm, D), row),
            pl.BlockSpec((1, tm, GLA_W), row),
            pl.BlockSpec((tm, FNET_W), lambda b, s: (s, b)),
            pl.BlockSpec((1, tm, MLA_W), row),
            pl.BlockSpec((1, 6, D), lambda b, s: (b, 0, 0)),
            pl.BlockSpec((D, D), c2),
            pl.BlockSpec((1, D), c2),
            pl.BlockSpec((2, D, ROUTE_LANES), lambda b, s: (0, 0, 0)),
            pl.BlockSpec((1, ROUTE_LANES), c2),
        ],
        out_specs=(pl.BlockSpec((1, tm, D), row), pl.BlockSpec((1, tm, D // 2), row),
                   pl.BlockSpec((1, tm, ROUTE_LANES), row), pl.BlockSpec((1, ROUTE_LANES), c2)),
        out_shape=(jax.ShapeDtypeStruct((B, S, D), F32), jax.ShapeDtypeStruct((B, S, D // 2), jnp.int32),
                   jax.ShapeDtypeStruct((B, S, ROUTE_LANES), F32), jax.ShapeDtypeStruct((1, ROUTE_LANES), F32)),
        compiler_params=_cparams(("arbitrary", "arbitrary")),
        name="out_proj",
    )(x, o_gla, o_fft, o_mla, mod, w_out, norm_g, rw, rb)


RANK_ROWS = 512


def _rank_kernel(route_ref, start_ref, dest_ref, carry_ref):
    @pl.when(pl.program_id(0) == 0)
    def _():
        carry_ref[...] = jnp.zeros_like(carry_ref)

    n = route_ref.shape[0]
    lane_f = lax.broadcasted_iota(jnp.int32, (1, ROUTE_LANES), 1).astype(F32)
    e1 = route_ref[:, 0:1]
    e2 = route_ref[:, 1:2]
    oh1 = jnp.where(lane_f == e1, 1.0, 0.0)
    oh2 = jnp.where(lane_f == e2, 1.0, 0.0)
    both = (oh1 + oh2).astype(BF16)
    row = lax.broadcasted_iota(jnp.int32, (n, n), 0)
    col = lax.broadcasted_iota(jnp.int32, (n, n), 1)
    before = jnp.where(col < row, 1.0, 0.0).astype(BF16)
    pos = _dot(before, both) + carry_ref[...] + start_ref[...]
    d1 = jnp.sum(oh1 * pos, axis=-1, keepdims=True)
    d2 = jnp.sum(oh2 * pos, axis=-1, keepdims=True)
    lane = lax.broadcasted_iota(jnp.int32, (1, ROUTE_LANES), 1)
    dest_ref[...] = jnp.where(lane == 0, d1, jnp.where(lane == 1, d2, 0.0)).astype(jnp.int32)
    carry_ref[...] += jnp.sum(oh1 + oh2, axis=0, keepdims=True)


def route_rank(route2d, start):
    T = route2d.shape[0]
    rr = min(T, RANK_ROWS)
    return pl.pallas_call(
        _rank_kernel,
        grid=(T // rr,),
        in_specs=[pl.BlockSpec((rr, ROUTE_LANES), lambda i: (i, 0)),
                  pl.BlockSpec((1, ROUTE_LANES), lambda i: (0, 0))],
        out_specs=pl.BlockSpec((rr, ROUTE_LANES), lambda i: (i, 0)),
        out_shape=jax.ShapeDtypeStruct((T, ROUTE_LANES), jnp.int32),
        scratch_shapes=[pltpu.VMEM((1, ROUTE_LANES), F32)],
        compiler_params=_cparams(("arbitrary",)),
        name="route_rank",
    )(route2d, start)


MOE_ROWS = 512


def _moe_kernel(blk_e_ref, n_used_ref, x_ref, w1_ref, w3_ref, w2_ref, o_ref):
    i = pl.program_id(0)

    @pl.when(i < n_used_ref[0])
    def _():
        half = D_MODEL // 2
        x_lo, x_hi = _unpack_halves(x_ref[...])
        x_lo = x_lo.astype(BF16)
        x_hi = x_hi.astype(BF16)
        a = _dot(x_lo, w1_ref[0, 0:half, :]) + _dot(x_hi, w1_ref[0, half:, :])
        b = _dot(x_lo, w3_ref[0, 0:half, :]) + _dot(x_hi, w3_ref[0, half:, :])
        hm = (_silu(a) * b).astype(BF16)
        y = _dot(hm, w2_ref[0])
        o_ref[...] = _pack_halves(y.astype(BF16).astype(F32))

    @pl.when(i >= n_used_ref[0])
    def _():
        o_ref[...] = jnp.zeros_like(o_ref)


def moe_mlp(xb, blk_e, n_used, w1, w3, w2):
    P, half = xb.shape
    n_blk = P // MOE_ROWS
    wmap = lambda i, be, nu: (be[i], 0, 0)
    grid_spec = pltpu.PrefetchScalarGridSpec(
        num_scalar_prefetch=2,
        grid=(n_blk,),
        in_specs=[
            pl.BlockSpec((MOE_ROWS, half), lambda i, be, nu: (i, 0)),
            pl.BlockSpec((1, D_MODEL, D_EXPERT), wmap),
            pl.BlockSpec((1, D_MODEL, D_EXPERT), wmap),
            pl.BlockSpec((1, D_EXPERT, D_MODEL), wmap),
        ],
        out_specs=pl.BlockSpec((MOE_ROWS, half), lambda i, be, nu: (i, 0)),
    )
    return pl.pallas_call(
        _moe_kernel,
        grid_spec=grid_spec,
        out_shape=jax.ShapeDtypeStruct((P, half), jnp.int32),
        compiler_params=_cparams(("arbitrary",)),
        name="moe_mlp",
    )(blk_e, n_used, xb, w1, w3, w2)


ROW_TILE = 512


def _dispatch_kernel(dest_ref, h_ref, init_ref, xb_ref, stage, sem):
    del init_ref
    n = h_ref.shape[0]
    stage[...] = h_ref[...]
    for r in range(n):
        for c in range(TOP_K):
            d = dest_ref[0, 0, TOP_K * r + c]
            pltpu.make_async_copy(stage.at[pl.ds(r, 1)], xb_ref.at[pl.ds(d, 1)], sem).start()
    for c in range(TOP_K):
        pltpu.make_async_copy(stage, xb_ref.at[pl.ds(0, n)], sem).wait()


def dispatch_rows(hp, dest, n_rows):
    T, half = hp.shape
    rt = min(T, ROW_TILE)
    dest2 = dest.reshape(T // rt, 1, TOP_K * rt)
    init = jnp.zeros((n_rows, half), jnp.int32)
    return pl.pallas_call(
        _dispatch_kernel,
        grid=(T // rt,),
        in_specs=[
            pl.BlockSpec((1, 1, TOP_K * rt), lambda i: (i, 0, 0), memory_space=pltpu.SMEM),
            pl.BlockSpec((rt, half), lambda i: (i, 0)),
            pl.BlockSpec(memory_space=pl.ANY),
        ],
        out_specs=pl.BlockSpec(memory_space=pl.ANY),
        out_shape=jax.ShapeDtypeStruct((n_rows, half), jnp.int32),
        scratch_shapes=[pltpu.VMEM((rt, half), jnp.int32), pltpu.SemaphoreType.DMA(())],
        input_output_aliases={2: 0},
        compiler_params=_cparams(("arbitrary",)),
        name="dispatch_rows",
    )(dest2, hp, init)


def _combine_kernel(dest_ref, x_ref, route_ref, mod_ref, y_ref, o_ref, buf, sem):
    i = pl.program_id(0)
    n_tiles = pl.num_programs(0) - 1
    n = x_ref.shape[0]

    for s in range(2):
        @pl.when((i < n_tiles) & (i % 2 == s))
        def _(s=s):
            for r in range(n):
                for c in range(TOP_K):
                    d = dest_ref[0, 0, TOP_K * r + c]
                    pltpu.make_async_copy(y_ref.at[pl.ds(d, 1)], buf.at[s, c, pl.ds(r, 1)], sem.at[s]).start()

    @pl.when(i > 0)
    def _():
        s = (i - 1) % 2
        for c in range(TOP_K):
            pltpu.make_async_copy(y_ref.at[pl.ds(0, n)], buf.at[s, c], sem.at[s]).wait()
        g2 = mod_ref[0, 5:6, :]
        wa = route_ref[:, 2:3]
        wb = route_ref[:, 3:4]
        half = D_MODEL // 2
        a_lo, a_hi = _unpack_halves(buf[s, 0])
        b_lo, b_hi = _unpack_halves(buf[s, 1])
        o_ref[:, 0:half] = x_ref[:, 0:half] + g2[:, 0:half] * (wa * a_lo + wb * b_lo)
        o_ref[:, half:] = x_ref[:, half:] + g2[:, half:] * (wa * a_hi + wb * b_hi)


def combine(x, yb, dest, route, mod):
    B, S, D = x.shape
    T = B * S
    rt = min(S, ROW_TILE)
    n_tiles = T // rt
    per_b = S // rt
    dest2 = dest.reshape(n_tiles, 1, TOP_K * rt)
    prev = lambda i: jnp.maximum(i - 1, 0)
    out = pl.pallas_call(
        _combine_kernel,
        grid=(n_tiles + 1,),
        in_specs=[
            pl.BlockSpec((1, 1, TOP_K * rt), lambda i: (jnp.minimum(i, n_tiles - 1), 0, 0),
                         memory_space=pltpu.SMEM),
            pl.BlockSpec((rt, D), lambda i: (prev(i), 0)),
            pl.BlockSpec((rt, ROUTE_LANES), lambda i: (prev(i), 0)),
            pl.BlockSpec((1, 6, D), lambda i: (prev(i) // per_b, 0, 0)),
            pl.BlockSpec(memory_space=pl.ANY),
        ],
        out_specs=pl.BlockSpec((rt, D), lambda i: (prev(i), 0)),
        out_shape=jax.ShapeDtypeStruct((T, D), F32),
        scratch_shapes=[pltpu.VMEM((2, TOP_K, rt, D // 2), jnp.int32), pltpu.SemaphoreType.DMA((2,))],
        compiler_params=_cparams(("arbitrary",)),
        name="combine",
    )(dest2, x.reshape(T, D), route.reshape(T, ROUTE_LANES), mod, yb)
    return out.reshape(B, S, D)


def _prep_params(p):
    f = {}
    w_in = p["w_in"]
    L = w_in.shape[0]
    z = lambda n: jnp.zeros((L, D_MODEL, n), F32)
    f["w_in"] = jnp.concatenate(
        [w_in[:, :, 0:768], w_in[:, :, 800:1056], w_in[:, :, 1056:1312], w_in[:, :, 1312:1440],
         z(MLA_NOPE), w_in[:, :, 1440:1472], z(HEAD_PAD - MLA_QK),
         w_in[:, :, 768:800], z(128 - 2 * GLA_GATE_RANK)], axis=-1).astype(BF16)
    up = jnp.zeros((L, 128, 256), F32)
    up = up.at[:, 0:16, 0:128].set(p["gla_gate_up_f"]).at[:, 16:32, 128:256].set(p["gla_gate_up_b"])
    f["gate_up"] = up.astype(BF16)
    f["gate_bias"] = jnp.concatenate([p["gla_gate_bias_f"], p["gla_gate_bias_b"]], axis=-1)[:, None, :]
    f["lora_g"] = jnp.concatenate([p["mla_q_lora_norm_g"], p["mla_kv_lora_norm_g"]], axis=-1)[:, None, :]
    f["gla_norm_g"] = jnp.tile(p["gla_out_norm_g"], (1, GLA_HEADS))[:, None, :]
    f["norm1_g"] = p["norm1_g"][:, None, :]
    f["norm2_g"] = p["norm2_g"][:, None, :]
    wq = p["mla_w_uq"].reshape(L, MLA_Q_LORA, MLA_HEADS, MLA_QK).transpose(0, 2, 1, 3)
    f["wq"] = jnp.pad(wq, ((0, 0), (0, 0), (0, 0), (0, HEAD_PAD - MLA_QK))).astype(BF16)
    wkv = p["mla_w_ukv"].reshape(L, MLA_KV_LORA, MLA_HEADS, MLA_NOPE + MLA_V).transpose(0, 2, 1, 3)
    f["wk"] = jnp.pad(wkv[..., :MLA_NOPE], ((0, 0), (0, 0), (0, 0), (0, HEAD_PAD - MLA_NOPE))).astype(BF16)
    wv = wkv[..., MLA_NOPE:]
    zv = jnp.zeros_like(wv)
    even = (jnp.arange(MLA_HEADS) % 2 == 0)[None, :, None, None]
    f["wv"] = jnp.where(even, jnp.concatenate([wv, zv], -1), jnp.concatenate([zv, wv], -1)).astype(BF16)
    pad_qk = ((0, 0), (0, HEAD_PAD - MLA_QK))
    f["qg"] = (jnp.pad(p["mla_q_norm_g"], pad_qk) * (MLA_QK ** -0.5 * LOG2E))[:, None, :]
    f["kg"] = jnp.pad(p["mla_k_norm_g"], pad_qk)[:, None, :]
    bound = (jnp.max(jnp.abs(p["mla_q_norm_g"]), axis=-1) * jnp.max(jnp.abs(p["mla_k_norm_g"]), axis=-1)
             * (math.sqrt(MLA_QK) * 1.01 * LOG2E) + 0.1).astype(BF16).astype(F32)
    use_bound = bound <= MAX_STATIC_SHIFT * LOG2E
    f["mla_flag"] = use_bound.astype(jnp.int32)[:, None]
    f["mla_shift"] = jnp.zeros((L, 1, HEAD_PAD), F32).at[:, 0, SHIFT_LANE].set(jnp.where(use_bound, -bound, 0.0))
    f["w_out"] = p["w_out"].astype(BF16)
    rw = jnp.concatenate([p["router_group_w"], p["router_expert_w"]], axis=-1)
    rw = jnp.pad(rw, ((0, 0), (0, 0), (0, ROUTE_LANES - N_GROUPS - N_EXPERTS)))
    rw_hi = rw.astype(BF16)
    rw_lo = (rw - rw_hi.astype(F32)).astype(BF16)
    f["rw"] = jnp.stack([rw_hi, rw_lo], axis=1)
    rb = jnp.concatenate([p["router_group_b"], p["router_expert_b"]], axis=-1)
    f["rb"] = jnp.pad(rb, ((0, 0), (0, ROUTE_LANES - N_GROUPS - N_EXPERTS)))[:, None, :]
    f["w1"] = p["expert_w1"].astype(BF16)
    f["w3"] = p["expert_w3"].astype(BF16)
    f["w2"] = p["expert_w2"].astype(BF16)
    return f


def _const_tables():
    c = np.arange(FNET_W)
    same = (c[:, None] // FNET_GROUP_DIM) == (c[None, :] // FNET_GROUP_DIM)
    ang = 2.0 * np.pi * ((c[:, None] % FNET_GROUP_DIM) * (c[None, :] % FNET_GROUP_DIM) % FNET_GROUP_DIM) / FNET_GROUP_DIM
    dft64 = np.stack([np.where(same, np.cos(ang), 0.0), np.where(same, np.sin(ang), 0.0)]).astype(np.float32)
    vone = np.zeros((MLA_HEADS, 1, HEAD_PAD), np.float32)
    vone[0::2, 0, MLA_V] = 1.0
    vone[1::2, 0, 0] = 1.0
    return jnp.asarray(dft64, BF16), jnp.asarray(vone)


def _seq_tables(S):
    j = lax.broadcasted_iota(jnp.int32, (S, S), 0)
    k = lax.broadcasted_iota(jnp.int32, (S, S), 1)
    ang = ((j * k) % S).astype(F32) * (2.0 * math.pi / S)
    a_dft = jnp.concatenate([jnp.cos(ang), -jnp.sin(ang)], axis=1).astype(BF16)
    half = MLA_ROPE // 2
    freqs = ROPE_THETA ** (-jnp.arange(half, dtype=F32) / half)
    ra = jnp.arange(S, dtype=F32)[:, None] * freqs[None, :]
    cos, sin = jnp.cos(ra), jnp.sin(ra)
    one = lambda n: jnp.ones((S, n), F32)
    zero = lambda n: jnp.zeros((S, n), F32)
    tail = HEAD_PAD - MLA_QK
    rope = jnp.stack([
        jnp.concatenate([one(MLA_NOPE), cos, cos, one(tail)], axis=1),
        jnp.concatenate([zero(MLA_NOPE), -sin, sin, zero(tail)], axis=1),
    ])
    return a_dft, rope


def _dispatch_plan(route2d, counts):
    T = route2d.shape[0]
    R = T * TOP_K
    cnt = counts[0, :N_EXPERTS].astype(jnp.int32)
    padded = (cnt + MOE_ROWS - 1) // MOE_ROWS * MOE_ROWS
    pend = jnp.cumsum(padded)
    pstart = pend - padded
    start = jnp.zeros((1, ROUTE_LANES), F32).at[0, :N_EXPERTS].set(pstart.astype(F32))
    dest = route_rank(route2d, start)[:, 0:TOP_K]
    P = R + N_EXPERTS * MOE_ROWS
    n_blk = P // MOE_ROWS
    blk_first = jnp.arange(n_blk, dtype=jnp.int32) * MOE_ROWS
    blk_e = jnp.minimum(jnp.sum((pend[None, :] <= blk_first[:, None]).astype(jnp.int32), axis=1), N_EXPERTS - 1)
    n_used = (pend[-1] // MOE_ROWS).astype(jnp.int32).reshape(1)
    return dest, blk_e, n_used, P


def _trunk(x, mod_all, f, consts, tables):
    B, S, D = x.shape
    T = B * S
    dft64, vone = consts
    a_dft, rope = tables
    tm = min(S, 512)
    tq = min(S, 512)
    tk = min(S, 512)
    dft_scale = 1.0 / math.sqrt(FNET_GROUP_DIM * S)
    for l in range(DEPTH):
        mod = mod_all[l].reshape(B, 6, D)
        qkv, og, gate, xcs, cq, ckv, kr = in_proj(
            x, mod, f["norm1_g"][l], f["w_in"][l], dft64, f["gate_up"][l], f["gate_bias"][l], f["lora_g"][l], tm)
        o_gla = gla(qkv, og, gate, f["gla_norm_g"][l])
        o_fft = dft_seq(a_dft, xcs.reshape(2 * S, B * FNET_W), dft_scale)
        o_mla = mla(f["mla_flag"][l], cq, ckv, kr, rope, f["wq"][l], f["wk"][l], f["wv"][l], f["qg"][l], f["kg"][l],
                    vone, f["mla_shift"][l], tq, tk)
        x, h, route, counts = out_proj(x, o_gla, o_fft, o_mla, mod, f["w_out"][l], f["norm2_g"][l], f["rw"][l],
                                       f["rb"][l], tm)
        dest, blk_e, n_used, n_rows = _dispatch_plan(route.reshape(T, ROUTE_LANES), counts)
        xb = dispatch_rows(h.reshape(T, D // 2), dest, n_rows)
        yb = moe_mlp(xb, blk_e, n_used, f["w1"][l], f["w3"][l], f["w2"][l])
        x = combine(x, yb, dest, route, mod)
    return x


def kernel(x_prompt, x_sample, c_prompt, c_sample, ada_w, ada_b, norm1_g, norm2_g, w_in, w_out, gla_gate_up_f, gla_gate_bias_f, gla_gate_up_b, gla_gate_bias_b, gla_out_norm_g, mla_q_lora_norm_g, mla_w_uq, mla_kv_lora_norm_g, mla_w_ukv, mla_q_norm_g, mla_k_norm_g, router_group_w, router_group_b, router_expert_w, router_expert_b, expert_w1, expert_w3, expert_w2):
    p = dict(norm1_g=norm1_g, norm2_g=norm2_g, w_in=w_in, w_out=w_out, gla_gate_up_f=gla_gate_up_f,
             gla_gate_bias_f=gla_gate_bias_f, gla_gate_up_b=gla_gate_up_b, gla_gate_bias_b=gla_gate_bias_b,
             gla_out_norm_g=gla_out_norm_g, mla_q_lora_norm_g=mla_q_lora_norm_g, mla_w_uq=mla_w_uq,
             mla_kv_lora_norm_g=mla_kv_lora_norm_g, mla_w_ukv=mla_w_ukv, mla_q_norm_g=mla_q_norm_g,
             mla_k_norm_g=mla_k_norm_g, router_group_w=router_group_w, router_group_b=router_group_b,
             router_expert_w=router_expert_w, router_expert_b=router_expert_b, expert_w1=expert_w1,
             expert_w3=expert_w3, expert_w2=expert_w2)
    f = _prep_params(p)
    consts = _const_tables()
    nb_p = c_prompt.shape[0]
    mod_all = ada_modulation(jnp.concatenate([c_prompt, c_sample], axis=0), ada_w, ada_b)
    y_prompt = _trunk(x_prompt, mod_all[:, :nb_p], f, consts, _seq_tables(x_prompt.shape[1]))
    y_sample = _trunk(x_sample, mod_all[:, nb_p:], f, consts, _seq_tables(x_sample.shape[1]))
    return (y_prompt, y_sample)
```

```python
import functools
import math

import jax
import jax.numpy as jnp
import numpy as np
from jax import lax
from jax.experimental import pallas as pl
from jax.experimental.pallas import tpu as pltpu
from jax.experimental.pallas import tpu_sc as plsc

F32 = jnp.float32
BF16 = jnp.bfloat16

D_MODEL = 1024
DEPTH = 4
EPS = 1e-6

GLA_HEADS = 4
GLA_DK = 32
GLA_DV = 64
GLA_QK_W = GLA_HEADS * GLA_DK
GLA_W = GLA_HEADS * GLA_DV
GLA_GATE_RANK = 16
GLA_GATE_NORMALIZER = 16.0
GLA_CHUNK = 32
GLA_BLOCK = 128

FNET_GROUPS = 4
FNET_GROUP_DIM = 64
FNET_W = 256

MLA_HEADS = 8
MLA_Q_LORA = 256
MLA_KV_LORA = 128
MLA_NOPE = 64
MLA_ROPE = 32
MLA_V = 64
MLA_QK = 96
MLA_W = 512
ROPE_THETA = 10000.0
HEAD_PAD = 128

N_GROUPS = 4
EXPERTS_PER_GROUP = 8
N_EXPERTS = 32
TOP_K = 2
D_EXPERT = 512
ROUTE_LANES = 128

C_Q, C_K, C_V, C_OG, C_F, C_CQ, C_CKV, C_KR, C_GATE = 0, 128, 256, 512, 768, 1024, 1280, 1408, 1536
P_IN_PAD = 1664

VMEM_LIMIT_BYTES = 56 * 1024 * 1024


def _cparams(sem, vmem=None):
    return pltpu.CompilerParams(dimension_semantics=sem, vmem_limit_bytes=vmem or VMEM_LIMIT_BYTES)


def _silu(x):
    return x * (1.0 / (1.0 + jnp.exp(-x)))


def _log_sigmoid(x):
    return -(jnp.maximum(-x, 0.0) + jnp.log1p(jnp.exp(-jnp.abs(x))))


def _dot(a, b):
    return jnp.dot(a, b, preferred_element_type=F32)


def _dot_nt(a, b):
    return lax.dot_general(a, b, (((1,), (1,)), ((), ())), preferred_element_type=F32)


def _split_bf16(x):
    hi = x.astype(BF16)
    lo = (x - hi.astype(F32)).astype(BF16)
    return hi, lo


def _pack_halves(x):
    n = x.shape[-1] // 2
    lo = lax.shift_right_logical(lax.bitcast_convert_type(x[:, :n], jnp.int32), 16)
    hi = lax.bitcast_convert_type(x[:, n:], jnp.int32) & jnp.int32(-65536)
    return hi | lo


def _unpack_halves(w):
    lo = lax.bitcast_convert_type(lax.shift_left(w, 16), F32)
    hi = lax.bitcast_convert_type(w & jnp.int32(-65536), F32)
    return lo, hi


def _ada_kernel(c_ref, w_ref, b_ref, o_ref):
    c = _silu(c_ref[...]).astype(BF16)
    o_ref[0] = _dot(c, w_ref[0].astype(BF16)) + b_ref[0]


def ada_modulation(c_all, ada_w, ada_b):
    nb = c_all.shape[0]
    tn = 1536
    n = ada_w.shape[-1]
    return pl.pallas_call(
        _ada_kernel,
        grid=(DEPTH, n // tn),
        in_specs=[
            pl.BlockSpec((nb, D_MODEL), lambda l, j: (0, 0)),
            pl.BlockSpec((1, D_MODEL, tn), lambda l, j: (l, 0, j)),
            pl.BlockSpec((1, 1, tn), lambda l, j: (l, 0, j)),
        ],
        out_specs=pl.BlockSpec((1, nb, tn), lambda l, j: (l, 0, j)),
        out_shape=jax.ShapeDtypeStruct((DEPTH, nb, n), F32),
        compiler_params=_cparams(("parallel", "parallel")),
        name="ada_modulation",
    )(c_all, ada_w, ada_b.reshape(DEPTH, 1, n))


def _in_proj_body(x, mod_ref, g_ref, w_ref, dft_ref, up_ref, gb_ref, lg_ref,
                  qkv_ref, og_ref, gate_ref, xcs_ref, cq_ref, ckv_ref, kr_ref):
    sh = mod_ref[0, 0:1, :]
    sc = mod_ref[0, 1:2, :]
    r = lax.rsqrt(jnp.mean(x * x, axis=-1, keepdims=True) + EPS)
    h = (x * r * g_ref[...]) * (1.0 + sc) + sh
    u = _dot(h.astype(BF16), w_ref[...])

    q = u[:, C_Q:C_K] * (GLA_DK ** -0.5)
    qkv_ref[:, 0:128] = q.astype(BF16)
    qkv_ref[:, 128:512] = u[:, C_K:C_OG].astype(BF16)
    og_ref[...] = u[:, C_OG:C_F].astype(BF16)

    ug = u[:, C_GATE:C_GATE + 128].astype(BF16)
    gl = _dot(ug, up_ref[...]) + gb_ref[...]
    gate_ref[...] = _log_sigmoid(gl) * (1.0 / GLA_GATE_NORMALIZER)

    uf = u[:, C_F:C_CQ].astype(BF16)
    xcs_ref[0] = _dot(uf, dft_ref[0]).astype(BF16)
    xcs_ref[1] = _dot(uf, dft_ref[1]).astype(BF16)

    cq = u[:, C_CQ:C_CKV]
    rq = lax.rsqrt(jnp.mean(cq * cq, axis=-1, keepdims=True) + EPS)
    cq_ref[...] = (cq * rq * lg_ref[:, 0:256]).astype(BF16)
    ckv = u[:, C_CKV:C_KR]
    rkv = lax.rsqrt(jnp.mean(ckv * ckv, axis=-1, keepdims=True) + EPS)
    ckv_ref[...] = (ckv * rkv * lg_ref[:, 256:384]).astype(BF16)
    kr_ref[...] = u[:, C_KR:C_GATE].astype(BF16)


def _in_proj_kernel(x_ref, mod_ref, g_ref, w_ref, dft_ref, up_ref, gb_ref, lg_ref,
                    qkv_ref, og_ref, gate_ref, xcs_ref, cq_ref, ckv_ref, kr_ref):
    _in_proj_body(x_ref[0], mod_ref, g_ref, w_ref, dft_ref, up_ref, gb_ref, lg_ref,
                  qkv_ref.at[0], og_ref.at[0], gate_ref.at[0], xcs_ref, cq_ref.at[0], ckv_ref.at[0], kr_ref.at[0])


def in_proj(x, mod, norm_g, w_in, dft64, gate_up, gate_bias, lora_g, tm):
    B, S, D = x.shape
    ns = S // tm
    row = lambda b, s: (b, s, 0)
    const2 = lambda b, s: (0, 0)
    out_shapes = (
        jax.ShapeDtypeStruct((B, S, 512), BF16),
        jax.ShapeDtypeStruct((B, S, GLA_W), BF16),
        jax.ShapeDtypeStruct((B, S, 256), F32),
        jax.ShapeDtypeStruct((2, S, B * FNET_W), BF16),
        jax.ShapeDtypeStruct((B, S, MLA_Q_LORA), BF16),
        jax.ShapeDtypeStruct((B, S, MLA_KV_LORA), BF16),
        jax.ShapeDtypeStruct((B, S, HEAD_PAD), BF16),
    )
    return pl.pallas_call(
        _in_proj_kernel,
        grid=(B, ns),
        in_specs=[
            pl.BlockSpec((1, tm, D), row),
            pl.BlockSpec((1, 6, D), lambda b, s: (b, 0, 0)),
            pl.BlockSpec((1, D), const2),
            pl.BlockSpec((D, P_IN_PAD), const2),
            pl.BlockSpec((2, FNET_W, FNET_W), lambda b, s: (0, 0, 0)),
            pl.BlockSpec((128, 256), const2),
            pl.BlockSpec((1, 256), const2),
            pl.BlockSpec((1, 384), const2),
        ],
        out_specs=(
            pl.BlockSpec((1, tm, 512), row),
            pl.BlockSpec((1, tm, GLA_W), row),
            pl.BlockSpec((1, tm, 256), row),
            pl.BlockSpec((2, tm, FNET_W), lambda b, s: (0, s, b)),
            pl.BlockSpec((1, tm, MLA_Q_LORA), row),
            pl.BlockSpec((1, tm, MLA_KV_LORA), row),
            pl.BlockSpec((1, tm, HEAD_PAD), row),
        ),
        out_shape=out_shapes,
        compiler_params=_cparams(("parallel", "parallel")),
        name="in_proj",
    )(x, mod, norm_g, w_in, dft64, gate_up, gate_bias, lora_g)


def _gla_kernel(qkv_ref, og_ref, gate_ref, ng_ref, o_ref, o_s, st_s, *, seq):
    R, C = GLA_BLOCK, GLA_CHUNK
    n_sub = R // C
    nblk = seq // R
    shift_c = int(math.log2(C))

    row = lax.broadcasted_iota(jnp.int32, (R, R), 0)
    col = lax.broadcasted_iota(jnp.int32, (R, R), 1)
    same = (row >> shift_c) == (col >> shift_c)
    ones_blk = jnp.where(same, 1.0, 0.0)
    lane_qk = lax.broadcasted_iota(jnp.int32, (1, GLA_QK_W), 1)
    lane_v = lax.broadcasted_iota(jnp.int32, (1, GLA_W), 1)
    head_qk = [(lane_qk >> 5) == h for h in range(GLA_HEADS)]
    head_v = [(lane_v >> 6) == h for h in range(GLA_HEADS)]
    st_row = lax.broadcasted_iota(jnp.int32, (GLA_W, GLA_QK_W), 0)
    st_col = lax.broadcasted_iota(jnp.int32, (GLA_W, GLA_QK_W), 1)
    st_mask = (st_row >> 6) == (st_col >> 5)
    sub_row = lax.broadcasted_iota(jnp.int32, (R, 1), 0) >> shift_c
    nrow = lax.broadcasted_iota(jnp.int32, (GLA_W, GLA_W), 0)
    ncol = lax.broadcasted_iota(jnp.int32, (GLA_W, GLA_W), 1)
    head_mean = jnp.where((nrow >> 6) == (ncol >> 6), 1.0 / GLA_DV, 0.0).astype(BF16)

    def direction(backward):
        if backward:
            tri = same & (col >= row)
            att_ok = same & (col > row)
            order = list(range(n_sub - 1, -1, -1))
        else:
            tri = same & (col <= row)
            att_ok = same & (col <= row)
            order = list(range(n_sub))
        cum_lhs = jnp.concatenate([jnp.where(tri, 1.0, 0.0), ones_blk], axis=0).astype(BF16)
        att_ok4 = jnp.concatenate([att_ok] * GLA_HEADS, axis=1)
        return cum_lhs, att_ok4, order, (GLA_QK_W if backward else 0), (1 if backward else 0)

    def block(j, consts):
        cum_lhs, att_ok4, order, gate_off, slot = consts
        r0 = pl.multiple_of(j * R, R)
        qkv = qkv_ref[0, pl.ds(r0, R), :]
        q = qkv[:, 0:128].astype(F32)
        k = qkv[:, 128:256].astype(F32)
        v = qkv[:, 256:512]
        g = gate_ref[0, pl.ds(r0, R), gate_off:gate_off + GLA_QK_W]
        g_hi, g_lo = _split_bf16(g)
        cs = _dot(cum_lhs, jnp.concatenate([g_hi, g_lo], axis=1))
        b = cs[0:R, 0:128] + cs[0:R, 128:256]
        bl = cs[R:2 * R, 0:128] + cs[R:2 * R, 128:256]
        q_dec = q * jnp.exp(b)
        k_inv = (k * jnp.exp(-b)).astype(BF16)
        k_end = k * jnp.exp(bl - b)
        decay = jnp.exp(bl)

        att = jnp.concatenate(
            [_dot_nt(jnp.where(head_qk[h], q_dec, 0.0).astype(BF16), k_inv) for h in range(GLA_HEADS)],
            axis=1)
        att = jnp.where(att_ok4, att, 0.0).astype(BF16)
        v_heads = jnp.concatenate([jnp.where(head_v[h], v, jnp.zeros_like(v)) for h in range(GLA_HEADS)], axis=0)
        o_blk = _dot(att, v_heads)

        v_t = v.astype(F32).T.astype(BF16)
        q_dec_b = q_dec.astype(BF16)
        inter = [None] * n_sub
        for c in order:
            st = st_s[slot]
            inter[c] = _dot_nt(q_dec_b[c * C:(c + 1) * C, :], st.astype(BF16))
            k_c = jnp.where(sub_row == c, k_end, 0.0).astype(BF16)
            d_st = _dot(v_t, k_c)
            st_s[slot] = st * decay[c * C:c * C + 1, :] + jnp.where(st_mask, d_st, 0.0)
        o_s[slot, pl.ds(r0, R), :] = o_blk + jnp.concatenate(inter, axis=0)

    fwd = direction(False)
    bwd = direction(True)
    st_s[...] = jnp.zeros_like(st_s)

    def scan_step(i, carry):
        block(i, fwd)
        block(nblk - 1 - i, bwd)
        return carry

    lax.fori_loop(0, nblk, scan_step, 0)

    def finish(j, carry):
        r0 = pl.multiple_of(j * R, R)
        o = o_s[0, pl.ds(r0, R), :] + o_s[1, pl.ds(r0, R), :]
        sq_hi, sq_lo = _split_bf16(o * o)
        ms = _dot(sq_hi, head_mean) + _dot(sq_lo, head_mean)
        y = o * lax.rsqrt(ms + EPS) * ng_ref[...]
        og = og_ref[0, pl.ds(r0, R), :].astype(F32)
        o_ref[0, pl.ds(r0, R), :] = (y * _silu(og)).astype(o_ref.dtype)
        return carry

    lax.fori_loop(0, nblk, finish, 0)


def gla(qkv, og, gate, norm_g4):
    B, S, _ = qkv.shape
    full = lambda b: (b, 0, 0)
    return pl.pallas_call(
        functools.partial(_gla_kernel, seq=S),
        grid=(B,),
        in_specs=[
            pl.BlockSpec((1, S, 512), full),
            pl.BlockSpec((1, S, GLA_W), full),
            pl.BlockSpec((1, S, 256), full),
            pl.BlockSpec((1, GLA_W), lambda b: (0, 0)),
        ],
        out_specs=pl.BlockSpec((1, S, GLA_W), full),
        out_shape=jax.ShapeDtypeStruct((B, S, GLA_W), BF16),
        scratch_shapes=[pltpu.VMEM((2, S, GLA_W), F32), pltpu.VMEM((2, GLA_W, GLA_QK_W), F32)],
        compiler_params=_cparams(("parallel",)),
        name="gla",
    )(qkv, og, gate, norm_g4)


def _dft_kernel(a_ref, b_ref, o_ref, acc_ref, *, scale):
    kk = pl.program_id(2)

    @pl.when(kk == 0)
    def _():
        acc_ref[...] = jnp.zeros_like(acc_ref)

    acc_ref[...] += _dot(a_ref[...], b_ref[...])

    @pl.when(kk == pl.num_programs(2) - 1)
    def _():
        o_ref[...] = (acc_ref[...] * scale).astype(o_ref.dtype)


def dft_seq(a_dft, xcs, scale):
    m, k = a_dft.shape
    n = xcs.shape[1]
    tm, tn, tk = min(m, 1024), min(n, 1024), min(k, 2048)
    return pl.pallas_call(
        functools.partial(_dft_kernel, scale=scale),
        grid=(m // tm, n // tn, k // tk),
        in_specs=[pl.BlockSpec((tm, tk), lambda i, j, kk: (i, kk)),
                  pl.BlockSpec((tk, tn), lambda i, j, kk: (kk, j))],
        out_specs=pl.BlockSpec((tm, tn), lambda i, j, kk: (i, j)),
        out_shape=jax.ShapeDtypeStruct((m, n), BF16),
        scratch_shapes=[pltpu.VMEM((tm, tn), F32)],
        compiler_params=_cparams(("parallel", "parallel", "arbitrary")),
        name="dft_seq",
    )(a_dft, xcs)


def _rope(x, cos, sin, swap):
    return x * cos + _dot(x.astype(BF16), swap) * sin


SHIFT_LANE = HEAD_PAD - 1
MAX_STATIC_SHIFT = 40.0
LOG2E = math.log2(math.e)


def _mla_kernel(flag_ref, cq_ref, ckv_ref, kr_ref, rope_ref, wq_ref, wk_ref, wv_ref, qg_ref, kg_ref, vone_ref,
                shift_ref, o_ref, kt_s, v_s, q_s, acc_s, *, seq, tq, tk):
    qi = pl.program_id(1)
    inv_qk = 1.0 / MLA_QK
    lane = lax.broadcasted_iota(jnp.int32, (1, HEAD_PAD), 1)
    src = lax.broadcasted_iota(jnp.int32, (HEAD_PAD, HEAD_PAD), 0)
    dst = lax.broadcasted_iota(jnp.int32, (HEAD_PAD, HEAD_PAD), 1)
    half = MLA_ROPE // 2
    lo_half = (dst >= MLA_NOPE) & (dst < MLA_NOPE + half)
    hi_half = (dst >= MLA_NOPE + half) & (dst < MLA_QK)
    swap = jnp.where((lo_half & (src == dst + half)) | (hi_half & (src == dst - half)), 1.0, 0.0).astype(BF16)

    @pl.when(qi == 0)
    def _():
        one_lane = jnp.where(lane == SHIFT_LANE, 1.0, 0.0)

        def rows(i, carry):
            r0 = pl.multiple_of(i * tk, tk)
            ckv = ckv_ref[0, pl.ds(r0, tk), :]
            kr = kr_ref[0, pl.ds(r0, tk), :].astype(F32)
            cos = rope_ref[0, pl.ds(r0, tk), :]
            sin = rope_ref[1, pl.ds(r0, tk), :]
            for h in range(MLA_HEADS):
                kp = _dot(ckv, wk_ref[h]) + kr
                r = lax.rsqrt(jnp.sum(kp * kp, axis=-1, keepdims=True) * inv_qk + EPS)
                kn = _rope(kp * r * kg_ref[...], cos, sin, swap) + one_lane
                kt_s[h, i] = kn.T.astype(BF16)
                v_s[h, pl.ds(r0, tk), :] = (_dot(ckv, wv_ref[h]) + vone_ref[h]).astype(BF16)
            return carry

        lax.fori_loop(0, seq // tk, rows, 0)

    q0 = pl.multiple_of(qi * tq, tq)
    cos = rope_ref[0, pl.ds(q0, tq), :]
    sin = rope_ref[1, pl.ds(q0, tq), :]
    for h in range(MLA_HEADS):
        qp = _dot(cq_ref[0], wq_ref[h])
        r = lax.rsqrt(jnp.sum(qp * qp, axis=-1, keepdims=True) * inv_qk + EPS)
        q_s[h] = (_rope(qp * r * qg_ref[...], cos, sin, swap) + shift_ref[...]).astype(BF16)

    def finish():
        for pair in range(MLA_HEADS // 2):
            outs = []
            for h in (2 * pair, 2 * pair + 1):
                acc = acc_s[h]
                den_lane = MLA_V if h % 2 == 0 else 0
                den = jnp.sum(jnp.where(lane == den_lane, acc, 0.0), axis=-1, keepdims=True)
                outs.append(acc * (1.0 / den))
            both = jnp.where(lane < MLA_V, outs[0], outs[1])
            o_ref[0, :, pair * HEAD_PAD:(pair + 1) * HEAD_PAD] = both.astype(o_ref.dtype)

    @pl.when(flag_ref[0] == 1)
    def _():
        acc_s[...] = jnp.zeros_like(acc_s)

        def kv_step(j, carry):
            k0 = pl.multiple_of(j * tk, tk)
            for h in range(MLA_HEADS):
                p = jnp.exp2(_dot(q_s[h], kt_s[h, j])).astype(BF16)
                acc_s[h] += _dot(p, v_s[h, pl.ds(k0, tk), :])
            return carry

        lax.fori_loop(0, seq // tk, kv_step, 0)
        finish()

    @pl.when(flag_ref[0] == 0)
    def _():
        for h in range(MLA_HEADS):
            def kv_step(j, carry, h=h):
                m, acc = carry
                k0 = pl.multiple_of(j * tk, tk)
                s = _dot(q_s[h], kt_s[h, j])
                m_new = jnp.maximum(m, jnp.max(s, axis=-1, keepdims=True))
                p = jnp.exp2(s - m_new).astype(BF16)
                return m_new, jnp.exp2(m - m_new) * acc + _dot(p, v_s[h, pl.ds(k0, tk), :])

            m0 = jnp.full((tq, 1), -jnp.inf, F32)
            _, acc = lax.fori_loop(0, seq // tk, kv_step, (m0, jnp.zeros((tq, HEAD_PAD), F32)))
            acc_s[h] = acc
        finish()


def mla(flag, cq, ckv, kr, rope, wq, wk, wv, qg, kg, vone, shift, tq, tk):
    B, S, _ = cq.shape
    c3 = lambda b, q: (0, 0, 0)
    c2 = lambda b, q: (0, 0)
    return pl.pallas_call(
        functools.partial(_mla_kernel, seq=S, tq=tq, tk=tk),
        grid=(B, S // tq),
        in_specs=[
            pl.BlockSpec(memory_space=pltpu.SMEM),
            pl.BlockSpec((1, tq, MLA_Q_LORA), lambda b, q: (b, q, 0)),
            pl.BlockSpec((1, S, MLA_KV_LORA), lambda b, q: (b, 0, 0)),
            pl.BlockSpec((1, S, HEAD_PAD), lambda b, q: (b, 0, 0)),
            pl.BlockSpec((2, S, HEAD_PAD), c3),
            pl.BlockSpec((MLA_HEADS, MLA_Q_LORA, HEAD_PAD), c3),
            pl.BlockSpec((MLA_HEADS, MLA_KV_LORA, HEAD_PAD), c3),
            pl.BlockSpec((MLA_HEADS, MLA_KV_LORA, HEAD_PAD), c3),
            pl.BlockSpec((1, HEAD_PAD), c2),
            pl.BlockSpec((1, HEAD_PAD), c2),
            pl.BlockSpec((MLA_HEADS, 1, HEAD_PAD), c3),
            pl.BlockSpec((1, HEAD_PAD), c2),
        ],
        out_specs=pl.BlockSpec((1, tq, MLA_W), lambda b, q: (b, q, 0)),
        out_shape=jax.ShapeDtypeStruct((B, S, MLA_W), BF16),
        scratch_shapes=[pltpu.VMEM((MLA_HEADS, S // tk, HEAD_PAD, tk), BF16),
                        pltpu.VMEM((MLA_HEADS, S, HEAD_PAD), BF16),
                        pltpu.VMEM((MLA_HEADS, tq, HEAD_PAD), BF16),
                        pltpu.VMEM((MLA_HEADS, tq, HEAD_PAD), F32)],
        compiler_params=_cparams(("parallel", "arbitrary")),
        name="mla",
    )(flag, cq, ckv, kr, rope, wq, wk, wv, qg, kg, vone, shift)


def _out_proj_kernel(x_ref, gla_ref, fft_ref, mla_ref, mod_ref, w_ref, g_ref, rw_ref, rb_ref,
                     xo_ref, h_ref, route_ref, cnt_ref):
    mix = (_dot(gla_ref[0], w_ref[0:256, :]) + _dot(fft_ref[...], w_ref[256:512, :])
           + _dot(mla_ref[0], w_ref[512:1024, :]))
    g1 = mod_ref[0, 2:3, :]
    sh = mod_ref[0, 3:4, :]
    sc = mod_ref[0, 4:5, :]
    x = x_ref[0] + g1 * mix
    xo_ref[0] = x
    r = lax.rsqrt(jnp.mean(x * x, axis=-1, keepdims=True) + EPS)
    h = (x * r * g_ref[...]) * (1.0 + sc) + sh
    h_hi, h_lo = _split_bf16(h)
    h_ref[0] = _pack_halves(h_hi.astype(F32))

    logit = _dot(h_hi, rw_ref[0]) + _dot(h_lo, rw_ref[0]) + _dot(h_hi, rw_ref[1]) + rb_ref[...]
    lane = lax.broadcasted_iota(jnp.int32, (1, ROUTE_LANES), 1)
    lane_f = lane.astype(F32)
    neg = -1e30
    is_g = lane < N_GROUPS
    is_e = (lane >= N_GROUPS) & (lane < N_GROUPS + N_EXPERTS)
    lg = jnp.where(is_g, logit, neg)
    g_max = jnp.max(lg, axis=-1, keepdims=True)
    g_den = jnp.sum(jnp.where(is_g, jnp.exp(lg - g_max), 0.0), axis=-1, keepdims=True)
    g_w = 1.0 / g_den
    g_top = jnp.min(jnp.where(is_g & (logit == g_max), lane_f, 1e9), axis=-1, keepdims=True)
    e_grp = ((lane - N_GROUPS) >> 3).astype(F32)
    in_grp = is_e & (e_grp == g_top)
    le = jnp.where(in_grp, logit, neg)
    t1 = jnp.max(le, axis=-1, keepdims=True)
    i1 = jnp.min(jnp.where(in_grp & (le == t1), lane_f, 1e9), axis=-1, keepdims=True)
    le2 = jnp.where(lane_f == i1, neg, le)
    t2 = jnp.max(le2, axis=-1, keepdims=True)
    i2 = jnp.min(jnp.where(in_grp & (le2 == t2), lane_f, 1e9), axis=-1, keepdims=True)
    e21 = jnp.exp(t2 - t1)
    w1 = g_w / (1.0 + e21)
    w2 = w1 * e21
    route = jnp.where(lane == 0, i1 - N_GROUPS,
                      jnp.where(lane == 1, i2 - N_GROUPS,
                                jnp.where(lane == 2, w1, jnp.where(lane == 3, w2, 0.0))))
    route_ref[0] = route

    @pl.when((pl.program_id(0) == 0) & (pl.program_id(1) == 0))
    def _():
        cnt_ref[...] = jnp.zeros_like(cnt_ref)

    picked = jnp.where((lane_f == i1 - N_GROUPS) | (lane_f == i2 - N_GROUPS), 1.0, 0.0)
    cnt_ref[...] += jnp.sum(picked, axis=0, keepdims=True)


def out_proj(x, o_gla, o_fft, o_mla, mod, w_out, norm_g, rw, rb, tm):
    B, S, D = x.shape
    row = lambda b, s: (b, s, 0)
    c2 = lambda b, s: (0, 0)
    return pl.pallas_call(
        _out_proj_kernel,
        grid=(B, S // tm),
        in_specs=[
            pl.BlockSpec((1, tm, D), row),
            pl.BlockSpec((1, tm, GLA_W), row),
            pl.BlockSpec((tm, FNET_W), lambda b, s: (s, b)),
            pl.BlockSpec((1, tm, MLA_W), row),
            pl.BlockSpec((1, 6, D), lambda b, s: (b, 0, 0)),
            pl.BlockSpec((D, D), c2),
            pl.BlockSpec((1, D), c2),
            pl.BlockSpec((2, D, ROUTE_LANES), lambda b, s: (0, 0, 0)),
            pl.BlockSpec((1, ROUTE_LANES), c2),
        ],
        out_specs=(pl.BlockSpec((1, tm, D), row), pl.BlockSpec((1, tm, D // 2), row),
                   pl.BlockSpec((1, tm, ROUTE_LANES), row), pl.BlockSpec((1, ROUTE_LANES), c2)),
        out_shape=(jax.ShapeDtypeStruct((B, S, D), F32), jax.ShapeDtypeStruct((B, S, D // 2), jnp.int32),
                   jax.ShapeDtypeStruct((B, S, ROUTE_LANES), F32), jax.ShapeDtypeStruct((1, ROUTE_LANES), F32)),
        compiler_params=_cparams(("arbitrary", "arbitrary")),
        name="out_proj",
    )(x, o_gla, o_fft, o_mla, mod, w_out, norm_g, rw, rb)


RANK_ROWS = 512


def _rank_kernel(route_ref, start_ref, dest_ref, carry_ref, before_ref):
    n = route_ref.shape[0]

    @pl.when(pl.program_id(0) == 0)
    def _():
        carry_ref[...] = jnp.zeros_like(carry_ref)
        row = lax.broadcasted_iota(jnp.int32, (n, n), 0)
        col = lax.broadcasted_iota(jnp.int32, (n, n), 1)
        before_ref[...] = jnp.where(col < row, 1.0, 0.0).astype(BF16)

    lane_f = lax.broadcasted_iota(jnp.int32, (1, ROUTE_LANES), 1).astype(F32)
    e1 = route_ref[:, 0:1]
    e2 = route_ref[:, 1:2]
    oh1 = jnp.where(lane_f == e1, 1.0, 0.0)
    oh2 = jnp.where(lane_f == e2, 1.0, 0.0)
    both = (oh1 + oh2).astype(BF16)
    pos = _dot(before_ref[...], both) + carry_ref[...] + start_ref[...]
    d1 = jnp.sum(oh1 * pos, axis=-1, keepdims=True)
    d2 = jnp.sum(oh2 * pos, axis=-1, keepdims=True)
    lane = lax.broadcasted_iota(jnp.int32, (1, ROUTE_LANES), 1)
    dest_ref[...] = jnp.where(lane == 0, d1, jnp.where(lane == 1, d2, 0.0)).astype(jnp.int32)
    carry_ref[...] += jnp.sum(oh1 + oh2, axis=0, keepdims=True)


def route_rank(route2d, start):
    T = route2d.shape[0]
    rr = min(T, RANK_ROWS)
    return pl.pallas_call(
        _rank_kernel,
        grid=(T // rr,),
        in_specs=[pl.BlockSpec((rr, ROUTE_LANES), lambda i: (i, 0)),
                  pl.BlockSpec((1, ROUTE_LANES), lambda i: (0, 0))],
        out_specs=pl.BlockSpec((rr, ROUTE_LANES), lambda i: (i, 0)),
        out_shape=jax.ShapeDtypeStruct((T, ROUTE_LANES), jnp.int32),
        scratch_shapes=[pltpu.VMEM((1, ROUTE_LANES), F32), pltpu.VMEM((rr, rr), BF16)],
        compiler_params=_cparams(("arbitrary",)),
        name="route_rank",
    )(route2d, start)


MOE_ROWS = 512


def _moe_kernel(blk_e_ref, n_used_ref, x_ref, w1_ref, w3_ref, w2_ref, o_ref):
    i = pl.program_id(0)

    @pl.when(i < n_used_ref[0])
    def _():
        half = D_MODEL // 2
        x_lo, x_hi = _unpack_halves(x_ref[...])
        x_lo = x_lo.astype(BF16)
        x_hi = x_hi.astype(BF16)
        a = _dot(x_lo, w1_ref[0, 0:half, :]) + _dot(x_hi, w1_ref[0, half:, :])
        b = _dot(x_lo, w3_ref[0, 0:half, :]) + _dot(x_hi, w3_ref[0, half:, :])
        hm = (_silu(a) * b).astype(BF16)
        y = _dot(hm, w2_ref[0])
        o_ref[...] = _pack_halves(y.astype(BF16).astype(F32))

    @pl.when(i >= n_used_ref[0])
    def _():
        o_ref[...] = jnp.zeros_like(o_ref)


def moe_mlp(xb, blk_e, n_used, w1, w3, w2):
    P, half = xb.shape
    n_blk = P // MOE_ROWS
    wmap = lambda i, be, nu: (be[i], 0, 0)
    grid_spec = pltpu.PrefetchScalarGridSpec(
        num_scalar_prefetch=2,
        grid=(n_blk,),
        in_specs=[
            pl.BlockSpec((MOE_ROWS, half), lambda i, be, nu: (i, 0)),
            pl.BlockSpec((1, D_MODEL, D_EXPERT), wmap),
            pl.BlockSpec((1, D_MODEL, D_EXPERT), wmap),
            pl.BlockSpec((1, D_EXPERT, D_MODEL), wmap),
        ],
        out_specs=pl.BlockSpec((MOE_ROWS, half), lambda i, be, nu: (i, 0)),
    )
    return pl.pallas_call(
        _moe_kernel,
        grid_spec=grid_spec,
        out_shape=jax.ShapeDtypeStruct((P, half), jnp.int32),
        compiler_params=_cparams(("arbitrary",)),
        name="moe_mlp",
    )(blk_e, n_used, xb, w1, w3, w2)


ROW_TILE = 512


def _dispatch_kernel(dest_ref, h_ref, init_ref, xb_ref, stage, sem):
    del init_ref
    n = h_ref.shape[0]
    stage[...] = h_ref[...]
    for r in range(n):
        for c in range(TOP_K):
            d = dest_ref[0, 0, TOP_K * r + c]
            pltpu.make_async_copy(stage.at[pl.ds(r, 1)], xb_ref.at[pl.ds(d, 1)], sem).start()
    for c in range(TOP_K):
        pltpu.make_async_copy(stage, xb_ref.at[pl.ds(0, n)], sem).wait()


def dispatch_rows(hp, dest, n_rows):
    T, half = hp.shape
    rt = min(T, ROW_TILE)
    dest2 = dest.reshape(T // rt, 1, TOP_K * rt)
    init = jnp.zeros((n_rows, half), jnp.int32)
    return pl.pallas_call(
        _dispatch_kernel,
        grid=(T // rt,),
        in_specs=[
            pl.BlockSpec((1, 1, TOP_K * rt), lambda i: (i, 0, 0), memory_space=pltpu.SMEM),
            pl.BlockSpec((rt, half), lambda i: (i, 0)),
            pl.BlockSpec(memory_space=pl.ANY),
        ],
        out_specs=pl.BlockSpec(memory_space=pl.ANY),
        out_shape=jax.ShapeDtypeStruct((n_rows, half), jnp.int32),
        scratch_shapes=[pltpu.VMEM((rt, half), jnp.int32), pltpu.SemaphoreType.DMA(())],
        input_output_aliases={2: 0},
        compiler_params=_cparams(("arbitrary",)),
        name="dispatch_rows",
    )(dest2, hp, init)


def _combine_kernel(dest_ref, x_ref, route_ref, mod_ref, y_ref, o_ref, buf, sem):
    i = pl.program_id(0)
    n_tiles = pl.num_programs(0) - 1
    n = x_ref.shape[0]

    for s in range(2):
        @pl.when((i < n_tiles) & (i % 2 == s))
        def _(s=s):
            for r in range(n):
                for c in range(TOP_K):
                    d = dest_ref[0, 0, TOP_K * r + c]
                    pltpu.make_async_copy(y_ref.at[pl.ds(d, 1)], buf.at[s, c, pl.ds(r, 1)], sem.at[s]).start()

    @pl.when(i > 0)
    def _():
        s = (i - 1) % 2
        for c in range(TOP_K):
            pltpu.make_async_copy(y_ref.at[pl.ds(0, n)], buf.at[s, c], sem.at[s]).wait()
        g2 = mod_ref[0, 5:6, :]
        wa = route_ref[:, 2:3]
        wb = route_ref[:, 3:4]
        half = D_MODEL // 2
        a_lo, a_hi = _unpack_halves(buf[s, 0])
        b_lo, b_hi = _unpack_halves(buf[s, 1])
        o_ref[:, 0:half] = x_ref[:, 0:half] + g2[:, 0:half] * (wa * a_lo + wb * b_lo)
        o_ref[:, half:] = x_ref[:, half:] + g2[:, half:] * (wa * a_hi + wb * b_hi)


SC_WINDOW = 128


def sc_gather_rows(table, idx):
    n = idx.shape[0]
    width = table.shape[1]
    mesh = plsc.VectorSubcoreMesh(core_axis_name="core", subcore_axis_name="subcore")

    @functools.partial(pl.kernel, out_type=jax.ShapeDtypeStruct((n, width), table.dtype), mesh=mesh,
                       name="sc_gather_rows")
    def gather_kernel(x_hbm, i_hbm, o_hbm):
        def body(i_vmem, o_vmem):
            pltpu.sync_copy(x_hbm.at[i_vmem.at[0]], o_vmem)

        pltpu.emit_pipeline(
            body,
            grid=(n // SC_WINDOW,),
            in_specs=[pl.BlockSpec((1, SC_WINDOW), lambda i: (0, i))],
            out_specs=[pl.BlockSpec((SC_WINDOW, width), lambda i: (i, 0), pipeline_mode=pl.Buffered(1))],
            core_axis_name=("core", "subcore"),
            dimension_semantics=(pltpu.PARALLEL,),
        )(i_hbm, o_hbm)

    return gather_kernel(table, idx.reshape(1, n))


def _combine_dense_kernel(x_ref, ya_ref, yb_ref, route_ref, mod_ref, o_ref):
    g2 = mod_ref[0, 5:6, :]
    wa = route_ref[:, 2:3]
    wb = route_ref[:, 3:4]
    half = D_MODEL // 2
    a_lo, a_hi = _unpack_halves(ya_ref[...])
    b_lo, b_hi = _unpack_halves(yb_ref[...])
    o_ref[:, 0:half] = x_ref[:, 0:half] + g2[:, 0:half] * (wa * a_lo + wb * b_lo)
    o_ref[:, half:] = x_ref[:, half:] + g2[:, half:] * (wa * a_hi + wb * b_hi)


def combine_sc(x, yb, dest, route, mod):
    B, S, D = x.shape
    T = B * S
    rt = min(S, ROW_TILE)
    per_b = S // rt
    ya = sc_gather_rows(yb, dest[:, 0])
    yc = sc_gather_rows(yb, dest[:, 1])
    row = lambda i: (i, 0)
    out = pl.pallas_call(
        _combine_dense_kernel,
        grid=(T // rt,),
        in_specs=[
            pl.BlockSpec((rt, D), row),
            pl.BlockSpec((rt, D // 2), row),
            pl.BlockSpec((rt, D // 2), row),
            pl.BlockSpec((rt, ROUTE_LANES), row),
            pl.BlockSpec((1, 6, D), lambda i: (i // per_b, 0, 0)),
        ],
        out_specs=pl.BlockSpec((rt, D), row),
        out_shape=jax.ShapeDtypeStruct((T, D), F32),
        compiler_params=_cparams(("parallel",)),
        name="combine_dense",
    )(x.reshape(T, D), ya, yc, route.reshape(T, ROUTE_LANES), mod)
    return out.reshape(B, S, D)


def combine(x, yb, dest, route, mod):
    B, S, D = x.shape
    T = B * S
    rt = min(S, ROW_TILE)
    n_tiles = T // rt
    per_b = S // rt
    dest2 = dest.reshape(n_tiles, 1, TOP_K * rt)
    prev = lambda i: jnp.maximum(i - 1, 0)
    out = pl.pallas_call(
        _combine_kernel,
        grid=(n_tiles + 1,),
        in_specs=[
            pl.BlockSpec((1, 1, TOP_K * rt), lambda i: (jnp.minimum(i, n_tiles - 1), 0, 0),
                         memory_space=pltpu.SMEM),
            pl.BlockSpec((rt, D), lambda i: (prev(i), 0)),
            pl.BlockSpec((rt, ROUTE_LANES), lambda i: (prev(i), 0)),
            pl.BlockSpec((1, 6, D), lambda i: (prev(i) // per_b, 0, 0)),
            pl.BlockSpec(memory_space=pl.ANY),
        ],
        out_specs=pl.BlockSpec((rt, D), lambda i: (prev(i), 0)),
        out_shape=jax.ShapeDtypeStruct((T, D), F32),
        scratch_shapes=[pltpu.VMEM((2, TOP_K, rt, D // 2), jnp.int32), pltpu.SemaphoreType.DMA((2,))],
        compiler_params=_cparams(("arbitrary",)),
        name="combine",
    )(dest2, x.reshape(T, D), route.reshape(T, ROUTE_LANES), mod, yb)
    return out.reshape(B, S, D)


def _prep_params(p):
    f = {}
    w_in = p["w_in"]
    L = w_in.shape[0]
    z = lambda n: jnp.zeros((L, D_MODEL, n), F32)
    f["w_in"] = jnp.concatenate(
        [w_in[:, :, 0:768], w_in[:, :, 800:1056], w_in[:, :, 1056:1312], w_in[:, :, 1312:1440],
         z(MLA_NOPE), w_in[:, :, 1440:1472], z(HEAD_PAD - MLA_QK),
         w_in[:, :, 768:800], z(128 - 2 * GLA_GATE_RANK)], axis=-1).astype(BF16)
    up = jnp.zeros((L, 128, 256), F32)
    up = up.at[:, 0:16, 0:128].set(p["gla_gate_up_f"]).at[:, 16:32, 128:256].set(p["gla_gate_up_b"])
    f["gate_up"] = up.astype(BF16)
    f["gate_bias"] = jnp.concatenate([p["gla_gate_bias_f"], p["gla_gate_bias_b"]], axis=-1)[:, None, :]
    f["lora_g"] = jnp.concatenate([p["mla_q_lora_norm_g"], p["mla_kv_lora_norm_g"]], axis=-1)[:, None, :]
    f["gla_norm_g"] = jnp.tile(p["gla_out_norm_g"], (1, GLA_HEADS))[:, None, :]
    f["norm1_g"] = p["norm1_g"][:, None, :]
    f["norm2_g"] = p["norm2_g"][:, None, :]
    wq = p["mla_w_uq"].reshape(L, MLA_Q_LORA, MLA_HEADS, MLA_QK).transpose(0, 2, 1, 3)
    f["wq"] = jnp.pad(wq, ((0, 0), (0, 0), (0, 0), (0, HEAD_PAD - MLA_QK))).astype(BF16)
    wkv = p["mla_w_ukv"].reshape(L, MLA_KV_LORA, MLA_HEADS, MLA_NOPE + MLA_V).transpose(0, 2, 1, 3)
    f["wk"] = jnp.pad(wkv[..., :MLA_NOPE], ((0, 0), (0, 0), (0, 0), (0, HEAD_PAD - MLA_NOPE))).astype(BF16)
    wv = wkv[..., MLA_NOPE:]
    zv = jnp.zeros_like(wv)
    even = (jnp.arange(MLA_HEADS) % 2 == 0)[None, :, None, None]
    f["wv"] = jnp.where(even, jnp.concatenate([wv, zv], -1), jnp.concatenate([zv, wv], -1)).astype(BF16)
    pad_qk = ((0, 0), (0, HEAD_PAD - MLA_QK))
    f["qg"] = (jnp.pad(p["mla_q_norm_g"], pad_qk) * (MLA_QK ** -0.5 * LOG2E))[:, None, :]
    f["kg"] = jnp.pad(p["mla_k_norm_g"], pad_qk)[:, None, :]
    bound = (jnp.max(jnp.abs(p["mla_q_norm_g"]), axis=-1) * jnp.max(jnp.abs(p["mla_k_norm_g"]), axis=-1)
             * (math.sqrt(MLA_QK) * 1.01 * LOG2E) + 0.1).astype(BF16).astype(F32)
    use_bound = bound <= MAX_STATIC_SHIFT * LOG2E
    f["mla_flag"] = use_bound.astype(jnp.int32)[:, None]
    f["mla_shift"] = jnp.zeros((L, 1, HEAD_PAD), F32).at[:, 0, SHIFT_LANE].set(jnp.where(use_bound, -bound, 0.0))
    f["w_out"] = p["w_out"].astype(BF16)
    rw = jnp.concatenate([p["router_group_w"], p["router_expert_w"]], axis=-1)
    rw = jnp.pad(rw, ((0, 0), (0, 0), (0, ROUTE_LANES - N_GROUPS - N_EXPERTS)))
    rw_hi = rw.astype(BF16)
    rw_lo = (rw - rw_hi.astype(F32)).astype(BF16)
    f["rw"] = jnp.stack([rw_hi, rw_lo], axis=1)
    rb = jnp.concatenate([p["router_group_b"], p["router_expert_b"]], axis=-1)
    f["rb"] = jnp.pad(rb, ((0, 0), (0, ROUTE_LANES - N_GROUPS - N_EXPERTS)))[:, None, :]
    f["w1"] = p["expert_w1"].astype(BF16)
    f["w3"] = p["expert_w3"].astype(BF16)
    f["w2"] = p["expert_w2"].astype(BF16)
    return f


def _const_tables():
    c = np.arange(FNET_W)
    same = (c[:, None] // FNET_GROUP_DIM) == (c[None, :] // FNET_GROUP_DIM)
    ang = 2.0 * np.pi * ((c[:, None] % FNET_GROUP_DIM) * (c[None, :] % FNET_GROUP_DIM) % FNET_GROUP_DIM) / FNET_GROUP_DIM
    dft64 = np.stack([np.where(same, np.cos(ang), 0.0), np.where(same, np.sin(ang), 0.0)]).astype(np.float32)
    vone = np.zeros((MLA_HEADS, 1, HEAD_PAD), np.float32)
    vone[0::2, 0, MLA_V] = 1.0
    vone[1::2, 0, 0] = 1.0
    return jnp.asarray(dft64, BF16), jnp.asarray(vone)


def _seq_tables(S):
    j = lax.broadcasted_iota(jnp.int32, (S, S), 0)
    k = lax.broadcasted_iota(jnp.int32, (S, S), 1)
    ang = ((j * k) % S).astype(F32) * (2.0 * math.pi / S)
    a_dft = jnp.concatenate([jnp.cos(ang), -jnp.sin(ang)], axis=1).astype(BF16)
    half = MLA_ROPE // 2
    freqs = ROPE_THETA ** (-jnp.arange(half, dtype=F32) / half)
    ra = jnp.arange(S, dtype=F32)[:, None] * freqs[None, :]
    cos, sin = jnp.cos(ra), jnp.sin(ra)
    one = lambda n: jnp.ones((S, n), F32)
    zero = lambda n: jnp.zeros((S, n), F32)
    tail = HEAD_PAD - MLA_QK
    rope = jnp.stack([
        jnp.concatenate([one(MLA_NOPE), cos, cos, one(tail)], axis=1),
        jnp.concatenate([zero(MLA_NOPE), -sin, sin, zero(tail)], axis=1),
    ])
    return a_dft, rope


def _dispatch_plan(route2d, counts):
    T = route2d.shape[0]
    R = T * TOP_K
    cnt = counts[0, :N_EXPERTS].astype(jnp.int32)
    padded = (cnt + MOE_ROWS - 1) // MOE_ROWS * MOE_ROWS
    pend = jnp.cumsum(padded)
    pstart = pend - padded
    start = jnp.zeros((1, ROUTE_LANES), F32).at[0, :N_EXPERTS].set(pstart.astype(F32))
    dest = route_rank(route2d, start)[:, 0:TOP_K]
    P = R + N_EXPERTS * MOE_ROWS
    n_blk = P // MOE_ROWS
    blk_first = jnp.arange(n_blk, dtype=jnp.int32) * MOE_ROWS
    blk_e = jnp.minimum(jnp.sum((pend[None, :] <= blk_first[:, None]).astype(jnp.int32), axis=1), N_EXPERTS - 1)
    n_used = (pend[-1] // MOE_ROWS).astype(jnp.int32).reshape(1)
    return dest, blk_e, n_used, P


def _trunk(x, mod_all, f, consts, tables):
    B, S, D = x.shape
    T = B * S
    dft64, vone = consts
    a_dft, rope = tables
    tm = min(S, 512)
    tq = min(S, 512)
    tk = min(S, 512)
    dft_scale = 1.0 / math.sqrt(FNET_GROUP_DIM * S)
    for l in range(DEPTH):
        mod = mod_all[l].reshape(B, 6, D)
        qkv, og, gate, xcs, cq, ckv, kr = in_proj(
            x, mod, f["norm1_g"][l], f["w_in"][l], dft64, f["gate_up"][l], f["gate_bias"][l], f["lora_g"][l], tm)
        o_gla = gla(qkv, og, gate, f["gla_norm_g"][l])
        o_fft = dft_seq(a_dft, xcs.reshape(2 * S, B * FNET_W), dft_scale)
        o_mla = mla(f["mla_flag"][l], cq, ckv, kr, rope, f["wq"][l], f["wk"][l], f["wv"][l], f["qg"][l], f["kg"][l],
                    vone, f["mla_shift"][l], tq, tk)
        x, h, route, counts = out_proj(x, o_gla, o_fft, o_mla, mod, f["w_out"][l], f["norm2_g"][l], f["rw"][l],
                                       f["rb"][l], tm)
        dest, blk_e, n_used, n_rows = _dispatch_plan(route.reshape(T, ROUTE_LANES), counts)
        xb = dispatch_rows(h.reshape(T, D // 2), dest, n_rows)
        yb = moe_mlp(xb, blk_e, n_used, f["w1"][l], f["w3"][l], f["w2"][l])
        x = combine_sc(x, yb, dest, route, mod)
    return x


def kernel(x_prompt, x_sample, c_prompt, c_sample, ada_w, ada_b, norm1_g, norm2_g, w_in, w_out, gla_gate_up_f, gla_gate_bias_f, gla_gate_up_b, gla_gate_bias_b, gla_out_norm_g, mla_q_lora_norm_g, mla_w_uq, mla_kv_lora_norm_g, mla_w_ukv, mla_q_norm_g, mla_k_norm_g, router_group_w, router_group_b, router_expert_w, router_expert_b, expert_w1, expert_w3, expert_w2):
    p = dict(norm1_g=norm1_g, norm2_g=norm2_g, w_in=w_in, w_out=w_out, gla_gate_up_f=gla_gate_up_f,
             gla_gate_bias_f=gla_gate_bias_f, gla_gate_up_b=gla_gate_up_b, gla_gate_bias_b=gla_gate_bias_b,
             gla_out_norm_g=gla_out_norm_g, mla_q_lora_norm_g=mla_q_lora_norm_g, mla_w_uq=mla_w_uq,
             mla_kv_lora_norm_g=mla_kv_lora_norm_g, mla_w_ukv=mla_w_ukv, mla_q_norm_g=mla_q_norm_g,
             mla_k_norm_g=mla_k_norm_g, router_group_w=router_group_w, router_group_b=router_group_b,
             router_expert_w=router_expert_w, router_expert_b=router_expert_b, expert_w1=expert_w1,
             expert_w3=expert_w3, expert_w2=expert_w2)
    f = _prep_params(p)
    consts = _const_tables()
    nb_p = c_prompt.shape[0]
    mod_all = ada_modulation(jnp.concatenate([c_prompt, c_sample], axis=0), ada_w, ada_b)
    y_prompt = _trunk(x_prompt, mod_all[:, :nb_p], f, consts, _seq_tables(x_prompt.shape[1]))
    y_sample = _trunk(x_sample, mod_all[:, nb_p:], f, consts, _seq_tables(x_sample.shape[1]))
    return (y_prompt, y_sample)
```

```python
import functools
import math

import jax
import jax.numpy as jnp
import numpy as np
from jax import lax
from jax.experimental import pallas as pl
from jax.experimental.pallas import tpu as pltpu
from jax.experimental.pallas import tpu_sc as plsc

F32 = jnp.float32
BF16 = jnp.bfloat16

D_MODEL = 1024
DEPTH = 4
EPS = 1e-6

GLA_HEADS = 4
GLA_DK = 32
GLA_DV = 64
GLA_QK_W = GLA_HEADS * GLA_DK
GLA_W = GLA_HEADS * GLA_DV
GLA_GATE_RANK = 16
GLA_GATE_NORMALIZER = 16.0
GLA_CHUNK = 32
GLA_BLOCK = 128

FNET_GROUPS = 4
FNET_GROUP_DIM = 64
FNET_W = 256

MLA_HEADS = 8
MLA_Q_LORA = 256
MLA_KV_LORA = 128
MLA_NOPE = 64
MLA_ROPE = 32
MLA_V = 64
MLA_QK = 96
MLA_W = 512
ROPE_THETA = 10000.0
HEAD_PAD = 128

N_GROUPS = 4
EXPERTS_PER_GROUP = 8
N_EXPERTS = 32
TOP_K = 2
D_EXPERT = 512
ROUTE_LANES = 128

C_Q, C_K, C_V, C_OG, C_F, C_CQ, C_CKV, C_KR, C_GATE = 0, 128, 256, 512, 768, 1024, 1280, 1408, 1536
P_IN_PAD = 1664

VMEM_LIMIT_BYTES = 56 * 1024 * 1024


def _cparams(sem, vmem=None):
    return pltpu.CompilerParams(dimension_semantics=sem, vmem_limit_bytes=vmem or VMEM_LIMIT_BYTES)


def _silu(x):
    return x * (1.0 / (1.0 + jnp.exp(-x)))


def _log_sigmoid(x):
    return -(jnp.maximum(-x, 0.0) + jnp.log1p(jnp.exp(-jnp.abs(x))))


def _dot(a, b):
    return jnp.dot(a, b, preferred_element_type=F32)


def _dot_nt(a, b):
    return lax.dot_general(a, b, (((1,), (1,)), ((), ())), preferred_element_type=F32)


def _split_bf16(x):
    hi = x.astype(BF16)
    lo = (x - hi.astype(F32)).astype(BF16)
    return hi, lo


def _pack_halves(x):
    n = x.shape[-1] // 2
    lo = lax.shift_right_logical(lax.bitcast_convert_type(x[:, :n], jnp.int32), 16)
    hi = lax.bitcast_convert_type(x[:, n:], jnp.int32) & jnp.int32(-65536)
    return hi | lo


def _unpack_halves(w):
    lo = lax.bitcast_convert_type(lax.shift_left(w, 16), F32)
    hi = lax.bitcast_convert_type(w & jnp.int32(-65536), F32)
    return lo, hi


def _ada_kernel(c_ref, w_ref, b_ref, o_ref):
    c = _silu(c_ref[...]).astype(BF16)
    o_ref[0] = _dot(c, w_ref[0].astype(BF16)) + b_ref[0]


def ada_modulation(c_all, ada_w, ada_b):
    nb = c_all.shape[0]
    tn = 1536
    n = ada_w.shape[-1]
    return pl.pallas_call(
        _ada_kernel,
        grid=(DEPTH, n // tn),
        in_specs=[
            pl.BlockSpec((nb, D_MODEL), lambda l, j: (0, 0)),
            pl.BlockSpec((1, D_MODEL, tn), lambda l, j: (l, 0, j)),
            pl.BlockSpec((1, 1, tn), lambda l, j: (l, 0, j)),
        ],
        out_specs=pl.BlockSpec((1, nb, tn), lambda l, j: (l, 0, j)),
        out_shape=jax.ShapeDtypeStruct((DEPTH, nb, n), F32),
        compiler_params=_cparams(("parallel", "parallel")),
        name="ada_modulation",
    )(c_all, ada_w, ada_b.reshape(DEPTH, 1, n))


def _in_proj_body(x, mod_ref, g_ref, w_ref, dft_ref, up_ref, gb_ref, lg_ref,
                  qkv_ref, og_ref, gate_ref, xcs_ref, cq_ref, ckv_ref, kr_ref):
    sh = mod_ref[0, 0:1, :]
    sc = mod_ref[0, 1:2, :]
    r = lax.rsqrt(jnp.mean(x * x, axis=-1, keepdims=True) + EPS)
    h = (x * r * g_ref[...]) * (1.0 + sc) + sh
    u = _dot(h.astype(BF16), w_ref[...])

    q = u[:, C_Q:C_K] * (GLA_DK ** -0.5)
    qkv_ref[:, 0:128] = q.astype(BF16)
    qkv_ref[:, 128:512] = u[:, C_K:C_OG].astype(BF16)
    og_ref[...] = u[:, C_OG:C_F].astype(BF16)

    ug = u[:, C_GATE:C_GATE + 128].astype(BF16)
    gl = _dot(ug, up_ref[...]) + gb_ref[...]
    gate_ref[...] = _log_sigmoid(gl) * (1.0 / GLA_GATE_NORMALIZER)

    uf = u[:, C_F:C_CQ].astype(BF16)
    xcs_ref[0] = _dot(uf, dft_ref[0]).astype(BF16)
    xcs_ref[1] = _dot(uf, dft_ref[1]).astype(BF16)

    cq = u[:, C_CQ:C_CKV]
    rq = lax.rsqrt(jnp.mean(cq * cq, axis=-1, keepdims=True) + EPS)
    cq_ref[...] = (cq * rq * lg_ref[:, 0:256]).astype(BF16)
    ckv = u[:, C_CKV:C_KR]
    rkv = lax.rsqrt(jnp.mean(ckv * ckv, axis=-1, keepdims=True) + EPS)
    ckv_ref[...] = (ckv * rkv * lg_ref[:, 256:384]).astype(BF16)
    kr_ref[...] = u[:, C_KR:C_GATE].astype(BF16)


def _in_proj_kernel(x_ref, mod_ref, g_ref, w_ref, dft_ref, up_ref, gb_ref, lg_ref,
                    qkv_ref, og_ref, gate_ref, xcs_ref, cq_ref, ckv_ref, kr_ref):
    _in_proj_body(x_ref[0], mod_ref, g_ref, w_ref, dft_ref, up_ref, gb_ref, lg_ref,
                  qkv_ref.at[0], og_ref.at[0], gate_ref.at[0], xcs_ref, cq_ref.at[0], ckv_ref.at[0], kr_ref.at[0])


def in_proj(x, mod, norm_g, w_in, dft64, gate_up, gate_bias, lora_g, tm):
    B, S, D = x.shape
    ns = S // tm
    row = lambda b, s: (b, s, 0)
    const2 = lambda b, s: (0, 0)
    out_shapes = (
        jax.ShapeDtypeStruct((B, S, 512), BF16),
        jax.ShapeDtypeStruct((B, S, GLA_W), BF16),
        jax.ShapeDtypeStruct((B, S, 256), F32),
        jax.ShapeDtypeStruct((2, S, B * FNET_W), BF16),
        jax.ShapeDtypeStruct((B, S, MLA_Q_LORA), BF16),
        jax.ShapeDtypeStruct((B, S, MLA_KV_LORA), BF16),
        jax.ShapeDtypeStruct((B, S, HEAD_PAD), BF16),
    )
    return pl.pallas_call(
        _in_proj_kernel,
        grid=(B, ns),
        in_specs=[
            pl.BlockSpec((1, tm, D), row),
            pl.BlockSpec((1, 6, D), lambda b, s: (b, 0, 0)),
            pl.BlockSpec((1, D), const2),
            pl.BlockSpec((D, P_IN_PAD), const2),
            pl.BlockSpec((2, FNET_W, FNET_W), lambda b, s: (0, 0, 0)),
            pl.BlockSpec((128, 256), const2),
            pl.BlockSpec((1, 256), const2),
            pl.BlockSpec((1, 384), const2),
        ],
        out_specs=(
            pl.BlockSpec((1, tm, 512), row),
            pl.BlockSpec((1, tm, GLA_W), row),
            pl.BlockSpec((1, tm, 256), row),
            pl.BlockSpec((2, tm, FNET_W), lambda b, s: (0, s, b)),
            pl.BlockSpec((1, tm, MLA_Q_LORA), row),
            pl.BlockSpec((1, tm, MLA_KV_LORA), row),
            pl.BlockSpec((1, tm, HEAD_PAD), row),
        ),
        out_shape=out_shapes,
        compiler_params=_cparams(("parallel", "parallel")),
        name="in_proj",
    )(x, mod, norm_g, w_in, dft64, gate_up, gate_bias, lora_g)


def _gla_kernel(qkv_ref, og_ref, gate_ref, ng_ref, o_ref, o_s, st_s, *, seq):
    R, C = GLA_BLOCK, GLA_CHUNK
    n_sub = R // C
    nblk = seq // R
    shift_c = int(math.log2(C))

    row = lax.broadcasted_iota(jnp.int32, (R, R), 0)
    col = lax.broadcasted_iota(jnp.int32, (R, R), 1)
    same = (row >> shift_c) == (col >> shift_c)
    ones_blk = jnp.where(same, 1.0, 0.0)
    lane_qk = lax.broadcasted_iota(jnp.int32, (1, GLA_QK_W), 1)
    lane_v = lax.broadcasted_iota(jnp.int32, (1, GLA_W), 1)
    head_qk = [(lane_qk >> 5) == h for h in range(GLA_HEADS)]
    head_v = [(lane_v >> 6) == h for h in range(GLA_HEADS)]
    st_row = lax.broadcasted_iota(jnp.int32, (GLA_W, GLA_QK_W), 0)
    st_col = lax.broadcasted_iota(jnp.int32, (GLA_W, GLA_QK_W), 1)
    st_mask = (st_row >> 6) == (st_col >> 5)
    sub_row = lax.broadcasted_iota(jnp.int32, (R, 1), 0) >> shift_c
    nrow = lax.broadcasted_iota(jnp.int32, (GLA_W, GLA_W), 0)
    ncol = lax.broadcasted_iota(jnp.int32, (GLA_W, GLA_W), 1)
    head_mean = jnp.where((nrow >> 6) == (ncol >> 6), 1.0 / GLA_DV, 0.0).astype(BF16)

    def direction(backward):
        if backward:
            tri = same & (col >= row)
            att_ok = same & (col > row)
            order = list(range(n_sub - 1, -1, -1))
        else:
            tri = same & (col <= row)
            att_ok = same & (col <= row)
            order = list(range(n_sub))
        cum_lhs = jnp.concatenate([jnp.where(tri, 1.0, 0.0), ones_blk], axis=0).astype(BF16)
        att_ok4 = jnp.concatenate([att_ok] * GLA_HEADS, axis=1)
        return cum_lhs, att_ok4, order, (GLA_QK_W if backward else 0), (1 if backward else 0)

    def block(j, consts):
        cum_lhs, att_ok4, order, gate_off, slot = consts
        r0 = pl.multiple_of(j * R, R)
        qkv = qkv_ref[0, pl.ds(r0, R), :]
        q = qkv[:, 0:128].astype(F32)
        k = qkv[:, 128:256].astype(F32)
        v = qkv[:, 256:512]
        g = gate_ref[0, pl.ds(r0, R), gate_off:gate_off + GLA_QK_W]
        g_hi, g_lo = _split_bf16(g)
        cs = _dot(cum_lhs, jnp.concatenate([g_hi, g_lo], axis=1))
        b = cs[0:R, 0:128] + cs[0:R, 128:256]
        bl = cs[R:2 * R, 0:128] + cs[R:2 * R, 128:256]
        q_dec = q * jnp.exp(b)
        k_inv = (k * jnp.exp(-b)).astype(BF16)
        k_end = k * jnp.exp(bl - b)
        decay = jnp.exp(bl)

        att = jnp.concatenate(
            [_dot_nt(jnp.where(head_qk[h], q_dec, 0.0).astype(BF16), k_inv) for h in range(GLA_HEADS)],
            axis=1)
        att = jnp.where(att_ok4, att, 0.0).astype(BF16)
        v_heads = jnp.concatenate([jnp.where(head_v[h], v, jnp.zeros_like(v)) for h in range(GLA_HEADS)], axis=0)
        o_blk = _dot(att, v_heads)

        v_t = v.astype(F32).T.astype(BF16)
        q_dec_b = q_dec.astype(BF16)
        inter = [None] * n_sub
        for c in order:
            st = st_s[slot]
            inter[c] = _dot_nt(q_dec_b[c * C:(c + 1) * C, :], st.astype(BF16))
            k_c = jnp.where(sub_row == c, k_end, 0.0).astype(BF16)
            d_st = _dot(v_t, k_c)
            st_s[slot] = st * decay[c * C:c * C + 1, :] + jnp.where(st_mask, d_st, 0.0)
        o_s[slot, pl.ds(r0, R), :] = o_blk + jnp.concatenate(inter, axis=0)

    fwd = direction(False)
    bwd = direction(True)
    st_s[...] = jnp.zeros_like(st_s)

    def scan_step(i, carry):
        block(i, fwd)
        block(nblk - 1 - i, bwd)
        return carry

    lax.fori_loop(0, nblk, scan_step, 0)

    def finish(j, carry):
        r0 = pl.multiple_of(j * R, R)
        o = o_s[0, pl.ds(r0, R), :] + o_s[1, pl.ds(r0, R), :]
        sq_hi, sq_lo = _split_bf16(o * o)
        ms = _dot(sq_hi, head_mean) + _dot(sq_lo, head_mean)
        y = o * lax.rsqrt(ms + EPS) * ng_ref[...]
        og = og_ref[0, pl.ds(r0, R), :].astype(F32)
        o_ref[0, pl.ds(r0, R), :] = (y * _silu(og)).astype(o_ref.dtype)
        return carry

    lax.fori_loop(0, nblk, finish, 0)


def gla(qkv, og, gate, norm_g4):
    B, S, _ = qkv.shape
    full = lambda b: (b, 0, 0)
    return pl.pallas_call(
        functools.partial(_gla_kernel, seq=S),
        grid=(B,),
        in_specs=[
            pl.BlockSpec((1, S, 512), full),
            pl.BlockSpec((1, S, GLA_W), full),
            pl.BlockSpec((1, S, 256), full),
            pl.BlockSpec((1, GLA_W), lambda b: (0, 0)),
        ],
        out_specs=pl.BlockSpec((1, S, GLA_W), full),
        out_shape=jax.ShapeDtypeStruct((B, S, GLA_W), BF16),
        scratch_shapes=[pltpu.VMEM((2, S, GLA_W), F32), pltpu.VMEM((2, GLA_W, GLA_QK_W), F32)],
        compiler_params=_cparams(("parallel",)),
        name="gla",
    )(qkv, og, gate, norm_g4)


def _dft_kernel(a_ref, b_ref, o_ref, acc_ref, *, scale):
    kk = pl.program_id(2)

    @pl.when(kk == 0)
    def _():
        acc_ref[...] = jnp.zeros_like(acc_ref)

    acc_ref[...] += _dot(a_ref[...], b_ref[...])

    @pl.when(kk == pl.num_programs(2) - 1)
    def _():
        o_ref[...] = (acc_ref[...] * scale).astype(o_ref.dtype)


def dft_seq(a_dft, xcs, scale):
    m, k = a_dft.shape
    n = xcs.shape[1]
    tm, tn, tk = min(m, 1024), min(n, 1024), min(k, 2048)
    return pl.pallas_call(
        functools.partial(_dft_kernel, scale=scale),
        grid=(m // tm, n // tn, k // tk),
        in_specs=[pl.BlockSpec((tm, tk), lambda i, j, kk: (i, kk)),
                  pl.BlockSpec((tk, tn), lambda i, j, kk: (kk, j))],
        out_specs=pl.BlockSpec((tm, tn), lambda i, j, kk: (i, j)),
        out_shape=jax.ShapeDtypeStruct((m, n), BF16),
        scratch_shapes=[pltpu.VMEM((tm, tn), F32)],
        compiler_params=_cparams(("parallel", "parallel", "arbitrary")),
        name="dft_seq",
    )(a_dft, xcs)


def _rope(x, cos, sin, swap):
    return x * cos + _dot(x.astype(BF16), swap) * sin


SHIFT_LANE = HEAD_PAD - 1
MAX_STATIC_SHIFT = 40.0
LOG2E = math.log2(math.e)


def _mla_kernel(flag_ref, cq_ref, ckv_ref, kr_ref, rope_ref, wq_ref, wk_ref, wv_ref, qg_ref, kg_ref, vone_ref,
                shift_ref, o_ref, kt_s, v_s, q_s, acc_s, *, seq, tq, tk):
    qi = pl.program_id(1)
    inv_qk = 1.0 / MLA_QK
    lane = lax.broadcasted_iota(jnp.int32, (1, HEAD_PAD), 1)
    src = lax.broadcasted_iota(jnp.int32, (HEAD_PAD, HEAD_PAD), 0)
    dst = lax.broadcasted_iota(jnp.int32, (HEAD_PAD, HEAD_PAD), 1)
    half = MLA_ROPE // 2
    lo_half = (dst >= MLA_NOPE) & (dst < MLA_NOPE + half)
    hi_half = (dst >= MLA_NOPE + half) & (dst < MLA_QK)
    swap = jnp.where((lo_half & (src == dst + half)) | (hi_half & (src == dst - half)), 1.0, 0.0).astype(BF16)

    @pl.when(qi == 0)
    def _():
        one_lane = jnp.where(lane == SHIFT_LANE, 1.0, 0.0)

        def rows(i, carry):
            r0 = pl.multiple_of(i * tk, tk)
            ckv = ckv_ref[0, pl.ds(r0, tk), :]
            kr = kr_ref[0, pl.ds(r0, tk), :].astype(F32)
            cos = rope_ref[0, pl.ds(r0, tk), :]
            sin = rope_ref[1, pl.ds(r0, tk), :]
            for h in range(MLA_HEADS):
                kp = _dot(ckv, wk_ref[h]) + kr
                r = lax.rsqrt(jnp.sum(kp * kp, axis=-1, keepdims=True) * inv_qk + EPS)
                kn = _rope(kp * r * kg_ref[...], cos, sin, swap) + one_lane
                kt_s[h, i] = kn.T.astype(BF16)
                v_s[h, pl.ds(r0, tk), :] = (_dot(ckv, wv_ref[h]) + vone_ref[h]).astype(BF16)
            return carry

        lax.fori_loop(0, seq // tk, rows, 0)

    q0 = pl.multiple_of(qi * tq, tq)
    cos = rope_ref[0, pl.ds(q0, tq), :]
    sin = rope_ref[1, pl.ds(q0, tq), :]
    for h in range(MLA_HEADS):
        qp = _dot(cq_ref[0], wq_ref[h])
        r = lax.rsqrt(jnp.sum(qp * qp, axis=-1, keepdims=True) * inv_qk + EPS)
        q_s[h] = (_rope(qp * r * qg_ref[...], cos, sin, swap) + shift_ref[...]).astype(BF16)

    def finish():
        for pair in range(MLA_HEADS // 2):
            outs = []
            for h in (2 * pair, 2 * pair + 1):
                acc = acc_s[h]
                den_lane = MLA_V if h % 2 == 0 else 0
                den = jnp.sum(jnp.where(lane == den_lane, acc, 0.0), axis=-1, keepdims=True)
                outs.append(acc * (1.0 / den))
            both = jnp.where(lane < MLA_V, outs[0], outs[1])
            o_ref[0, :, pair * HEAD_PAD:(pair + 1) * HEAD_PAD] = both.astype(o_ref.dtype)

    @pl.when(flag_ref[0] == 1)
    def _():
        acc_s[...] = jnp.zeros_like(acc_s)

        def kv_step(j, carry):
            k0 = pl.multiple_of(j * tk, tk)
            for h in range(MLA_HEADS):
                p = jnp.exp2(_dot(q_s[h], kt_s[h, j])).astype(BF16)
                acc_s[h] += _dot(p, v_s[h, pl.ds(k0, tk), :])
            return carry

        lax.fori_loop(0, seq // tk, kv_step, 0)
        finish()

    @pl.when(flag_ref[0] == 0)
    def _():
        for h in range(MLA_HEADS):
            def kv_step(j, carry, h=h):
                m, acc = carry
                k0 = pl.multiple_of(j * tk, tk)
                s = _dot(q_s[h], kt_s[h, j])
                m_new = jnp.maximum(m, jnp.max(s, axis=-1, keepdims=True))
                p = jnp.exp2(s - m_new).astype(BF16)
                return m_new, jnp.exp2(m - m_new) * acc + _dot(p, v_s[h, pl.ds(k0, tk), :])

            m0 = jnp.full((tq, 1), -jnp.inf, F32)
            _, acc = lax.fori_loop(0, seq // tk, kv_step, (m0, jnp.zeros((tq, HEAD_PAD), F32)))
            acc_s[h] = acc
        finish()


def mla(flag, cq, ckv, kr, rope, wq, wk, wv, qg, kg, vone, shift, tq, tk):
    B, S, _ = cq.shape
    c3 = lambda b, q: (0, 0, 0)
    c2 = lambda b, q: (0, 0)
    return pl.pallas_call(
        functools.partial(_mla_kernel, seq=S, tq=tq, tk=tk),
        grid=(B, S // tq),
        in_specs=[
            pl.BlockSpec(memory_space=pltpu.SMEM),
            pl.BlockSpec((1, tq, MLA_Q_LORA), lambda b, q: (b, q, 0)),
            pl.BlockSpec((1, S, MLA_KV_LORA), lambda b, q: (b, 0, 0)),
            pl.BlockSpec((1, S, HEAD_PAD), lambda b, q: (b, 0, 0)),
            pl.BlockSpec((2, S, HEAD_PAD), c3),
            pl.BlockSpec((MLA_HEADS, MLA_Q_LORA, HEAD_PAD), c3),
            pl.BlockSpec((MLA_HEADS, MLA_KV_LORA, HEAD_PAD), c3),
            pl.BlockSpec((MLA_HEADS, MLA_KV_LORA, HEAD_PAD), c3),
            pl.BlockSpec((1, HEAD_PAD), c2),
            pl.BlockSpec((1, HEAD_PAD), c2),
            pl.BlockSpec((MLA_HEADS, 1, HEAD_PAD), c3),
            pl.BlockSpec((1, HEAD_PAD), c2),
        ],
        out_specs=pl.BlockSpec((1, tq, MLA_W), lambda b, q: (b, q, 0)),
        out_shape=jax.ShapeDtypeStruct((B, S, MLA_W), BF16),
        scratch_shapes=[pltpu.VMEM((MLA_HEADS, S // tk, HEAD_PAD, tk), BF16),
                        pltpu.VMEM((MLA_HEADS, S, HEAD_PAD), BF16),
                        pltpu.VMEM((MLA_HEADS, tq, HEAD_PAD), BF16),
                        pltpu.VMEM((MLA_HEADS, tq, HEAD_PAD), F32)],
        compiler_params=_cparams(("parallel", "arbitrary")),
        name="mla",
    )(flag, cq, ckv, kr, rope, wq, wk, wv, qg, kg, vone, shift)


def _out_proj_kernel(x_ref, gla_ref, fft_ref, mla_ref, mod_ref, w_ref, g_ref, rw_ref, rb_ref,
                     xo_ref, h_ref, route_ref, cnt_ref):
    mix = (_dot(gla_ref[0], w_ref[0:256, :]) + _dot(fft_ref[...], w_ref[256:512, :])
           + _dot(mla_ref[0], w_ref[512:1024, :]))
    g1 = mod_ref[0, 2:3, :]
    sh = mod_ref[0, 3:4, :]
    sc = mod_ref[0, 4:5, :]
    x = x_ref[0] + g1 * mix
    xo_ref[0] = x
    r = lax.rsqrt(jnp.mean(x * x, axis=-1, keepdims=True) + EPS)
    h = (x * r * g_ref[...]) * (1.0 + sc) + sh
    h_hi, h_lo = _split_bf16(h)
    h_ref[0] = _pack_halves(h_hi.astype(F32))

    logit = _dot(h_hi, rw_ref[0]) + _dot(h_lo, rw_ref[0]) + _dot(h_hi, rw_ref[1]) + rb_ref[...]
    lane = lax.broadcasted_iota(jnp.int32, (1, ROUTE_LANES), 1)
    lane_f = lane.astype(F32)
    neg = -1e30
    is_g = lane < N_GROUPS
    is_e = (lane >= N_GROUPS) & (lane < N_GROUPS + N_EXPERTS)
    lg = jnp.where(is_g, logit, neg)
    g_max = jnp.max(lg, axis=-1, keepdims=True)
    g_den = jnp.sum(jnp.where(is_g, jnp.exp(lg - g_max), 0.0), axis=-1, keepdims=True)
    g_w = 1.0 / g_den
    g_top = jnp.min(jnp.where(is_g & (logit == g_max), lane_f, 1e9), axis=-1, keepdims=True)
    e_grp = ((lane - N_GROUPS) >> 3).astype(F32)
    in_grp = is_e & (e_grp == g_top)
    le = jnp.where(in_grp, logit, neg)
    t1 = jnp.max(le, axis=-1, keepdims=True)
    i1 = jnp.min(jnp.where(in_grp & (le == t1), lane_f, 1e9), axis=-1, keepdims=True)
    le2 = jnp.where(lane_f == i1, neg, le)
    t2 = jnp.max(le2, axis=-1, keepdims=True)
    i2 = jnp.min(jnp.where(in_grp & (le2 == t2), lane_f, 1e9), axis=-1, keepdims=True)
    e21 = jnp.exp(t2 - t1)
    w1 = g_w / (1.0 + e21)
    w2 = w1 * e21
    route = jnp.where(lane == 0, i1 - N_GROUPS,
                      jnp.where(lane == 1, i2 - N_GROUPS,
                                jnp.where(lane == 2, w1, jnp.where(lane == 3, w2, 0.0))))
    route_ref[0] = route

    @pl.when((pl.program_id(0) == 0) & (pl.program_id(1) == 0))
    def _():
        cnt_ref[...] = jnp.zeros_like(cnt_ref)

    picked = jnp.where((lane_f == i1 - N_GROUPS) | (lane_f == i2 - N_GROUPS), 1.0, 0.0)
    cnt_ref[...] += jnp.sum(picked, axis=0, keepdims=True)


def out_proj(x, o_gla, o_fft, o_mla, mod, w_out, norm_g, rw, rb, tm):
    B, S, D = x.shape
    row = lambda b, s: (b, s, 0)
    c2 = lambda b, s: (0, 0)
    return pl.pallas_call(
        _out_proj_kernel,
        grid=(B, S // tm),
        in_specs=[
            pl.BlockSpec((1, tm, D), row),
            pl.BlockSpec((1, tm, GLA_W), row),
            pl.BlockSpec((tm, FNET_W), lambda b, s: (s, b)),
            pl.BlockSpec((1, tm, MLA_W), row),
            pl.BlockSpec((1, 6, D), lambda b, s: (b, 0, 0)),
            pl.BlockSpec((D, D), c2),
            pl.BlockSpec((1, D), c2),
            pl.BlockSpec((2, D, ROUTE_LANES), lambda b, s: (0, 0, 0)),
            pl.BlockSpec((1, ROUTE_LANES), c2),
        ],
        out_specs=(pl.BlockSpec((1, tm, D), row), pl.BlockSpec((1, tm, D // 2), row),
                   pl.BlockSpec((1, tm, ROUTE_LANES), row), pl.BlockSpec((1, ROUTE_LANES), c2)),
        out_shape=(jax.ShapeDtypeStruct((B, S, D), F32), jax.ShapeDtypeStruct((B, S, D // 2), jnp.int32),
                   jax.ShapeDtypeStruct((B, S, ROUTE_LANES), F32), jax.ShapeDtypeStruct((1, ROUTE_LANES), F32)),
        compiler_params=_cparams(("arbitrary", "arbitrary")),
        name="out_proj",
    )(x, o_gla, o_fft, o_mla, mod, w_out, norm_g, rw, rb)


RANK_ROWS = 512


def _rank_kernel(route_ref, start_ref, dest_ref, carry_ref, before_ref):
    n = route_ref.shape[0]

    @pl.when(pl.program_id(0) == 0)
    def _():
        carry_ref[...] = jnp.zeros_like(carry_ref)
        row = lax.broadcasted_iota(jnp.int32, (n, n), 0)
        col = lax.broadcasted_iota(jnp.int32, (n, n), 1)
        before_ref[...] = jnp.where(col < row, 1.0, 0.0).astype(BF16)

    lane_f = lax.broadcasted_iota(jnp.int32, (1, ROUTE_LANES), 1).astype(F32)
    e1 = route_ref[:, 0:1]
    e2 = route_ref[:, 1:2]
    oh1 = jnp.where(lane_f == e1, 1.0, 0.0)
    oh2 = jnp.where(lane_f == e2, 1.0, 0.0)
    both = (oh1 + oh2).astype(BF16)
    pos = _dot(before_ref[...], both) + carry_ref[...] + start_ref[...]
    d1 = jnp.sum(oh1 * pos, axis=-1, keepdims=True)
    d2 = jnp.sum(oh2 * pos, axis=-1, keepdims=True)
    lane = lax.broadcasted_iota(jnp.int32, (1, ROUTE_LANES), 1)
    dest_ref[...] = jnp.where(lane == 0, d1, jnp.where(lane == 1, d2, 0.0)).astype(jnp.int32)
    carry_ref[...] += jnp.sum(oh1 + oh2, axis=0, keepdims=True)


def route_rank(route2d, start):
    T = route2d.shape[0]
    rr = min(T, RANK_ROWS)
    return pl.pallas_call(
        _rank_kernel,
        grid=(T // rr,),
        in_specs=[pl.BlockSpec((rr, ROUTE_LANES), lambda i: (i, 0)),
                  pl.BlockSpec((1, ROUTE_LANES), lambda i: (0, 0))],
        out_specs=pl.BlockSpec((rr, ROUTE_LANES), lambda i: (i, 0)),
        out_shape=jax.ShapeDtypeStruct((T, ROUTE_LANES), jnp.int32),
        scratch_shapes=[pltpu.VMEM((1, ROUTE_LANES), F32), pltpu.VMEM((rr, rr), BF16)],
        compiler_params=_cparams(("arbitrary",)),
        name="route_rank",
    )(route2d, start)


MOE_ROWS = 512


def _moe_kernel(blk_e_ref, n_used_ref, x_ref, w1_ref, w3_ref, w2_ref, o_ref):
    i = pl.program_id(0)

    @pl.when(i < n_used_ref[0])
    def _():
        half = D_MODEL // 2
        x_lo, x_hi = _unpack_halves(x_ref[...])
        x_lo = x_lo.astype(BF16)
        x_hi = x_hi.astype(BF16)
        a = _dot(x_lo, w1_ref[0, 0:half, :]) + _dot(x_hi, w1_ref[0, half:, :])
        b = _dot(x_lo, w3_ref[0, 0:half, :]) + _dot(x_hi, w3_ref[0, half:, :])
        hm = (_silu(a) * b).astype(BF16)
        y = _dot(hm, w2_ref[0])
        o_ref[...] = _pack_halves(y.astype(BF16).astype(F32))

    @pl.when(i >= n_used_ref[0])
    def _():
        o_ref[...] = jnp.zeros_like(o_ref)


def moe_mlp(xb, blk_e, n_used, w1, w3, w2):
    P, half = xb.shape
    n_blk = P // MOE_ROWS
    wmap = lambda i, be, nu: (be[i], 0, 0)
    grid_spec = pltpu.PrefetchScalarGridSpec(
        num_scalar_prefetch=2,
        grid=(n_blk,),
        in_specs=[
            pl.BlockSpec((MOE_ROWS, half), lambda i, be, nu: (i, 0)),
            pl.BlockSpec((1, D_MODEL, D_EXPERT), wmap),
            pl.BlockSpec((1, D_MODEL, D_EXPERT), wmap),
            pl.BlockSpec((1, D_EXPERT, D_MODEL), wmap),
        ],
        out_specs=pl.BlockSpec((MOE_ROWS, half), lambda i, be, nu: (i, 0)),
    )
    return pl.pallas_call(
        _moe_kernel,
        grid_spec=grid_spec,
        out_shape=jax.ShapeDtypeStruct((P, half), jnp.int32),
        compiler_params=_cparams(("arbitrary",)),
        name="moe_mlp",
    )(blk_e, n_used, xb, w1, w3, w2)


ROW_TILE = 512


def _dispatch_kernel(dest_ref, h_ref, init_ref, xb_ref, stage, sem):
    del init_ref
    n = h_ref.shape[0]
    stage[...] = h_ref[...]
    for r in range(n):
        for c in range(TOP_K):
            d = dest_ref[0, 0, TOP_K * r + c]
            pltpu.make_async_copy(stage.at[pl.ds(r, 1)], xb_ref.at[pl.ds(d, 1)], sem).start()
    for c in range(TOP_K):
        pltpu.make_async_copy(stage, xb_ref.at[pl.ds(0, n)], sem).wait()


def dispatch_rows(hp, dest, n_rows):
    T, half = hp.shape
    rt = min(T, ROW_TILE)
    dest2 = dest.reshape(T // rt, 1, TOP_K * rt)
    init = jnp.zeros((n_rows, half), jnp.int32)
    return pl.pallas_call(
        _dispatch_kernel,
        grid=(T // rt,),
        in_specs=[
            pl.BlockSpec((1, 1, TOP_K * rt), lambda i: (i, 0, 0), memory_space=pltpu.SMEM),
            pl.BlockSpec((rt, half), lambda i: (i, 0)),
            pl.BlockSpec(memory_space=pl.ANY),
        ],
        out_specs=pl.BlockSpec(memory_space=pl.ANY),
        out_shape=jax.ShapeDtypeStruct((n_rows, half), jnp.int32),
        scratch_shapes=[pltpu.VMEM((rt, half), jnp.int32), pltpu.SemaphoreType.DMA(())],
        input_output_aliases={2: 0},
        compiler_params=_cparams(("arbitrary",)),
        name="dispatch_rows",
    )(dest2, hp, init)


def _combine_kernel(dest_ref, x_ref, route_ref, mod_ref, y_ref, o_ref, buf, sem):
    i = pl.program_id(0)
    n_tiles = pl.num_programs(0) - 1
    n = x_ref.shape[0]

    for s in range(2):
        @pl.when((i < n_tiles) & (i % 2 == s))
        def _(s=s):
            for r in range(n):
                for c in range(TOP_K):
                    d = dest_ref[0, 0, TOP_K * r + c]
                    pltpu.make_async_copy(y_ref.at[pl.ds(d, 1)], buf.at[s, c, pl.ds(r, 1)], sem.at[s]).start()

    @pl.when(i > 0)
    def _():
        s = (i - 1) % 2
        for c in range(TOP_K):
            pltpu.make_async_copy(y_ref.at[pl.ds(0, n)], buf.at[s, c], sem.at[s]).wait()
        g2 = mod_ref[0, 5:6, :]
        wa = route_ref[:, 2:3]
        wb = route_ref[:, 3:4]
        half = D_MODEL // 2
        a_lo, a_hi = _unpack_halves(buf[s, 0])
        b_lo, b_hi = _unpack_halves(buf[s, 1])
        o_ref[:, 0:half] = x_ref[:, 0:half] + g2[:, 0:half] * (wa * a_lo + wb * b_lo)
        o_ref[:, half:] = x_ref[:, half:] + g2[:, half:] * (wa * a_hi + wb * b_hi)


SC_WINDOW = 128


def sc_gather_rows(table, idx):
    n = idx.shape[0]
    width = table.shape[1]
    mesh = plsc.VectorSubcoreMesh(core_axis_name="core", subcore_axis_name="subcore")

    @functools.partial(pl.kernel, out_type=jax.ShapeDtypeStruct((n, width), table.dtype), mesh=mesh,
                       name="sc_gather_rows")
    def gather_kernel(x_hbm, i_hbm, o_hbm):
        def body(i_vmem, o_vmem):
            pltpu.sync_copy(x_hbm.at[i_vmem.at[0]], o_vmem)

        pltpu.emit_pipeline(
            body,
            grid=(n // SC_WINDOW,),
            in_specs=[pl.BlockSpec((1, SC_WINDOW), lambda i: (0, i))],
            out_specs=[pl.BlockSpec((SC_WINDOW, width), lambda i: (i, 0), pipeline_mode=pl.Buffered(1))],
            core_axis_name=("core", "subcore"),
            dimension_semantics=(pltpu.PARALLEL,),
        )(i_hbm, o_hbm)

    return gather_kernel(table, idx.reshape(1, n))


def sc_scatter_rows(rows, idx, n_out):
    n = idx.shape[0]
    t, width = rows.shape
    n_src = t // SC_WINDOW
    mesh = plsc.VectorSubcoreMesh(core_axis_name="core", subcore_axis_name="subcore")

    @functools.partial(pl.kernel, out_type=jax.ShapeDtypeStruct((n_out, width), rows.dtype), mesh=mesh,
                       name="sc_scatter_rows")
    def scatter_kernel(x_hbm, i_hbm, o_hbm):
        def body(x_vmem, i_vmem):
            pltpu.sync_copy(x_vmem, o_hbm.at[i_vmem.at[0]])

        pltpu.emit_pipeline(
            body,
            grid=(n // SC_WINDOW,),
            in_specs=[pl.BlockSpec((SC_WINDOW, width), lambda i: (i % n_src, 0), pipeline_mode=pl.Buffered(1)),
                      pl.BlockSpec((1, SC_WINDOW), lambda i: (0, i))],
            out_specs=[],
            core_axis_name=("core", "subcore"),
            dimension_semantics=(pltpu.PARALLEL,),
        )(x_hbm, i_hbm)

    return scatter_kernel(rows, idx.reshape(1, n))


def _combine_dense_kernel(x_ref, ya_ref, yb_ref, route_ref, mod_ref, o_ref):
    g2 = mod_ref[0, 5:6, :]
    wa = route_ref[:, 2:3]
    wb = route_ref[:, 3:4]
    half = D_MODEL // 2
    a_lo, a_hi = _unpack_halves(ya_ref[...])
    b_lo, b_hi = _unpack_halves(yb_ref[...])
    o_ref[:, 0:half] = x_ref[:, 0:half] + g2[:, 0:half] * (wa * a_lo + wb * b_lo)
    o_ref[:, half:] = x_ref[:, half:] + g2[:, half:] * (wa * a_hi + wb * b_hi)


def combine_sc(x, yb, dest, route, mod):
    B, S, D = x.shape
    T = B * S
    rt = min(S, ROW_TILE)
    per_b = S // rt
    ya = sc_gather_rows(yb, dest[:, 0])
    yc = sc_gather_rows(yb, dest[:, 1])
    row = lambda i: (i, 0)
    out = pl.pallas_call(
        _combine_dense_kernel,
        grid=(T // rt,),
        in_specs=[
            pl.BlockSpec((rt, D), row),
            pl.BlockSpec((rt, D // 2), row),
            pl.BlockSpec((rt, D // 2), row),
            pl.BlockSpec((rt, ROUTE_LANES), row),
            pl.BlockSpec((1, 6, D), lambda i: (i // per_b, 0, 0)),
        ],
        out_specs=pl.BlockSpec((rt, D), row),
        out_shape=jax.ShapeDtypeStruct((T, D), F32),
        compiler_params=_cparams(("parallel",)),
        name="combine_dense",
    )(x.reshape(T, D), ya, yc, route.reshape(T, ROUTE_LANES), mod)
    return out.reshape(B, S, D)


def combine(x, yb, dest, route, mod):
    B, S, D = x.shape
    T = B * S
    rt = min(S, ROW_TILE)
    n_tiles = T // rt
    per_b = S // rt
    dest2 = dest.reshape(n_tiles, 1, TOP_K * rt)
    prev = lambda i: jnp.maximum(i - 1, 0)
    out = pl.pallas_call(
        _combine_kernel,
        grid=(n_tiles + 1,),
        in_specs=[
            pl.BlockSpec((1, 1, TOP_K * rt), lambda i: (jnp.minimum(i, n_tiles - 1), 0, 0),
                         memory_space=pltpu.SMEM),
            pl.BlockSpec((rt, D), lambda i: (prev(i), 0)),
            pl.BlockSpec((rt, ROUTE_LANES), lambda i: (prev(i), 0)),
            pl.BlockSpec((1, 6, D), lambda i: (prev(i) // per_b, 0, 0)),
            pl.BlockSpec(memory_space=pl.ANY),
        ],
        out_specs=pl.BlockSpec((rt, D), lambda i: (prev(i), 0)),
        out_shape=jax.ShapeDtypeStruct((T, D), F32),
        scratch_shapes=[pltpu.VMEM((2, TOP_K, rt, D // 2), jnp.int32), pltpu.SemaphoreType.DMA((2,))],
        compiler_params=_cparams(("arbitrary",)),
        name="combine",
    )(dest2, x.reshape(T, D), route.reshape(T, ROUTE_LANES), mod, yb)
    return out.reshape(B, S, D)


def _prep_params(p):
    f = {}
    w_in = p["w_in"]
    L = w_in.shape[0]
    z = lambda n: jnp.zeros((L, D_MODEL, n), F32)
    f["w_in"] = jnp.concatenate(
        [w_in[:, :, 0:768], w_in[:, :, 800:1056], w_in[:, :, 1056:1312], w_in[:, :, 1312:1440],
         z(MLA_NOPE), w_in[:, :, 1440:1472], z(HEAD_PAD - MLA_QK),
         w_in[:, :, 768:800], z(128 - 2 * GLA_GATE_RANK)], axis=-1).astype(BF16)
    up = jnp.zeros((L, 128, 256), F32)
    up = up.at[:, 0:16, 0:128].set(p["gla_gate_up_f"]).at[:, 16:32, 128:256].set(p["gla_gate_up_b"])
    f["gate_up"] = up.astype(BF16)
    f["gate_bias"] = jnp.concatenate([p["gla_gate_bias_f"], p["gla_gate_bias_b"]], axis=-1)[:, None, :]
    f["lora_g"] = jnp.concatenate([p["mla_q_lora_norm_g"], p["mla_kv_lora_norm_g"]], axis=-1)[:, None, :]
    f["gla_norm_g"] = jnp.tile(p["gla_out_norm_g"], (1, GLA_HEADS))[:, None, :]
    f["norm1_g"] = p["norm1_g"][:, None, :]
    f["norm2_g"] = p["norm2_g"][:, None, :]
    wq = p["mla_w_uq"].reshape(L, MLA_Q_LORA, MLA_HEADS, MLA_QK).transpose(0, 2, 1, 3)
    f["wq"] = jnp.pad(wq, ((0, 0), (0, 0), (0, 0), (0, HEAD_PAD - MLA_QK))).astype(BF16)
    wkv = p["mla_w_ukv"].reshape(L, MLA_KV_LORA, MLA_HEADS, MLA_NOPE + MLA_V).transpose(0, 2, 1, 3)
    f["wk"] = jnp.pad(wkv[..., :MLA_NOPE], ((0, 0), (0, 0), (0, 0), (0, HEAD_PAD - MLA_NOPE))).astype(BF16)
    wv = wkv[..., MLA_NOPE:]
    zv = jnp.zeros_like(wv)
    even = (jnp.arange(MLA_HEADS) % 2 == 0)[None, :, None, None]
    f["wv"] = jnp.where(even, jnp.concatenate([wv, zv], -1), jnp.concatenate([zv, wv], -1)).astype(BF16)
    pad_qk = ((0, 0), (0, HEAD_PAD - MLA_QK))
    f["qg"] = (jnp.pad(p["mla_q_norm_g"], pad_qk) * (MLA_QK ** -0.5 * LOG2E))[:, None, :]
    f["kg"] = jnp.pad(p["mla_k_norm_g"], pad_qk)[:, None, :]
    bound = (jnp.max(jnp.abs(p["mla_q_norm_g"]), axis=-1) * jnp.max(jnp.abs(p["mla_k_norm_g"]), axis=-1)
             * (math.sqrt(MLA_QK) * 1.01 * LOG2E) + 0.1).astype(BF16).astype(F32)
    use_bound = bound <= MAX_STATIC_SHIFT * LOG2E
    f["mla_flag"] = use_bound.astype(jnp.int32)[:, None]
    f["mla_shift"] = jnp.zeros((L, 1, HEAD_PAD), F32).at[:, 0, SHIFT_LANE].set(jnp.where(use_bound, -bound, 0.0))
    f["w_out"] = p["w_out"].astype(BF16)
    rw = jnp.concatenate([p["router_group_w"], p["router_expert_w"]], axis=-1)
    rw = jnp.pad(rw, ((0, 0), (0, 0), (0, ROUTE_LANES - N_GROUPS - N_EXPERTS)))
    rw_hi = rw.astype(BF16)
    rw_lo = (rw - rw_hi.astype(F32)).astype(BF16)
    f["rw"] = jnp.stack([rw_hi, rw_lo], axis=1)
    rb = jnp.concatenate([p["router_group_b"], p["router_expert_b"]], axis=-1)
    f["rb"] = jnp.pad(rb, ((0, 0), (0, ROUTE_LANES - N_GROUPS - N_EXPERTS)))[:, None, :]
    f["w1"] = p["expert_w1"].astype(BF16)
    f["w3"] = p["expert_w3"].astype(BF16)
    f["w2"] = p["expert_w2"].astype(BF16)
    return f


def _const_tables():
    c = np.arange(FNET_W)
    same = (c[:, None] // FNET_GROUP_DIM) == (c[None, :] // FNET_GROUP_DIM)
    ang = 2.0 * np.pi * ((c[:, None] % FNET_GROUP_DIM) * (c[None, :] % FNET_GROUP_DIM) % FNET_GROUP_DIM) / FNET_GROUP_DIM
    dft64 = np.stack([np.where(same, np.cos(ang), 0.0), np.where(same, np.sin(ang), 0.0)]).astype(np.float32)
    vone = np.zeros((MLA_HEADS, 1, HEAD_PAD), np.float32)
    vone[0::2, 0, MLA_V] = 1.0
    vone[1::2, 0, 0] = 1.0
    return jnp.asarray(dft64, BF16), jnp.asarray(vone)


def _seq_tables(S):
    j = lax.broadcasted_iota(jnp.int32, (S, S), 0)
    k = lax.broadcasted_iota(jnp.int32, (S, S), 1)
    ang = ((j * k) % S).astype(F32) * (2.0 * math.pi / S)
    a_dft = jnp.concatenate([jnp.cos(ang), -jnp.sin(ang)], axis=1).astype(BF16)
    half = MLA_ROPE // 2
    freqs = ROPE_THETA ** (-jnp.arange(half, dtype=F32) / half)
    ra = jnp.arange(S, dtype=F32)[:, None] * freqs[None, :]
    cos, sin = jnp.cos(ra), jnp.sin(ra)
    one = lambda n: jnp.ones((S, n), F32)
    zero = lambda n: jnp.zeros((S, n), F32)
    tail = HEAD_PAD - MLA_QK
    rope = jnp.stack([
        jnp.concatenate([one(MLA_NOPE), cos, cos, one(tail)], axis=1),
        jnp.concatenate([zero(MLA_NOPE), -sin, sin, zero(tail)], axis=1),
    ])
    return a_dft, rope


def _dispatch_plan(route2d, counts):
    T = route2d.shape[0]
    R = T * TOP_K
    cnt = counts[0, :N_EXPERTS].astype(jnp.int32)
    padded = (cnt + MOE_ROWS - 1) // MOE_ROWS * MOE_ROWS
    pend = jnp.cumsum(padded)
    pstart = pend - padded
    start = jnp.zeros((1, ROUTE_LANES), F32).at[0, :N_EXPERTS].set(pstart.astype(F32))
    dest = route_rank(route2d, start)[:, 0:TOP_K]
    P = R + N_EXPERTS * MOE_ROWS
    n_blk = P // MOE_ROWS
    blk_first = jnp.arange(n_blk, dtype=jnp.int32) * MOE_ROWS
    blk_e = jnp.minimum(jnp.sum((pend[None, :] <= blk_first[:, None]).astype(jnp.int32), axis=1), N_EXPERTS - 1)
    n_used = (pend[-1] // MOE_ROWS).astype(jnp.int32).reshape(1)
    return dest, blk_e, n_used, P


def _trunk(x, mod_all, f, consts, tables):
    B, S, D = x.shape
    T = B * S
    dft64, vone = consts
    a_dft, rope = tables
    tm = min(S, 512)
    tq = min(S, 512)
    tk = min(S, 512)
    dft_scale = 1.0 / math.sqrt(FNET_GROUP_DIM * S)
    for l in range(DEPTH):
        mod = mod_all[l].reshape(B, 6, D)
        qkv, og, gate, xcs, cq, ckv, kr = in_proj(
            x, mod, f["norm1_g"][l], f["w_in"][l], dft64, f["gate_up"][l], f["gate_bias"][l], f["lora_g"][l], tm)
        o_gla = gla(qkv, og, gate, f["gla_norm_g"][l])
        o_fft = dft_seq(a_dft, xcs.reshape(2 * S, B * FNET_W), dft_scale)
        o_mla = mla(f["mla_flag"][l], cq, ckv, kr, rope, f["wq"][l], f["wk"][l], f["wv"][l], f["qg"][l], f["kg"][l],
                    vone, f["mla_shift"][l], tq, tk)
        x, h, route, counts = out_proj(x, o_gla, o_fft, o_mla, mod, f["w_out"][l], f["norm2_g"][l], f["rw"][l],
                                       f["rb"][l], tm)
        dest, blk_e, n_used, n_rows = _dispatch_plan(route.reshape(T, ROUTE_LANES), counts)
        xb = sc_scatter_rows(h.reshape(T, D // 2), dest.T.reshape(TOP_K * T), n_rows)
        yb = moe_mlp(xb, blk_e, n_used, f["w1"][l], f["w3"][l], f["w2"][l])
        x = combine_sc(x, yb, dest, route, mod)
    return x


def kernel(x_prompt, x_sample, c_prompt, c_sample, ada_w, ada_b, norm1_g, norm2_g, w_in, w_out, gla_gate_up_f, gla_gate_bias_f, gla_gate_up_b, gla_gate_bias_b, gla_out_norm_g, mla_q_lora_norm_g, mla_w_uq, mla_kv_lora_norm_g, mla_w_ukv, mla_q_norm_g, mla_k_norm_g, router_group_w, router_group_b, router_expert_w, router_expert_b, expert_w1, expert_w3, expert_w2):
    p = dict(norm1_g=norm1_g, norm2_g=norm2_g, w_in=w_in, w_out=w_out, gla_gate_up_f=gla_gate_up_f,
             gla_gate_bias_f=gla_gate_bias_f, gla_gate_up_b=gla_gate_up_b, gla_gate_bias_b=gla_gate_bias_b,
             gla_out_norm_g=gla_out_norm_g, mla_q_lora_norm_g=mla_q_lora_norm_g, mla_w_uq=mla_w_uq,
             mla_kv_lora_norm_g=mla_kv_lora_norm_g, mla_w_ukv=mla_w_ukv, mla_q_norm_g=mla_q_norm_g,
             mla_k_norm_g=mla_k_norm_g, router_group_w=router_group_w, router_group_b=router_group_b,
             router_expert_w=router_expert_w, router_expert_b=router_expert_b, expert_w1=expert_w1,
             expert_w3=expert_w3, expert_w2=expert_w2)
    f = _prep_params(p)
    consts = _const_tables()
    nb_p = c_prompt.shape[0]
    mod_all = ada_modulation(jnp.concatenate([c_prompt, c_sample], axis=0), ada_w, ada_b)
    y_prompt = _trunk(x_prompt, mod_all[:, :nb_p], f, consts, _seq_tables(x_prompt.shape[1]))
    y_sample = _trunk(x_sample, mod_all[:, nb_p:], f, consts, _seq_tables(x_sample.shape[1]))
    return (y_prompt, y_sample)
```

```python
import functools
import math

import jax
import jax.numpy as jnp
import numpy as np
from jax import lax
from jax.experimental import pallas as pl
from jax.experimental.pallas import tpu as pltpu
from jax.experimental.pallas import tpu_sc as plsc

F32 = jnp.float32
BF16 = jnp.bfloat16

D_MODEL = 1024
DEPTH = 4
EPS = 1e-6

GLA_HEADS = 4
GLA_DK = 32
GLA_DV = 64
GLA_QK_W = GLA_HEADS * GLA_DK
GLA_W = GLA_HEADS * GLA_DV
GLA_GATE_RANK = 16
GLA_GATE_NORMALIZER = 16.0
GLA_CHUNK = 32
GLA_BLOCK = 128

FNET_GROUPS = 4
FNET_GROUP_DIM = 64
FNET_W = 256

MLA_HEADS = 8
MLA_Q_LORA = 256
MLA_KV_LORA = 128
MLA_NOPE = 64
MLA_ROPE = 32
MLA_V = 64
MLA_QK = 96
MLA_W = 512
ROPE_THETA = 10000.0
HEAD_PAD = 128

N_GROUPS = 4
EXPERTS_PER_GROUP = 8
N_EXPERTS = 32
TOP_K = 2
D_EXPERT = 512
ROUTE_LANES = 128

C_Q, C_K, C_V, C_OG, C_F, C_CQ, C_CKV, C_KR, C_GATE = 0, 128, 256, 512, 768, 1024, 1280, 1408, 1536
P_IN_PAD = 1664

VMEM_LIMIT_BYTES = 56 * 1024 * 1024


def _cparams(sem, vmem=None):
    return pltpu.CompilerParams(dimension_semantics=sem, vmem_limit_bytes=vmem or VMEM_LIMIT_BYTES)


def _silu(x):
    return x * (1.0 / (1.0 + jnp.exp(-x)))


def _log_sigmoid(x):
    return -(jnp.maximum(-x, 0.0) + jnp.log1p(jnp.exp(-jnp.abs(x))))


def _dot(a, b):
    return jnp.dot(a, b, preferred_element_type=F32)


def _dot_nt(a, b):
    return lax.dot_general(a, b, (((1,), (1,)), ((), ())), preferred_element_type=F32)


def _split_bf16(x):
    hi = x.astype(BF16)
    lo = (x - hi.astype(F32)).astype(BF16)
    return hi, lo


def _pack_halves(x):
    n = x.shape[-1] // 2
    lo = lax.shift_right_logical(lax.bitcast_convert_type(x[:, :n], jnp.int32), 16)
    hi = lax.bitcast_convert_type(x[:, n:], jnp.int32) & jnp.int32(-65536)
    return hi | lo


def _unpack_halves(w):
    lo = lax.bitcast_convert_type(lax.shift_left(w, 16), F32)
    hi = lax.bitcast_convert_type(w & jnp.int32(-65536), F32)
    return lo, hi


def _ada_kernel(c_ref, w_ref, b_ref, o_ref):
    c = _silu(c_ref[...]).astype(BF16)
    o_ref[0] = _dot(c, w_ref[0].astype(BF16)) + b_ref[0]


def ada_modulation(c_all, ada_w, ada_b):
    nb = c_all.shape[0]
    tn = 1536
    n = ada_w.shape[-1]
    return pl.pallas_call(
        _ada_kernel,
        grid=(DEPTH, n // tn),
        in_specs=[
            pl.BlockSpec((nb, D_MODEL), lambda l, j: (0, 0)),
            pl.BlockSpec((1, D_MODEL, tn), lambda l, j: (l, 0, j)),
            pl.BlockSpec((1, 1, tn), lambda l, j: (l, 0, j)),
        ],
        out_specs=pl.BlockSpec((1, nb, tn), lambda l, j: (l, 0, j)),
        out_shape=jax.ShapeDtypeStruct((DEPTH, nb, n), F32),
        compiler_params=_cparams(("parallel", "parallel")),
        name="ada_modulation",
    )(c_all, ada_w, ada_b.reshape(DEPTH, 1, n))


def _in_proj_body(x, mod_ref, g_ref, w_ref, dft_ref, up_ref, gb_ref, lg_ref,
                  qkv_ref, og_ref, gate_ref, xcs_ref, cq_ref, ckv_ref, kr_ref):
    sh = mod_ref[0, 0:1, :]
    sc = mod_ref[0, 1:2, :]
    r = lax.rsqrt(jnp.mean(x * x, axis=-1, keepdims=True) + EPS)
    h = (x * r * g_ref[...]) * (1.0 + sc) + sh
    u = _dot(h.astype(BF16), w_ref[...])

    q = u[:, C_Q:C_K] * (GLA_DK ** -0.5)
    qkv_ref[:, 0:128] = q.astype(BF16)
    qkv_ref[:, 128:512] = u[:, C_K:C_OG].astype(BF16)
    og_ref[...] = u[:, C_OG:C_F].astype(BF16)

    ug = u[:, C_GATE:C_GATE + 128].astype(BF16)
    gl = _dot(ug, up_ref[...]) + gb_ref[...]
    gate_ref[...] = _log_sigmoid(gl) * (1.0 / GLA_GATE_NORMALIZER)

    uf = u[:, C_F:C_CQ].astype(BF16)
    xcs_ref[0] = _dot(uf, dft_ref[0]).astype(BF16)
    xcs_ref[1] = _dot(uf, dft_ref[1]).astype(BF16)

    cq = u[:, C_CQ:C_CKV]
    rq = lax.rsqrt(jnp.mean(cq * cq, axis=-1, keepdims=True) + EPS)
    cq_ref[...] = (cq * rq * lg_ref[:, 0:256]).astype(BF16)
    ckv = u[:, C_CKV:C_KR]
    rkv = lax.rsqrt(jnp.mean(ckv * ckv, axis=-1, keepdims=True) + EPS)
    ckv_ref[...] = (ckv * rkv * lg_ref[:, 256:384]).astype(BF16)
    kr_ref[...] = u[:, C_KR:C_GATE].astype(BF16)


def _in_proj_kernel(x_ref, mod_ref, g_ref, w_ref, dft_ref, up_ref, gb_ref, lg_ref,
                    qkv_ref, og_ref, gate_ref, xcs_ref, cq_ref, ckv_ref, kr_ref):
    _in_proj_body(x_ref[0], mod_ref, g_ref, w_ref, dft_ref, up_ref, gb_ref, lg_ref,
                  qkv_ref.at[0], og_ref.at[0], gate_ref.at[0], xcs_ref, cq_ref.at[0], ckv_ref.at[0], kr_ref.at[0])


def _in_proj_moe_kernel(x_ref, ya_ref, yc_ref, route_ref, modp_ref, mod_ref, g_ref, w_ref, dft_ref, up_ref, gb_ref,
                        lg_ref, xo_ref, qkv_ref, og_ref, gate_ref, xcs_ref, cq_ref, ckv_ref, kr_ref):
    x = _moe_residual(x_ref[0], ya_ref[0], yc_ref[0], route_ref[0], modp_ref)
    xo_ref[0] = x
    _in_proj_body(x, mod_ref, g_ref, w_ref, dft_ref, up_ref, gb_ref, lg_ref,
                  qkv_ref.at[0], og_ref.at[0], gate_ref.at[0], xcs_ref, cq_ref.at[0], ckv_ref.at[0], kr_ref.at[0])


def in_proj(x, mod, norm_g, w_in, dft64, gate_up, gate_bias, lora_g, tm, moe=None):
    B, S, D = x.shape
    ns = S // tm
    row = lambda b, s: (b, s, 0)
    const2 = lambda b, s: (0, 0)
    modmap = lambda b, s: (b, 0, 0)
    if moe is None:
        body, lead_in, lead_specs, lead_out_specs, lead_out_shapes = _in_proj_kernel, (x,), [pl.BlockSpec((1, tm, D), row)], (), ()
    else:
        ya, yc, route, mod_prev = moe
        body = _in_proj_moe_kernel
        lead_in = (x, ya, yc, route, mod_prev)
        lead_specs = [pl.BlockSpec((1, tm, D), row), pl.BlockSpec((1, tm, D // 2), row),
                      pl.BlockSpec((1, tm, D // 2), row), pl.BlockSpec((1, tm, ROUTE_LANES), row),
                      pl.BlockSpec((1, 6, D), modmap)]
        lead_out_specs = (pl.BlockSpec((1, tm, D), row),)
        lead_out_shapes = (jax.ShapeDtypeStruct((B, S, D), F32),)
    out_shapes = lead_out_shapes + (
        jax.ShapeDtypeStruct((B, S, 512), BF16),
        jax.ShapeDtypeStruct((B, S, GLA_W), BF16),
        jax.ShapeDtypeStruct((B, S, 256), F32),
        jax.ShapeDtypeStruct((2, S, B * FNET_W), BF16),
        jax.ShapeDtypeStruct((B, S, MLA_Q_LORA), BF16),
        jax.ShapeDtypeStruct((B, S, MLA_KV_LORA), BF16),
        jax.ShapeDtypeStruct((B, S, HEAD_PAD), BF16),
    )
    return pl.pallas_call(
        body,
        grid=(B, ns),
        in_specs=lead_specs + [
            pl.BlockSpec((1, 6, D), modmap),
            pl.BlockSpec((1, D), const2),
            pl.BlockSpec((D, P_IN_PAD), const2),
            pl.BlockSpec((2, FNET_W, FNET_W), lambda b, s: (0, 0, 0)),
            pl.BlockSpec((128, 256), const2),
            pl.BlockSpec((1, 256), const2),
            pl.BlockSpec((1, 384), const2),
        ],
        out_specs=lead_out_specs + (
            pl.BlockSpec((1, tm, 512), row),
            pl.BlockSpec((1, tm, GLA_W), row),
            pl.BlockSpec((1, tm, 256), row),
            pl.BlockSpec((2, tm, FNET_W), lambda b, s: (0, s, b)),
            pl.BlockSpec((1, tm, MLA_Q_LORA), row),
            pl.BlockSpec((1, tm, MLA_KV_LORA), row),
            pl.BlockSpec((1, tm, HEAD_PAD), row),
        ),
        out_shape=out_shapes,
        compiler_params=_cparams(("parallel", "parallel")),
        name="in_proj",
    )(*lead_in, mod, norm_g, w_in, dft64, gate_up, gate_bias, lora_g)


def _gla_kernel(qkv_ref, og_ref, gate_ref, ng_ref, o_ref, o_s, st_s, *, seq):
    R, C = GLA_BLOCK, GLA_CHUNK
    n_sub = R // C
    nblk = seq // R
    shift_c = int(math.log2(C))

    row = lax.broadcasted_iota(jnp.int32, (R, R), 0)
    col = lax.broadcasted_iota(jnp.int32, (R, R), 1)
    same = (row >> shift_c) == (col >> shift_c)
    ones_blk = jnp.where(same, 1.0, 0.0)
    lane_qk = lax.broadcasted_iota(jnp.int32, (1, GLA_QK_W), 1)
    lane_v = lax.broadcasted_iota(jnp.int32, (1, GLA_W), 1)
    head_qk = [(lane_qk >> 5) == h for h in range(GLA_HEADS)]
    head_v = [(lane_v >> 6) == h for h in range(GLA_HEADS)]
    st_row = lax.broadcasted_iota(jnp.int32, (GLA_W, GLA_QK_W), 0)
    st_col = lax.broadcasted_iota(jnp.int32, (GLA_W, GLA_QK_W), 1)
    st_mask = (st_row >> 6) == (st_col >> 5)
    sub_row = lax.broadcasted_iota(jnp.int32, (R, 1), 0) >> shift_c
    nrow = lax.broadcasted_iota(jnp.int32, (GLA_W, GLA_W), 0)
    ncol = lax.broadcasted_iota(jnp.int32, (GLA_W, GLA_W), 1)
    head_mean = jnp.where((nrow >> 6) == (ncol >> 6), 1.0 / GLA_DV, 0.0).astype(BF16)

    def direction(backward):
        if backward:
            tri = same & (col >= row)
            att_ok = same & (col > row)
            order = list(range(n_sub - 1, -1, -1))
        else:
            tri = same & (col <= row)
            att_ok = same & (col <= row)
            order = list(range(n_sub))
        cum_lhs = jnp.concatenate([jnp.where(tri, 1.0, 0.0), ones_blk], axis=0).astype(BF16)
        att_ok4 = jnp.concatenate([att_ok] * GLA_HEADS, axis=1)
        return cum_lhs, att_ok4, order, (GLA_QK_W if backward else 0), (1 if backward else 0)

    def block(j, consts):
        cum_lhs, att_ok4, order, gate_off, slot = consts
        r0 = pl.multiple_of(j * R, R)
        qkv = qkv_ref[0, pl.ds(r0, R), :]
        q = qkv[:, 0:128].astype(F32)
        k = qkv[:, 128:256].astype(F32)
        v = qkv[:, 256:512]
        g = gate_ref[0, pl.ds(r0, R), gate_off:gate_off + GLA_QK_W]
        g_hi, g_lo = _split_bf16(g)
        cs = _dot(cum_lhs, jnp.concatenate([g_hi, g_lo], axis=1))
        b = cs[0:R, 0:128] + cs[0:R, 128:256]
        bl = cs[R:2 * R, 0:128] + cs[R:2 * R, 128:256]
        q_dec = q * jnp.exp(b)
        k_inv = (k * jnp.exp(-b)).astype(BF16)
        k_end = k * jnp.exp(bl - b)
        decay = jnp.exp(bl)

        att = jnp.concatenate(
            [_dot_nt(jnp.where(head_qk[h], q_dec, 0.0).astype(BF16), k_inv) for h in range(GLA_HEADS)],
            axis=1)
        att = jnp.where(att_ok4, att, 0.0).astype(BF16)
        v_heads = jnp.concatenate([jnp.where(head_v[h], v, jnp.zeros_like(v)) for h in range(GLA_HEADS)], axis=0)
        o_blk = _dot(att, v_heads)

        v_t = v.astype(F32).T.astype(BF16)
        q_dec_b = q_dec.astype(BF16)
        inter = [None] * n_sub
        for c in order:
            st = st_s[slot]
            inter[c] = _dot_nt(q_dec_b[c * C:(c + 1) * C, :], st.astype(BF16))
            k_c = jnp.where(sub_row == c, k_end, 0.0).astype(BF16)
            d_st = _dot(v_t, k_c)
            st_s[slot] = st * decay[c * C:c * C + 1, :] + jnp.where(st_mask, d_st, 0.0)
        o_s[slot, pl.ds(r0, R), :] = o_blk + jnp.concatenate(inter, axis=0)

    fwd = direction(False)
    bwd = direction(True)
    st_s[...] = jnp.zeros_like(st_s)

    def scan_step(i, carry):
        block(i, fwd)
        block(nblk - 1 - i, bwd)
        return carry

    lax.fori_loop(0, nblk, scan_step, 0)

    def finish(j, carry):
        r0 = pl.multiple_of(j * R, R)
        o = o_s[0, pl.ds(r0, R), :] + o_s[1, pl.ds(r0, R), :]
        sq_hi, sq_lo = _split_bf16(o * o)
        ms = _dot(sq_hi, head_mean) + _dot(sq_lo, head_mean)
        y = o * lax.rsqrt(ms + EPS) * ng_ref[...]
        og = og_ref[0, pl.ds(r0, R), :].astype(F32)
        o_ref[0, pl.ds(r0, R), :] = (y * _silu(og)).astype(o_ref.dtype)
        return carry

    lax.fori_loop(0, nblk, finish, 0)


def gla(qkv, og, gate, norm_g4):
    B, S, _ = qkv.shape
    full = lambda b: (b, 0, 0)
    return pl.pallas_call(
        functools.partial(_gla_kernel, seq=S),
        grid=(B,),
        in_specs=[
            pl.BlockSpec((1, S, 512), full),
            pl.BlockSpec((1, S, GLA_W), full),
            pl.BlockSpec((1, S, 256), full),
            pl.BlockSpec((1, GLA_W), lambda b: (0, 0)),
        ],
        out_specs=pl.BlockSpec((1, S, GLA_W), full),
        out_shape=jax.ShapeDtypeStruct((B, S, GLA_W), BF16),
        scratch_shapes=[pltpu.VMEM((2, S, GLA_W), F32), pltpu.VMEM((2, GLA_W, GLA_QK_W), F32)],
        compiler_params=_cparams(("parallel",)),
        name="gla",
    )(qkv, og, gate, norm_g4)


def _dft_kernel(a_ref, b_ref, o_ref, acc_ref, *, scale):
    kk = pl.program_id(2)

    @pl.when(kk == 0)
    def _():
        acc_ref[...] = jnp.zeros_like(acc_ref)

    acc_ref[...] += _dot(a_ref[...], b_ref[...])

    @pl.when(kk == pl.num_programs(2) - 1)
    def _():
        o_ref[...] = (acc_ref[...] * scale).astype(o_ref.dtype)


def dft_seq(a_dft, xcs, scale):
    m, k = a_dft.shape
    n = xcs.shape[1]
    tm, tn, tk = min(m, 1024), min(n, 1024), min(k, 2048)
    return pl.pallas_call(
        functools.partial(_dft_kernel, scale=scale),
        grid=(m // tm, n // tn, k // tk),
        in_specs=[pl.BlockSpec((tm, tk), lambda i, j, kk: (i, kk)),
                  pl.BlockSpec((tk, tn), lambda i, j, kk: (kk, j))],
        out_specs=pl.BlockSpec((tm, tn), lambda i, j, kk: (i, j)),
        out_shape=jax.ShapeDtypeStruct((m, n), BF16),
        scratch_shapes=[pltpu.VMEM((tm, tn), F32)],
        compiler_params=_cparams(("parallel", "parallel", "arbitrary")),
        name="dft_seq",
    )(a_dft, xcs)


def _rope(x, cos, sin, swap):
    return x * cos + _dot(x.astype(BF16), swap) * sin


SHIFT_LANE = HEAD_PAD - 1
MAX_STATIC_SHIFT = 40.0
LOG2E = math.log2(math.e)


def _mla_kernel(flag_ref, cq_ref, ckv_ref, kr_ref, rope_ref, wq_ref, wk_ref, wv_ref, qg_ref, kg_ref, vone_ref,
                shift_ref, o_ref, kt_s, v_s, q_s, acc_s, *, seq, tq, tk):
    qi = pl.program_id(1)
    inv_qk = 1.0 / MLA_QK
    lane = lax.broadcasted_iota(jnp.int32, (1, HEAD_PAD), 1)
    src = lax.broadcasted_iota(jnp.int32, (HEAD_PAD, HEAD_PAD), 0)
    dst = lax.broadcasted_iota(jnp.int32, (HEAD_PAD, HEAD_PAD), 1)
    half = MLA_ROPE // 2
    lo_half = (dst >= MLA_NOPE) & (dst < MLA_NOPE + half)
    hi_half = (dst >= MLA_NOPE + half) & (dst < MLA_QK)
    swap = jnp.where((lo_half & (src == dst + half)) | (hi_half & (src == dst - half)), 1.0, 0.0).astype(BF16)

    @pl.when(qi == 0)
    def _():
        one_lane = jnp.where(lane == SHIFT_LANE, 1.0, 0.0)

        def rows(i, carry):
            r0 = pl.multiple_of(i * tk, tk)
            ckv = ckv_ref[0, pl.ds(r0, tk), :]
            kr = kr_ref[0, pl.ds(r0, tk), :].astype(F32)
            cos = rope_ref[0, pl.ds(r0, tk), :]
            sin = rope_ref[1, pl.ds(r0, tk), :]
            for h in range(MLA_HEADS):
                kp = _dot(ckv, wk_ref[h]) + kr
                r = lax.rsqrt(jnp.sum(kp * kp, axis=-1, keepdims=True) * inv_qk + EPS)
                kn = _rope(kp * r * kg_ref[...], cos, sin, swap) + one_lane
                kt_s[h, i] = kn.T.astype(BF16)
                v_s[h, pl.ds(r0, tk), :] = (_dot(ckv, wv_ref[h]) + vone_ref[h]).astype(BF16)
            return carry

        lax.fori_loop(0, seq // tk, rows, 0)

    q0 = pl.multiple_of(qi * tq, tq)
    cos = rope_ref[0, pl.ds(q0, tq), :]
    sin = rope_ref[1, pl.ds(q0, tq), :]
    for h in range(MLA_HEADS):
        qp = _dot(cq_ref[0], wq_ref[h])
        r = lax.rsqrt(jnp.sum(qp * qp, axis=-1, keepdims=True) * inv_qk + EPS)
        q_s[h] = (_rope(qp * r * qg_ref[...], cos, sin, swap) + shift_ref[...]).astype(BF16)

    def finish():
        for pair in range(MLA_HEADS // 2):
            outs = []
            for h in (2 * pair, 2 * pair + 1):
                acc = acc_s[h]
                den_lane = MLA_V if h % 2 == 0 else 0
                den = jnp.sum(jnp.where(lane == den_lane, acc, 0.0), axis=-1, keepdims=True)
                outs.append(acc * (1.0 / den))
            both = jnp.where(lane < MLA_V, outs[0], outs[1])
            o_ref[0, :, pair * HEAD_PAD:(pair + 1) * HEAD_PAD] = both.astype(o_ref.dtype)

    @pl.when(flag_ref[0] == 1)
    def _():
        acc_s[...] = jnp.zeros_like(acc_s)

        def kv_step(j, carry):
            k0 = pl.multiple_of(j * tk, tk)
            for h in range(MLA_HEADS):
                p = jnp.exp2(_dot(q_s[h], kt_s[h, j])).astype(BF16)
                acc_s[h] += _dot(p, v_s[h, pl.ds(k0, tk), :])
            return carry

        lax.fori_loop(0, seq // tk, kv_step, 0)
        finish()

    @pl.when(flag_ref[0] == 0)
    def _():
        for h in range(MLA_HEADS):
            def kv_step(j, carry, h=h):
                m, acc = carry
                k0 = pl.multiple_of(j * tk, tk)
                s = _dot(q_s[h], kt_s[h, j])
                m_new = jnp.maximum(m, jnp.max(s, axis=-1, keepdims=True))
                p = jnp.exp2(s - m_new).astype(BF16)
                return m_new, jnp.exp2(m - m_new) * acc + _dot(p, v_s[h, pl.ds(k0, tk), :])

            m0 = jnp.full((tq, 1), -jnp.inf, F32)
            _, acc = lax.fori_loop(0, seq // tk, kv_step, (m0, jnp.zeros((tq, HEAD_PAD), F32)))
            acc_s[h] = acc
        finish()


def mla(flag, cq, ckv, kr, rope, wq, wk, wv, qg, kg, vone, shift, tq, tk):
    B, S, _ = cq.shape
    c3 = lambda b, q: (0, 0, 0)
    c2 = lambda b, q: (0, 0)
    return pl.pallas_call(
        functools.partial(_mla_kernel, seq=S, tq=tq, tk=tk),
        grid=(B, S // tq),
        in_specs=[
            pl.BlockSpec(memory_space=pltpu.SMEM),
            pl.BlockSpec((1, tq, MLA_Q_LORA), lambda b, q: (b, q, 0)),
            pl.BlockSpec((1, S, MLA_KV_LORA), lambda b, q: (b, 0, 0)),
            pl.BlockSpec((1, S, HEAD_PAD), lambda b, q: (b, 0, 0)),
            pl.BlockSpec((2, S, HEAD_PAD), c3),
            pl.BlockSpec((MLA_HEADS, MLA_Q_LORA, HEAD_PAD), c3),
            pl.BlockSpec((MLA_HEADS, MLA_KV_LORA, HEAD_PAD), c3),
            pl.BlockSpec((MLA_HEADS, MLA_KV_LORA, HEAD_PAD), c3),
            pl.BlockSpec((1, HEAD_PAD), c2),
            pl.BlockSpec((1, HEAD_PAD), c2),
            pl.BlockSpec((MLA_HEADS, 1, HEAD_PAD), c3),
            pl.BlockSpec((1, HEAD_PAD), c2),
        ],
        out_specs=pl.BlockSpec((1, tq, MLA_W), lambda b, q: (b, q, 0)),
        out_shape=jax.ShapeDtypeStruct((B, S, MLA_W), BF16),
        scratch_shapes=[pltpu.VMEM((MLA_HEADS, S // tk, HEAD_PAD, tk), BF16),
                        pltpu.VMEM((MLA_HEADS, S, HEAD_PAD), BF16),
                        pltpu.VMEM((MLA_HEADS, tq, HEAD_PAD), BF16),
                        pltpu.VMEM((MLA_HEADS, tq, HEAD_PAD), F32)],
        compiler_params=_cparams(("parallel", "arbitrary")),
        name="mla",
    )(flag, cq, ckv, kr, rope, wq, wk, wv, qg, kg, vone, shift)


def _out_proj_kernel(x_ref, gla_ref, fft_ref, mla_ref, mod_ref, w_ref, g_ref, rw_ref, rb_ref,
                     xo_ref, h_ref, route_ref, cnt_ref):
    mix = (_dot(gla_ref[0], w_ref[0:256, :]) + _dot(fft_ref[...], w_ref[256:512, :])
           + _dot(mla_ref[0], w_ref[512:1024, :]))
    g1 = mod_ref[0, 2:3, :]
    sh = mod_ref[0, 3:4, :]
    sc = mod_ref[0, 4:5, :]
    x = x_ref[0] + g1 * mix
    xo_ref[0] = x
    r = lax.rsqrt(jnp.mean(x * x, axis=-1, keepdims=True) + EPS)
    h = (x * r * g_ref[...]) * (1.0 + sc) + sh
    h_hi, h_lo = _split_bf16(h)
    h_ref[0] = _pack_halves(h_hi.astype(F32))

    logit = _dot(h_hi, rw_ref[0]) + _dot(h_lo, rw_ref[0]) + _dot(h_hi, rw_ref[1]) + rb_ref[...]
    lane = lax.broadcasted_iota(jnp.int32, (1, ROUTE_LANES), 1)
    lane_f = lane.astype(F32)
    neg = -1e30
    is_g = lane < N_GROUPS
    is_e = (lane >= N_GROUPS) & (lane < N_GROUPS + N_EXPERTS)
    lg = jnp.where(is_g, logit, neg)
    g_max = jnp.max(lg, axis=-1, keepdims=True)
    g_den = jnp.sum(jnp.where(is_g, jnp.exp(lg - g_max), 0.0), axis=-1, keepdims=True)
    g_w = 1.0 / g_den
    g_top = jnp.min(jnp.where(is_g & (logit == g_max), lane_f, 1e9), axis=-1, keepdims=True)
    e_grp = ((lane - N_GROUPS) >> 3).astype(F32)
    in_grp = is_e & (e_grp == g_top)
    le = jnp.where(in_grp, logit, neg)
    t1 = jnp.max(le, axis=-1, keepdims=True)
    i1 = jnp.min(jnp.where(in_grp & (le == t1), lane_f, 1e9), axis=-1, keepdims=True)
    le2 = jnp.where(lane_f == i1, neg, le)
    t2 = jnp.max(le2, axis=-1, keepdims=True)
    i2 = jnp.min(jnp.where(in_grp & (le2 == t2), lane_f, 1e9), axis=-1, keepdims=True)
    e21 = jnp.exp(t2 - t1)
    w1 = g_w / (1.0 + e21)
    w2 = w1 * e21
    route = jnp.where(lane == 0, i1 - N_GROUPS,
                      jnp.where(lane == 1, i2 - N_GROUPS,
                                jnp.where(lane == 2, w1, jnp.where(lane == 3, w2, 0.0))))
    route_ref[0] = route

    @pl.when((pl.program_id(0) == 0) & (pl.program_id(1) == 0))
    def _():
        cnt_ref[...] = jnp.zeros_like(cnt_ref)

    picked = jnp.where((lane_f == i1 - N_GROUPS) | (lane_f == i2 - N_GROUPS), 1.0, 0.0)
    cnt_ref[...] += jnp.sum(picked, axis=0, keepdims=True)


def out_proj(x, o_gla, o_fft, o_mla, mod, w_out, norm_g, rw, rb, tm):
    B, S, D = x.shape
    row = lambda b, s: (b, s, 0)
    c2 = lambda b, s: (0, 0)
    return pl.pallas_call(
        _out_proj_kernel,
        grid=(B, S // tm),
        in_specs=[
            pl.BlockSpec((1, tm, D), row),
            pl.BlockSpec((1, tm, GLA_W), row),
            pl.BlockSpec((tm, FNET_W), lambda b, s: (s, b)),
            pl.BlockSpec((1, tm, MLA_W), row),
            pl.BlockSpec((1, 6, D), lambda b, s: (b, 0, 0)),
            pl.BlockSpec((D, D), c2),
            pl.BlockSpec((1, D), c2),
            pl.BlockSpec((2, D, ROUTE_LANES), lambda b, s: (0, 0, 0)),
            pl.BlockSpec((1, ROUTE_LANES), c2),
        ],
        out_specs=(pl.BlockSpec((1, tm, D), row), pl.BlockSpec((1, tm, D // 2), row),
                   pl.BlockSpec((1, tm, ROUTE_LANES), row), pl.BlockSpec((1, ROUTE_LANES), c2)),
        out_shape=(jax.ShapeDtypeStruct((B, S, D), F32), jax.ShapeDtypeStruct((B, S, D // 2), jnp.int32),
                   jax.ShapeDtypeStruct((B, S, ROUTE_LANES), F32), jax.ShapeDtypeStruct((1, ROUTE_LANES), F32)),
        compiler_params=_cparams(("arbitrary", "arbitrary")),
        name="out_proj",
    )(x, o_gla, o_fft, o_mla, mod, w_out, norm_g, rw, rb)


RANK_ROWS = 512


def _rank_kernel(route_ref, start_ref, dest_ref, carry_ref, before_ref):
    n = route_ref.shape[0]

    @pl.when(pl.program_id(0) == 0)
    def _():
        carry_ref[...] = jnp.zeros_like(carry_ref)
        row = lax.broadcasted_iota(jnp.int32, (n, n), 0)
        col = lax.broadcasted_iota(jnp.int32, (n, n), 1)
        before_ref[...] = jnp.where(col < row, 1.0, 0.0).astype(BF16)

    lane_f = lax.broadcasted_iota(jnp.int32, (1, ROUTE_LANES), 1).astype(F32)
    e1 = route_ref[:, 0:1]
    e2 = route_ref[:, 1:2]
    oh1 = jnp.where(lane_f == e1, 1.0, 0.0)
    oh2 = jnp.where(lane_f == e2, 1.0, 0.0)
    both = (oh1 + oh2).astype(BF16)
    pos = _dot(before_ref[...], both) + carry_ref[...] + start_ref[...]
    d1 = jnp.sum(oh1 * pos, axis=-1, keepdims=True)
    d2 = jnp.sum(oh2 * pos, axis=-1, keepdims=True)
    lane = lax.broadcasted_iota(jnp.int32, (1, ROUTE_LANES), 1)
    dest_ref[...] = jnp.where(lane == 0, d1, jnp.where(lane == 1, d2, 0.0)).astype(jnp.int32)
    carry_ref[...] += jnp.sum(oh1 + oh2, axis=0, keepdims=True)


def route_rank(route2d, start):
    T = route2d.shape[0]
    rr = min(T, RANK_ROWS)
    return pl.pallas_call(
        _rank_kernel,
        grid=(T // rr,),
        in_specs=[pl.BlockSpec((rr, ROUTE_LANES), lambda i: (i, 0)),
                  pl.BlockSpec((1, ROUTE_LANES), lambda i: (0, 0))],
        out_specs=pl.BlockSpec((rr, ROUTE_LANES), lambda i: (i, 0)),
        out_shape=jax.ShapeDtypeStruct((T, ROUTE_LANES), jnp.int32),
        scratch_shapes=[pltpu.VMEM((1, ROUTE_LANES), F32), pltpu.VMEM((rr, rr), BF16)],
        compiler_params=_cparams(("arbitrary",)),
        name="route_rank",
    )(route2d, start)


MOE_ROWS = 512


def _moe_kernel(blk_e_ref, n_used_ref, x_ref, w1_ref, w3_ref, w2_ref, o_ref):
    i = pl.program_id(0)

    @pl.when(i < n_used_ref[0])
    def _():
        half = D_MODEL // 2
        x_lo, x_hi = _unpack_halves(x_ref[...])
        x_lo = x_lo.astype(BF16)
        x_hi = x_hi.astype(BF16)
        a = _dot(x_lo, w1_ref[0, 0:half, :]) + _dot(x_hi, w1_ref[0, half:, :])
        b = _dot(x_lo, w3_ref[0, 0:half, :]) + _dot(x_hi, w3_ref[0, half:, :])
        hm = (_silu(a) * b).astype(BF16)
        y = _dot(hm, w2_ref[0])
        o_ref[...] = _pack_halves(y.astype(BF16).astype(F32))

    @pl.when(i >= n_used_ref[0])
    def _():
        o_ref[...] = jnp.zeros_like(o_ref)


def moe_mlp(xb, blk_e, n_used, w1, w3, w2):
    P, half = xb.shape
    n_blk = P // MOE_ROWS
    wmap = lambda i, be, nu: (be[i], 0, 0)
    grid_spec = pltpu.PrefetchScalarGridSpec(
        num_scalar_prefetch=2,
        grid=(n_blk,),
        in_specs=[
            pl.BlockSpec((MOE_ROWS, half), lambda i, be, nu: (i, 0)),
            pl.BlockSpec((1, D_MODEL, D_EXPERT), wmap),
            pl.BlockSpec((1, D_MODEL, D_EXPERT), wmap),
            pl.BlockSpec((1, D_EXPERT, D_MODEL), wmap),
        ],
        out_specs=pl.BlockSpec((MOE_ROWS, half), lambda i, be, nu: (i, 0)),
    )
    return pl.pallas_call(
        _moe_kernel,
        grid_spec=grid_spec,
        out_shape=jax.ShapeDtypeStruct((P, half), jnp.int32),
        compiler_params=_cparams(("arbitrary",)),
        name="moe_mlp",
    )(blk_e, n_used, xb, w1, w3, w2)


SC_WINDOW = 128


def sc_gather_rows(table, idx):
    n = idx.shape[0]
    width = table.shape[1]
    mesh = plsc.VectorSubcoreMesh(core_axis_name="core", subcore_axis_name="subcore")

    @functools.partial(pl.kernel, out_type=jax.ShapeDtypeStruct((n, width), table.dtype), mesh=mesh,
                       name="sc_gather_rows")
    def gather_kernel(x_hbm, i_hbm, o_hbm):
        def body(i_vmem, o_vmem):
            pltpu.sync_copy(x_hbm.at[i_vmem.at[0]], o_vmem)

        pltpu.emit_pipeline(
            body,
            grid=(n // SC_WINDOW,),
            in_specs=[pl.BlockSpec((1, SC_WINDOW), lambda i: (0, i))],
            out_specs=[pl.BlockSpec((SC_WINDOW, width), lambda i: (i, 0), pipeline_mode=pl.Buffered(1))],
            core_axis_name=("core", "subcore"),
            dimension_semantics=(pltpu.PARALLEL,),
        )(i_hbm, o_hbm)

    return gather_kernel(table, idx.reshape(1, n))


def sc_scatter_rows(rows, idx, n_out):
    n = idx.shape[0]
    t, width = rows.shape
    n_src = t // SC_WINDOW
    mesh = plsc.VectorSubcoreMesh(core_axis_name="core", subcore_axis_name="subcore")

    @functools.partial(pl.kernel, out_type=jax.ShapeDtypeStruct((n_out, width), rows.dtype), mesh=mesh,
                       name="sc_scatter_rows")
    def scatter_kernel(x_hbm, i_hbm, o_hbm):
        def body(x_vmem, i_vmem):
            pltpu.sync_copy(x_vmem, o_hbm.at[i_vmem.at[0]])

        pltpu.emit_pipeline(
            body,
            grid=(n // SC_WINDOW,),
            in_specs=[pl.BlockSpec((SC_WINDOW, width), lambda i: (i % n_src, 0), pipeline_mode=pl.Buffered(1)),
                      pl.BlockSpec((1, SC_WINDOW), lambda i: (0, i))],
            out_specs=[],
            core_axis_name=("core", "subcore"),
            dimension_semantics=(pltpu.PARALLEL,),
        )(x_hbm, i_hbm)

    return scatter_kernel(rows, idx.reshape(1, n))


def _moe_residual(x, ya, yc, route, modp_ref):
    g2 = modp_ref[0, 5:6, :]
    wa = route[:, 2:3]
    wb = route[:, 3:4]
    a_lo, a_hi = _unpack_halves(ya)
    b_lo, b_hi = _unpack_halves(yc)
    y = jnp.concatenate([wa * a_lo + wb * b_lo, wa * a_hi + wb * b_hi], axis=1)
    return x + g2 * y


def _combine_kernel(x_ref, ya_ref, yc_ref, route_ref, mod_ref, o_ref):
    o_ref[0] = _moe_residual(x_ref[0], ya_ref[0], yc_ref[0], route_ref[0], mod_ref)


def combine(x, ya, yc, route, mod, tm):
    B, S, D = x.shape
    row = lambda b, s: (b, s, 0)
    return pl.pallas_call(
        _combine_kernel,
        grid=(B, S // tm),
        in_specs=[
            pl.BlockSpec((1, tm, D), row),
            pl.BlockSpec((1, tm, D // 2), row),
            pl.BlockSpec((1, tm, D // 2), row),
            pl.BlockSpec((1, tm, ROUTE_LANES), row),
            pl.BlockSpec((1, 6, D), lambda b, s: (b, 0, 0)),
        ],
        out_specs=pl.BlockSpec((1, tm, D), row),
        out_shape=jax.ShapeDtypeStruct((B, S, D), F32),
        compiler_params=_cparams(("parallel", "parallel")),
        name="combine",
    )(x, ya, yc, route, mod)


def _prep_params(p):
    f = {}
    w_in = p["w_in"]
    L = w_in.shape[0]
    z = lambda n: jnp.zeros((L, D_MODEL, n), F32)
    f["w_in"] = jnp.concatenate(
        [w_in[:, :, 0:768], w_in[:, :, 800:1056], w_in[:, :, 1056:1312], w_in[:, :, 1312:1440],
         z(MLA_NOPE), w_in[:, :, 1440:1472], z(HEAD_PAD - MLA_QK),
         w_in[:, :, 768:800], z(128 - 2 * GLA_GATE_RANK)], axis=-1).astype(BF16)
    up = jnp.zeros((L, 128, 256), F32)
    up = up.at[:, 0:16, 0:128].set(p["gla_gate_up_f"]).at[:, 16:32, 128:256].set(p["gla_gate_up_b"])
    f["gate_up"] = up.astype(BF16)
    f["gate_bias"] = jnp.concatenate([p["gla_gate_bias_f"], p["gla_gate_bias_b"]], axis=-1)[:, None, :]
    f["lora_g"] = jnp.concatenate([p["mla_q_lora_norm_g"], p["mla_kv_lora_norm_g"]], axis=-1)[:, None, :]
    f["gla_norm_g"] = jnp.tile(p["gla_out_norm_g"], (1, GLA_HEADS))[:, None, :]
    f["norm1_g"] = p["norm1_g"][:, None, :]
    f["norm2_g"] = p["norm2_g"][:, None, :]
    wq = p["mla_w_uq"].reshape(L, MLA_Q_LORA, MLA_HEADS, MLA_QK).transpose(0, 2, 1, 3)
    f["wq"] = jnp.pad(wq, ((0, 0), (0, 0), (0, 0), (0, HEAD_PAD - MLA_QK))).astype(BF16)
    wkv = p["mla_w_ukv"].reshape(L, MLA_KV_LORA, MLA_HEADS, MLA_NOPE + MLA_V).transpose(0, 2, 1, 3)
    f["wk"] = jnp.pad(wkv[..., :MLA_NOPE], ((0, 0), (0, 0), (0, 0), (0, HEAD_PAD - MLA_NOPE))).astype(BF16)
    wv = wkv[..., MLA_NOPE:]
    zv = jnp.zeros_like(wv)
    even = (jnp.arange(MLA_HEADS) % 2 == 0)[None, :, None, None]
    f["wv"] = jnp.where(even, jnp.concatenate([wv, zv], -1), jnp.concatenate([zv, wv], -1)).astype(BF16)
    pad_qk = ((0, 0), (0, HEAD_PAD - MLA_QK))
    f["qg"] = (jnp.pad(p["mla_q_norm_g"], pad_qk) * (MLA_QK ** -0.5 * LOG2E))[:, None, :]
    f["kg"] = jnp.pad(p["mla_k_norm_g"], pad_qk)[:, None, :]
    bound = (jnp.max(jnp.abs(p["mla_q_norm_g"]), axis=-1) * jnp.max(jnp.abs(p["mla_k_norm_g"]), axis=-1)
             * (math.sqrt(MLA_QK) * 1.01 * LOG2E) + 0.1).astype(BF16).astype(F32)
    use_bound = bound <= MAX_STATIC_SHIFT * LOG2E
    f["mla_flag"] = use_bound.astype(jnp.int32)[:, None]
    f["mla_shift"] = jnp.zeros((L, 1, HEAD_PAD), F32).at[:, 0, SHIFT_LANE].set(jnp.where(use_bound, -bound, 0.0))
    f["w_out"] = p["w_out"].astype(BF16)
    rw = jnp.concatenate([p["router_group_w"], p["router_expert_w"]], axis=-1)
    rw = jnp.pad(rw, ((0, 0), (0, 0), (0, ROUTE_LANES - N_GROUPS - N_EXPERTS)))
    rw_hi = rw.astype(BF16)
    rw_lo = (rw - rw_hi.astype(F32)).astype(BF16)
    f["rw"] = jnp.stack([rw_hi, rw_lo], axis=1)
    rb = jnp.concatenate([p["router_group_b"], p["router_expert_b"]], axis=-1)
    f["rb"] = jnp.pad(rb, ((0, 0), (0, ROUTE_LANES - N_GROUPS - N_EXPERTS)))[:, None, :]
    f["w1"] = p["expert_w1"].astype(BF16)
    f["w3"] = p["expert_w3"].astype(BF16)
    f["w2"] = p["expert_w2"].astype(BF16)
    return f


def _const_tables():
    c = np.arange(FNET_W)
    same = (c[:, None] // FNET_GROUP_DIM) == (c[None, :] // FNET_GROUP_DIM)
    ang = 2.0 * np.pi * ((c[:, None] % FNET_GROUP_DIM) * (c[None, :] % FNET_GROUP_DIM) % FNET_GROUP_DIM) / FNET_GROUP_DIM
    dft64 = np.stack([np.where(same, np.cos(ang), 0.0), np.where(same, np.sin(ang), 0.0)]).astype(np.float32)
    vone = np.zeros((MLA_HEADS, 1, HEAD_PAD), np.float32)
    vone[0::2, 0, MLA_V] = 1.0
    vone[1::2, 0, 0] = 1.0
    return jnp.asarray(dft64, BF16), jnp.asarray(vone)


def _seq_tables(S):
    j = lax.broadcasted_iota(jnp.int32, (S, S), 0)
    k = lax.broadcasted_iota(jnp.int32, (S, S), 1)
    ang = ((j * k) % S).astype(F32) * (2.0 * math.pi / S)
    a_dft = jnp.concatenate([jnp.cos(ang), -jnp.sin(ang)], axis=1).astype(BF16)
    half = MLA_ROPE // 2
    freqs = ROPE_THETA ** (-jnp.arange(half, dtype=F32) / half)
    ra = jnp.arange(S, dtype=F32)[:, None] * freqs[None, :]
    cos, sin = jnp.cos(ra), jnp.sin(ra)
    one = lambda n: jnp.ones((S, n), F32)
    zero = lambda n: jnp.zeros((S, n), F32)
    tail = HEAD_PAD - MLA_QK
    rope = jnp.stack([
        jnp.concatenate([one(MLA_NOPE), cos, cos, one(tail)], axis=1),
        jnp.concatenate([zero(MLA_NOPE), -sin, sin, zero(tail)], axis=1),
    ])
    return a_dft, rope


def _dispatch_plan(route2d, counts):
    T = route2d.shape[0]
    R = T * TOP_K
    cnt = counts[0, :N_EXPERTS].astype(jnp.int32)
    padded = (cnt + MOE_ROWS - 1) // MOE_ROWS * MOE_ROWS
    pend = jnp.cumsum(padded)
    pstart = pend - padded
    start = jnp.zeros((1, ROUTE_LANES), F32).at[0, :N_EXPERTS].set(pstart.astype(F32))
    dest = route_rank(route2d, start)[:, 0:TOP_K]
    P = R + N_EXPERTS * MOE_ROWS
    n_blk = P // MOE_ROWS
    blk_first = jnp.arange(n_blk, dtype=jnp.int32) * MOE_ROWS
    blk_e = jnp.minimum(jnp.sum((pend[None, :] <= blk_first[:, None]).astype(jnp.int32), axis=1), N_EXPERTS - 1)
    n_used = (pend[-1] // MOE_ROWS).astype(jnp.int32).reshape(1)
    return dest, blk_e, n_used, P


def _trunk(x, mod_all, f, consts, tables):
    B, S, D = x.shape
    T = B * S
    dft64, vone = consts
    a_dft, rope = tables
    tm = min(S, 512)
    tq = min(S, 512)
    tk = min(S, 512)
    dft_scale = 1.0 / math.sqrt(FNET_GROUP_DIM * S)
    moe = None
    for l in range(DEPTH):
        mod = mod_all[l].reshape(B, 6, D)
        outs = in_proj(x, mod, f["norm1_g"][l], f["w_in"][l], dft64, f["gate_up"][l], f["gate_bias"][l],
                       f["lora_g"][l], tm, moe)
        if moe is not None:
            x, outs = outs[0], outs[1:]
        qkv, og, gate, xcs, cq, ckv, kr = outs
        o_gla = gla(qkv, og, gate, f["gla_norm_g"][l])
        o_fft = dft_seq(a_dft, xcs.reshape(2 * S, B * FNET_W), dft_scale)
        o_mla = mla(f["mla_flag"][l], cq, ckv, kr, rope, f["wq"][l], f["wk"][l], f["wv"][l], f["qg"][l], f["kg"][l],
                    vone, f["mla_shift"][l], tq, tk)
        x, h, route, counts = out_proj(x, o_gla, o_fft, o_mla, mod, f["w_out"][l], f["norm2_g"][l], f["rw"][l],
                                       f["rb"][l], tm)
        dest, blk_e, n_used, n_rows = _dispatch_plan(route.reshape(T, ROUTE_LANES), counts)
        xb = sc_scatter_rows(h.reshape(T, D // 2), dest.T.reshape(TOP_K * T), n_rows)
        yb = moe_mlp(xb, blk_e, n_used, f["w1"][l], f["w3"][l], f["w2"][l])
        ya = sc_gather_rows(yb, dest[:, 0]).reshape(B, S, D // 2)
        yc = sc_gather_rows(yb, dest[:, 1]).reshape(B, S, D // 2)
        moe = (ya, yc, route, mod)
    return combine(x, *moe, tm)


def kernel(x_prompt, x_sample, c_prompt, c_sample, ada_w, ada_b, norm1_g, norm2_g, w_in, w_out, gla_gate_up_f, gla_gate_bias_f, gla_gate_up_b, gla_gate_bias_b, gla_out_norm_g, mla_q_lora_norm_g, mla_w_uq, mla_kv_lora_norm_g, mla_w_ukv, mla_q_norm_g, mla_k_norm_g, router_group_w, router_group_b, router_expert_w, router_expert_b, expert_w1, expert_w3, expert_w2):
    p = dict(norm1_g=norm1_g, norm2_g=norm2_g, w_in=w_in, w_out=w_out, gla_gate_up_f=gla_gate_up_f,
             gla_gate_bias_f=gla_gate_bias_f, gla_gate_up_b=gla_gate_up_b, gla_gate_bias_b=gla_gate_bias_b,
             gla_out_norm_g=gla_out_norm_g, mla_q_lora_norm_g=mla_q_lora_norm_g, mla_w_uq=mla_w_uq,
             mla_kv_lora_norm_g=mla_kv_lora_norm_g, mla_w_ukv=mla_w_ukv, mla_q_norm_g=mla_q_norm_g,
             mla_k_norm_g=mla_k_norm_g, router_group_w=router_group_w, router_group_b=router_group_b,
             router_expert_w=router_expert_w, router_expert_b=router_expert_b, expert_w1=expert_w1,
             expert_w3=expert_w3, expert_w2=expert_w2)
    f = _prep_params(p)
    consts = _const_tables()
    nb_p = c_prompt.shape[0]
    mod_all = ada_modulation(jnp.concatenate([c_prompt, c_sample], axis=0), ada_w, ada_b)
    y_prompt = _trunk(x_prompt, mod_all[:, :nb_p], f, consts, _seq_tables(x_prompt.shape[1]))
    y_sample = _trunk(x_sample, mod_all[:, nb_p:], f, consts, _seq_tables(x_sample.shape[1]))
    return (y_prompt, y_sample)
```

```python
import functools
import math

import jax
import jax.numpy as jnp
import numpy as np
from jax import lax
from jax.experimental import pallas as pl
from jax.experimental.pallas import tpu as pltpu
from jax.experimental.pallas import tpu_sc as plsc

F32 = jnp.float32
BF16 = jnp.bfloat16

D_MODEL = 1024
DEPTH = 4
EPS = 1e-6

GLA_HEADS = 4
GLA_DK = 32
GLA_DV = 64
GLA_QK_W = GLA_HEADS * GLA_DK
GLA_W = GLA_HEADS * GLA_DV
GLA_GATE_RANK = 16
GLA_GATE_NORMALIZER = 16.0
GLA_CHUNK = 32
GLA_BLOCK = 128

FNET_GROUPS = 4
FNET_GROUP_DIM = 64
FNET_W = 256

MLA_HEADS = 8
MLA_Q_LORA = 256
MLA_KV_LORA = 128
MLA_NOPE = 64
MLA_ROPE = 32
MLA_V = 64
MLA_QK = 96
MLA_W = 512
ROPE_THETA = 10000.0
HEAD_PAD = 128

N_GROUPS = 4
EXPERTS_PER_GROUP = 8
N_EXPERTS = 32
TOP_K = 2
D_EXPERT = 512
ROUTE_LANES = 128

C_Q, C_K, C_V, C_OG, C_F, C_CQ, C_CKV, C_KR, C_GATE = 0, 128, 256, 512, 768, 1024, 1280, 1408, 1536
P_IN_PAD = 1664

VMEM_LIMIT_BYTES = 56 * 1024 * 1024


def _cparams(sem, vmem=None):
    return pltpu.CompilerParams(dimension_semantics=sem, vmem_limit_bytes=vmem or VMEM_LIMIT_BYTES)


def _silu(x):
    return x * (1.0 / (1.0 + jnp.exp(-x)))


def _log_sigmoid(x):
    return -(jnp.maximum(-x, 0.0) + jnp.log1p(jnp.exp(-jnp.abs(x))))


def _dot(a, b):
    return jnp.dot(a, b, preferred_element_type=F32)


def _dot_nt(a, b):
    return lax.dot_general(a, b, (((1,), (1,)), ((), ())), preferred_element_type=F32)


def _split_bf16(x):
    hi = x.astype(BF16)
    lo = (x - hi.astype(F32)).astype(BF16)
    return hi, lo


def _pack_halves(x):
    n = x.shape[-1] // 2
    lo = lax.shift_right_logical(lax.bitcast_convert_type(x[:, :n], jnp.int32), 16)
    hi = lax.bitcast_convert_type(x[:, n:], jnp.int32) & jnp.int32(-65536)
    return hi | lo


def _unpack_halves(w):
    lo = lax.bitcast_convert_type(lax.shift_left(w, 16), F32)
    hi = lax.bitcast_convert_type(w & jnp.int32(-65536), F32)
    return lo, hi


def _ada_kernel(c_ref, w_ref, b_ref, o_ref):
    c = _silu(c_ref[...]).astype(BF16)
    o_ref[0] = _dot(c, w_ref[0].astype(BF16)) + b_ref[0]


def ada_modulation(c_all, ada_w, ada_b):
    nb = c_all.shape[0]
    tn = 1536
    n = ada_w.shape[-1]
    return pl.pallas_call(
        _ada_kernel,
        grid=(DEPTH, n // tn),
        in_specs=[
            pl.BlockSpec((nb, D_MODEL), lambda l, j: (0, 0)),
            pl.BlockSpec((1, D_MODEL, tn), lambda l, j: (l, 0, j)),
            pl.BlockSpec((1, 1, tn), lambda l, j: (l, 0, j)),
        ],
        out_specs=pl.BlockSpec((1, nb, tn), lambda l, j: (l, 0, j)),
        out_shape=jax.ShapeDtypeStruct((DEPTH, nb, n), F32),
        compiler_params=_cparams(("parallel", "parallel")),
        name="ada_modulation",
    )(c_all, ada_w, ada_b.reshape(DEPTH, 1, n))


def _in_proj_body(x, mod_ref, g_ref, w_ref, dft_ref, up_ref, gb_ref, lg_ref,
                  qkv_ref, og_ref, gate_ref, xcs_ref, cq_ref, ckv_ref, kr_ref):
    sh = mod_ref[0, 0:1, :]
    sc = mod_ref[0, 1:2, :]
    r = lax.rsqrt(jnp.mean(x * x, axis=-1, keepdims=True) + EPS)
    h = (x * r * g_ref[...]) * (1.0 + sc) + sh
    u = _dot(h.astype(BF16), w_ref[...])

    q = u[:, C_Q:C_K] * (GLA_DK ** -0.5)
    qkv_ref[:, 0:128] = q.astype(BF16)
    qkv_ref[:, 128:512] = u[:, C_K:C_OG].astype(BF16)
    og_ref[...] = u[:, C_OG:C_F].astype(BF16)

    ug = u[:, C_GATE:C_GATE + 128].astype(BF16)
    gl = _dot(ug, up_ref[...]) + gb_ref[...]
    gate_ref[...] = _log_sigmoid(gl) * (1.0 / GLA_GATE_NORMALIZER)

    uf = u[:, C_F:C_CQ].astype(BF16)
    xcs_ref[0] = _dot(uf, dft_ref[0]).astype(BF16)
    xcs_ref[1] = _dot(uf, dft_ref[1]).astype(BF16)

    cq = u[:, C_CQ:C_CKV]
    rq = lax.rsqrt(jnp.mean(cq * cq, axis=-1, keepdims=True) + EPS)
    cq_ref[...] = (cq * rq * lg_ref[:, 0:256]).astype(BF16)
    ckv = u[:, C_CKV:C_KR]
    rkv = lax.rsqrt(jnp.mean(ckv * ckv, axis=-1, keepdims=True) + EPS)
    ckv_ref[...] = (ckv * rkv * lg_ref[:, 256:384]).astype(BF16)
    kr_ref[...] = u[:, C_KR:C_GATE].astype(BF16)


def _in_proj_kernel(x_ref, mod_ref, g_ref, w_ref, dft_ref, up_ref, gb_ref, lg_ref,
                    qkv_ref, og_ref, gate_ref, xcs_ref, cq_ref, ckv_ref, kr_ref):
    _in_proj_body(x_ref[0], mod_ref, g_ref, w_ref, dft_ref, up_ref, gb_ref, lg_ref,
                  qkv_ref.at[0], og_ref.at[0], gate_ref.at[0], xcs_ref, cq_ref.at[0], ckv_ref.at[0], kr_ref.at[0])


def _in_proj_moe_kernel(x_ref, ya_ref, yc_ref, route_ref, modp_ref, mod_ref, g_ref, w_ref, dft_ref, up_ref, gb_ref,
                        lg_ref, xo_ref, qkv_ref, og_ref, gate_ref, xcs_ref, cq_ref, ckv_ref, kr_ref):
    x = _moe_residual(x_ref[0], ya_ref[0], yc_ref[0], route_ref[0], modp_ref)
    xo_ref[0] = x
    _in_proj_body(x, mod_ref, g_ref, w_ref, dft_ref, up_ref, gb_ref, lg_ref,
                  qkv_ref.at[0], og_ref.at[0], gate_ref.at[0], xcs_ref, cq_ref.at[0], ckv_ref.at[0], kr_ref.at[0])


def in_proj(x, mod, norm_g, w_in, dft64, gate_up, gate_bias, lora_g, tm, moe=None):
    B, S, D = x.shape
    ns = S // tm
    row = lambda b, s: (b, s, 0)
    const2 = lambda b, s: (0, 0)
    modmap = lambda b, s: (b, 0, 0)
    if moe is None:
        body, lead_in, lead_specs, lead_out_specs, lead_out_shapes = _in_proj_kernel, (x,), [pl.BlockSpec((1, tm, D), row)], (), ()
    else:
        ya, yc, route, mod_prev = moe
        body = _in_proj_moe_kernel
        lead_in = (x, ya, yc, route, mod_prev)
        lead_specs = [pl.BlockSpec((1, tm, D), row), pl.BlockSpec((1, tm, D // 2), row),
                      pl.BlockSpec((1, tm, D // 2), row), pl.BlockSpec((1, tm, ROUTE_LANES), row),
                      pl.BlockSpec((1, 6, D), modmap)]
        lead_out_specs = (pl.BlockSpec((1, tm, D), row),)
        lead_out_shapes = (jax.ShapeDtypeStruct((B, S, D), F32),)
    out_shapes = lead_out_shapes + (
        jax.ShapeDtypeStruct((B, S, 512), BF16),
        jax.ShapeDtypeStruct((B, S, GLA_W), BF16),
        jax.ShapeDtypeStruct((B, S, 256), F32),
        jax.ShapeDtypeStruct((2, S, B * FNET_W), BF16),
        jax.ShapeDtypeStruct((B, S, MLA_Q_LORA), BF16),
        jax.ShapeDtypeStruct((B, S, MLA_KV_LORA), BF16),
        jax.ShapeDtypeStruct((B, S, HEAD_PAD), BF16),
    )
    return pl.pallas_call(
        body,
        grid=(B, ns),
        in_specs=lead_specs + [
            pl.BlockSpec((1, 6, D), modmap),
            pl.BlockSpec((1, D), const2),
            pl.BlockSpec((D, P_IN_PAD), const2),
            pl.BlockSpec((2, FNET_W, FNET_W), lambda b, s: (0, 0, 0)),
            pl.BlockSpec((128, 256), const2),
            pl.BlockSpec((1, 256), const2),
            pl.BlockSpec((1, 384), const2),
        ],
        out_specs=lead_out_specs + (
            pl.BlockSpec((1, tm, 512), row),
            pl.BlockSpec((1, tm, GLA_W), row),
            pl.BlockSpec((1, tm, 256), row),
            pl.BlockSpec((2, tm, FNET_W), lambda b, s: (0, s, b)),
            pl.BlockSpec((1, tm, MLA_Q_LORA), row),
            pl.BlockSpec((1, tm, MLA_KV_LORA), row),
            pl.BlockSpec((1, tm, HEAD_PAD), row),
        ),
        out_shape=out_shapes,
        compiler_params=_cparams(("parallel", "parallel")),
        name="in_proj",
    )(*lead_in, mod, norm_g, w_in, dft64, gate_up, gate_bias, lora_g)


def _gla_kernel(qkv_ref, og_ref, gate_ref, ng_ref, o_ref, o_s, st_s, *, seq):
    R, C = GLA_BLOCK, GLA_CHUNK
    n_sub = R // C
    nblk = seq // R
    shift_c = int(math.log2(C))

    row = lax.broadcasted_iota(jnp.int32, (R, R), 0)
    col = lax.broadcasted_iota(jnp.int32, (R, R), 1)
    same = (row >> shift_c) == (col >> shift_c)
    ones_blk = jnp.where(same, 1.0, 0.0)
    lane_qk = lax.broadcasted_iota(jnp.int32, (1, GLA_QK_W), 1)
    lane_v = lax.broadcasted_iota(jnp.int32, (1, GLA_W), 1)
    head_qk = [(lane_qk >> 5) == h for h in range(GLA_HEADS)]
    head_v = [(lane_v >> 6) == h for h in range(GLA_HEADS)]
    st_row = lax.broadcasted_iota(jnp.int32, (GLA_W, GLA_QK_W), 0)
    st_col = lax.broadcasted_iota(jnp.int32, (GLA_W, GLA_QK_W), 1)
    st_mask = (st_row >> 6) == (st_col >> 5)
    sub_row = lax.broadcasted_iota(jnp.int32, (R, 1), 0) >> shift_c
    nrow = lax.broadcasted_iota(jnp.int32, (GLA_W, GLA_W), 0)
    ncol = lax.broadcasted_iota(jnp.int32, (GLA_W, GLA_W), 1)
    head_mean = jnp.where((nrow >> 6) == (ncol >> 6), 1.0 / GLA_DV, 0.0).astype(BF16)

    def direction(backward):
        if backward:
            tri = same & (col >= row)
            att_ok = same & (col > row)
            order = list(range(n_sub - 1, -1, -1))
        else:
            tri = same & (col <= row)
            att_ok = same & (col <= row)
            order = list(range(n_sub))
        cum_lhs = jnp.concatenate([jnp.where(tri, 1.0, 0.0), ones_blk], axis=0).astype(BF16)
        att_ok4 = jnp.concatenate([att_ok] * GLA_HEADS, axis=1)
        return cum_lhs, att_ok4, order, (GLA_QK_W if backward else 0), (1 if backward else 0)

    def block(j, consts):
        cum_lhs, att_ok4, order, gate_off, slot = consts
        r0 = pl.multiple_of(j * R, R)
        qkv = qkv_ref[0, pl.ds(r0, R), :]
        q = qkv[:, 0:128].astype(F32)
        k = qkv[:, 128:256].astype(F32)
        v = qkv[:, 256:512]
        g = gate_ref[0, pl.ds(r0, R), gate_off:gate_off + GLA_QK_W]
        g_hi, g_lo = _split_bf16(g)
        cs = _dot(cum_lhs, jnp.concatenate([g_hi, g_lo], axis=1))
        b = cs[0:R, 0:128] + cs[0:R, 128:256]
        bl = cs[R:2 * R, 0:128] + cs[R:2 * R, 128:256]
        q_dec = q * jnp.exp(b)
        k_inv = (k * jnp.exp(-b)).astype(BF16)
        k_end = k * jnp.exp(bl - b)
        decay = jnp.exp(bl)

        att = jnp.concatenate(
            [_dot_nt(jnp.where(head_qk[h], q_dec, 0.0).astype(BF16), k_inv) for h in range(GLA_HEADS)],
            axis=1)
        att = jnp.where(att_ok4, att, 0.0).astype(BF16)
        v_heads = jnp.concatenate([jnp.where(head_v[h], v, jnp.zeros_like(v)) for h in range(GLA_HEADS)], axis=0)
        o_blk = _dot(att, v_heads)

        v_t = v.astype(F32).T.astype(BF16)
        q_dec_b = q_dec.astype(BF16)
        inter = [None] * n_sub
        for c in order:
            st = st_s[slot]
            inter[c] = _dot_nt(q_dec_b[c * C:(c + 1) * C, :], st.astype(BF16))
            k_c = jnp.where(sub_row == c, k_end, 0.0).astype(BF16)
            d_st = _dot(v_t, k_c)
            st_s[slot] = st * decay[c * C:c * C + 1, :] + jnp.where(st_mask, d_st, 0.0)
        o_s[slot, pl.ds(r0, R), :] = o_blk + jnp.concatenate(inter, axis=0)

    fwd = direction(False)
    bwd = direction(True)
    st_s[...] = jnp.zeros_like(st_s)

    def scan_step(i, carry):
        block(i, fwd)
        block(nblk - 1 - i, bwd)
        return carry

    lax.fori_loop(0, nblk, scan_step, 0)

    def finish(j, carry):
        r0 = pl.multiple_of(j * R, R)
        o = o_s[0, pl.ds(r0, R), :] + o_s[1, pl.ds(r0, R), :]
        sq_hi, sq_lo = _split_bf16(o * o)
        ms = _dot(sq_hi, head_mean) + _dot(sq_lo, head_mean)
        y = o * lax.rsqrt(ms + EPS) * ng_ref[...]
        og = og_ref[0, pl.ds(r0, R), :].astype(F32)
        o_ref[0, pl.ds(r0, R), :] = (y * _silu(og)).astype(o_ref.dtype)
        return carry

    lax.fori_loop(0, nblk, finish, 0)


def gla(qkv, og, gate, norm_g4):
    B, S, _ = qkv.shape
    full = lambda b: (b, 0, 0)
    return pl.pallas_call(
        functools.partial(_gla_kernel, seq=S),
        grid=(B,),
        in_specs=[
            pl.BlockSpec((1, S, 512), full),
            pl.BlockSpec((1, S, GLA_W), full),
            pl.BlockSpec((1, S, 256), full),
            pl.BlockSpec((1, GLA_W), lambda b: (0, 0)),
        ],
        out_specs=pl.BlockSpec((1, S, GLA_W), full),
        out_shape=jax.ShapeDtypeStruct((B, S, GLA_W), BF16),
        scratch_shapes=[pltpu.VMEM((2, S, GLA_W), F32), pltpu.VMEM((2, GLA_W, GLA_QK_W), F32)],
        compiler_params=_cparams(("parallel",)),
        name="gla",
    )(qkv, og, gate, norm_g4)


def _dft_kernel(a_ref, b_ref, o_ref, acc_ref, *, scale):
    kk = pl.program_id(2)

    @pl.when(kk == 0)
    def _():
        acc_ref[...] = jnp.zeros_like(acc_ref)

    acc_ref[...] += _dot(a_ref[...], b_ref[...])

    @pl.when(kk == pl.num_programs(2) - 1)
    def _():
        o_ref[...] = (acc_ref[...] * scale).astype(o_ref.dtype)


def dft_seq(a_dft, xcs, scale):
    m, k = a_dft.shape
    n = xcs.shape[1]
    tm, tn, tk = min(m, 1024), min(n, 1024), min(k, 2048)
    return pl.pallas_call(
        functools.partial(_dft_kernel, scale=scale),
        grid=(m // tm, n // tn, k // tk),
        in_specs=[pl.BlockSpec((tm, tk), lambda i, j, kk: (i, kk)),
                  pl.BlockSpec((tk, tn), lambda i, j, kk: (kk, j))],
        out_specs=pl.BlockSpec((tm, tn), lambda i, j, kk: (i, j)),
        out_shape=jax.ShapeDtypeStruct((m, n), BF16),
        scratch_shapes=[pltpu.VMEM((tm, tn), F32)],
        compiler_params=_cparams(("parallel", "parallel", "arbitrary")),
        name="dft_seq",
    )(a_dft, xcs)


def _rope(x, cos, sin, swap):
    return x * cos + _dot(x.astype(BF16), swap) * sin


SHIFT_LANE = HEAD_PAD - 1
MAX_STATIC_SHIFT = 40.0
LOG2E = math.log2(math.e)


def _mla_kernel(flag_ref, cq_ref, ckv_ref, kr_ref, rope_ref, wq_ref, wk_ref, wv_ref, qg_ref, kg_ref, vone_ref,
                shift_ref, o_ref, kt_s, v_s, q_s, acc_s, *, seq, tq, tk):
    qi = pl.program_id(1)
    inv_qk = 1.0 / MLA_QK
    lane = lax.broadcasted_iota(jnp.int32, (1, HEAD_PAD), 1)
    src = lax.broadcasted_iota(jnp.int32, (HEAD_PAD, HEAD_PAD), 0)
    dst = lax.broadcasted_iota(jnp.int32, (HEAD_PAD, HEAD_PAD), 1)
    half = MLA_ROPE // 2
    lo_half = (dst >= MLA_NOPE) & (dst < MLA_NOPE + half)
    hi_half = (dst >= MLA_NOPE + half) & (dst < MLA_QK)
    swap = jnp.where((lo_half & (src == dst + half)) | (hi_half & (src == dst - half)), 1.0, 0.0).astype(BF16)

    @pl.when(qi == 0)
    def _():
        one_lane = jnp.where(lane == SHIFT_LANE, 1.0, 0.0)

        def rows(i, carry):
            r0 = pl.multiple_of(i * tk, tk)
            ckv = ckv_ref[0, pl.ds(r0, tk), :]
            kr = kr_ref[0, pl.ds(r0, tk), :].astype(F32)
            cos = rope_ref[0, pl.ds(r0, tk), :]
            sin = rope_ref[1, pl.ds(r0, tk), :]
            for h in range(MLA_HEADS):
                kp = _dot(ckv, wk_ref[h]) + kr
                r = lax.rsqrt(jnp.sum(kp * kp, axis=-1, keepdims=True) * inv_qk + EPS)
                kn = _rope(kp * r * kg_ref[...], cos, sin, swap) + one_lane
                kt_s[h, i] = kn.T.astype(BF16)
                v_s[h, pl.ds(r0, tk), :] = (_dot(ckv, wv_ref[h]) + vone_ref[h]).astype(BF16)
            return carry

        lax.fori_loop(0, seq // tk, rows, 0)

    q0 = pl.multiple_of(qi * tq, tq)
    cos = rope_ref[0, pl.ds(q0, tq), :]
    sin = rope_ref[1, pl.ds(q0, tq), :]
    for h in range(MLA_HEADS):
        qp = _dot(cq_ref[0], wq_ref[h])
        r = lax.rsqrt(jnp.sum(qp * qp, axis=-1, keepdims=True) * inv_qk + EPS)
        q_s[h] = (_rope(qp * r * qg_ref[...], cos, sin, swap) + shift_ref[...]).astype(BF16)

    def finish():
        for pair in range(MLA_HEADS // 2):
            outs = []
            for h in (2 * pair, 2 * pair + 1):
                acc = acc_s[h]
                den_lane = MLA_V if h % 2 == 0 else 0
                den = jnp.sum(jnp.where(lane == den_lane, acc, 0.0), axis=-1, keepdims=True)
                outs.append(acc * (1.0 / den))
            both = jnp.where(lane < MLA_V, outs[0], outs[1])
            o_ref[0, :, pair * HEAD_PAD:(pair + 1) * HEAD_PAD] = both.astype(o_ref.dtype)

    @pl.when(flag_ref[0] == 1)
    def _():
        acc_s[...] = jnp.zeros_like(acc_s)

        def kv_step(j, carry):
            k0 = pl.multiple_of(j * tk, tk)
            for h in range(MLA_HEADS):
                p = jnp.exp2(_dot(q_s[h], kt_s[h, j])).astype(BF16)
                acc_s[h] += _dot(p, v_s[h, pl.ds(k0, tk), :])
            return carry

        lax.fori_loop(0, seq // tk, kv_step, 0)
        finish()

    @pl.when(flag_ref[0] == 0)
    def _():
        for h in range(MLA_HEADS):
            def kv_step(j, carry, h=h):
                m, acc = carry
                k0 = pl.multiple_of(j * tk, tk)
                s = _dot(q_s[h], kt_s[h, j])
                m_new = jnp.maximum(m, jnp.max(s, axis=-1, keepdims=True))
                p = jnp.exp2(s - m_new).astype(BF16)
                return m_new, jnp.exp2(m - m_new) * acc + _dot(p, v_s[h, pl.ds(k0, tk), :])

            m0 = jnp.full((tq, 1), -jnp.inf, F32)
            _, acc = lax.fori_loop(0, seq // tk, kv_step, (m0, jnp.zeros((tq, HEAD_PAD), F32)))
            acc_s[h] = acc
        finish()


def mla(flag, cq, ckv, kr, rope, wq, wk, wv, qg, kg, vone, shift, tq, tk):
    B, S, _ = cq.shape
    c3 = lambda b, q: (0, 0, 0)
    c2 = lambda b, q: (0, 0)
    return pl.pallas_call(
        functools.partial(_mla_kernel, seq=S, tq=tq, tk=tk),
        grid=(B, S // tq),
        in_specs=[
            pl.BlockSpec(memory_space=pltpu.SMEM),
            pl.BlockSpec((1, tq, MLA_Q_LORA), lambda b, q: (b, q, 0)),
            pl.BlockSpec((1, S, MLA_KV_LORA), lambda b, q: (b, 0, 0)),
            pl.BlockSpec((1, S, HEAD_PAD), lambda b, q: (b, 0, 0)),
            pl.BlockSpec((2, S, HEAD_PAD), c3),
            pl.BlockSpec((MLA_HEADS, MLA_Q_LORA, HEAD_PAD), c3),
            pl.BlockSpec((MLA_HEADS, MLA_KV_LORA, HEAD_PAD), c3),
            pl.BlockSpec((MLA_HEADS, MLA_KV_LORA, HEAD_PAD), c3),
            pl.BlockSpec((1, HEAD_PAD), c2),
            pl.BlockSpec((1, HEAD_PAD), c2),
            pl.BlockSpec((MLA_HEADS, 1, HEAD_PAD), c3),
            pl.BlockSpec((1, HEAD_PAD), c2),
        ],
        out_specs=pl.BlockSpec((1, tq, MLA_W), lambda b, q: (b, q, 0)),
        out_shape=jax.ShapeDtypeStruct((B, S, MLA_W), BF16),
        scratch_shapes=[pltpu.VMEM((MLA_HEADS, S // tk, HEAD_PAD, tk), BF16),
                        pltpu.VMEM((MLA_HEADS, S, HEAD_PAD), BF16),
                        pltpu.VMEM((MLA_HEADS, tq, HEAD_PAD), BF16),
                        pltpu.VMEM((MLA_HEADS, tq, HEAD_PAD), F32)],
        compiler_params=_cparams(("parallel", "arbitrary")),
        name="mla",
    )(flag, cq, ckv, kr, rope, wq, wk, wv, qg, kg, vone, shift)


def _out_proj_kernel(x_ref, gla_ref, fft_ref, mla_ref, mod_ref, w_ref, g_ref, rw_ref, rb_ref,
                     xo_ref, h_ref, route_ref, cnt_ref):
    mix = (_dot(gla_ref[0], w_ref[0:256, :]) + _dot(fft_ref[...], w_ref[256:512, :])
           + _dot(mla_ref[0], w_ref[512:1024, :]))
    g1 = mod_ref[0, 2:3, :]
    sh = mod_ref[0, 3:4, :]
    sc = mod_ref[0, 4:5, :]
    x = x_ref[0] + g1 * mix
    xo_ref[0] = x
    r = lax.rsqrt(jnp.mean(x * x, axis=-1, keepdims=True) + EPS)
    h = (x * r * g_ref[...]) * (1.0 + sc) + sh
    h_hi, h_lo = _split_bf16(h)
    h_ref[0] = _pack_halves(h_hi.astype(F32))

    logit = _dot(h_hi, rw_ref[0]) + _dot(h_lo, rw_ref[0]) + _dot(h_hi, rw_ref[1]) + rb_ref[...]
    lane = lax.broadcasted_iota(jnp.int32, (1, ROUTE_LANES), 1)
    lane_f = lane.astype(F32)
    neg = -1e30
    is_g = lane < N_GROUPS
    is_e = (lane >= N_GROUPS) & (lane < N_GROUPS + N_EXPERTS)
    lg = jnp.where(is_g, logit, neg)
    g_max = jnp.max(lg, axis=-1, keepdims=True)
    g_den = jnp.sum(jnp.where(is_g, jnp.exp(lg - g_max), 0.0), axis=-1, keepdims=True)
    g_w = 1.0 / g_den
    g_top = jnp.min(jnp.where(is_g & (logit == g_max), lane_f, 1e9), axis=-1, keepdims=True)
    e_grp = ((lane - N_GROUPS) >> 3).astype(F32)
    in_grp = is_e & (e_grp == g_top)
    le = jnp.where(in_grp, logit, neg)
    t1 = jnp.max(le, axis=-1, keepdims=True)
    i1 = jnp.min(jnp.where(in_grp & (le == t1), lane_f, 1e9), axis=-1, keepdims=True)
    le2 = jnp.where(lane_f == i1, neg, le)
    t2 = jnp.max(le2, axis=-1, keepdims=True)
    i2 = jnp.min(jnp.where(in_grp & (le2 == t2), lane_f, 1e9), axis=-1, keepdims=True)
    e21 = jnp.exp(t2 - t1)
    w1 = g_w / (1.0 + e21)
    w2 = w1 * e21
    route = jnp.where(lane == 0, i1 - N_GROUPS,
                      jnp.where(lane == 1, i2 - N_GROUPS,
                                jnp.where(lane == 2, w1, jnp.where(lane == 3, w2, 0.0))))
    route_ref[0] = route

    @pl.when((pl.program_id(0) == 0) & (pl.program_id(1) == 0))
    def _():
        cnt_ref[...] = jnp.zeros_like(cnt_ref)

    picked = jnp.where((lane_f == i1 - N_GROUPS) | (lane_f == i2 - N_GROUPS), 1.0, 0.0)
    cnt_ref[...] += jnp.sum(picked, axis=0, keepdims=True)


def out_proj(x, o_gla, o_fft, o_mla, mod, w_out, norm_g, rw, rb, tm):
    B, S, D = x.shape
    row = lambda b, s: (b, s, 0)
    c2 = lambda b, s: (0, 0)
    return pl.pallas_call(
        _out_proj_kernel,
        grid=(B, S // tm),
        in_specs=[
            pl.BlockSpec((1, tm, D), row),
            pl.BlockSpec((1, tm, GLA_W), row),
            pl.BlockSpec((tm, FNET_W), lambda b, s: (s, b)),
            pl.BlockSpec((1, tm, MLA_W), row),
            pl.BlockSpec((1, 6, D), lambda b, s: (b, 0, 0)),
            pl.BlockSpec((D, D), c2),
            pl.BlockSpec((1, D), c2),
            pl.BlockSpec((2, D, ROUTE_LANES), lambda b, s: (0, 0, 0)),
            pl.BlockSpec((1, ROUTE_LANES), c2),
        ],
        out_specs=(pl.BlockSpec((1, tm, D), row), pl.BlockSpec((1, tm, D // 2), row),
                   pl.BlockSpec((1, tm, ROUTE_LANES), row), pl.BlockSpec((1, ROUTE_LANES), c2)),
        out_shape=(jax.ShapeDtypeStruct((B, S, D), F32), jax.ShapeDtypeStruct((B, S, D // 2), jnp.int32),
                   jax.ShapeDtypeStruct((B, S, ROUTE_LANES), F32), jax.ShapeDtypeStruct((1, ROUTE_LANES), F32)),
        compiler_params=_cparams(("arbitrary", "arbitrary")),
        name="out_proj",
    )(x, o_gla, o_fft, o_mla, mod, w_out, norm_g, rw, rb)


RANK_ROWS = 512


def _rank_kernel(route_ref, start_ref, dest_ref, carry_ref, before_ref):
    n = route_ref.shape[0]

    @pl.when(pl.program_id(0) == 0)
    def _():
        carry_ref[...] = jnp.zeros_like(carry_ref)
        row = lax.broadcasted_iota(jnp.int32, (n, n), 0)
        col = lax.broadcasted_iota(jnp.int32, (n, n), 1)
        before_ref[...] = jnp.where(col < row, 1.0, 0.0).astype(BF16)

    lane_f = lax.broadcasted_iota(jnp.int32, (1, ROUTE_LANES), 1).astype(F32)
    e1 = route_ref[:, 0:1]
    e2 = route_ref[:, 1:2]
    oh1 = jnp.where(lane_f == e1, 1.0, 0.0)
    oh2 = jnp.where(lane_f == e2, 1.0, 0.0)
    both = (oh1 + oh2).astype(BF16)
    pos = _dot(before_ref[...], both) + carry_ref[...] + start_ref[...]
    d1 = jnp.sum(oh1 * pos, axis=-1, keepdims=True)
    d2 = jnp.sum(oh2 * pos, axis=-1, keepdims=True)
    lane = lax.broadcasted_iota(jnp.int32, (1, ROUTE_LANES), 1)
    cols = jnp.where(lane == 0, d1, jnp.where(lane == 1, d2, 0.0))
    dest_ref[...] = cols.T[0:TOP_K, :].astype(jnp.int32)
    carry_ref[...] += jnp.sum(oh1 + oh2, axis=0, keepdims=True)


def route_rank(route2d, start):
    T = route2d.shape[0]
    rr = min(T, RANK_ROWS)
    return pl.pallas_call(
        _rank_kernel,
        grid=(T // rr,),
        in_specs=[pl.BlockSpec((rr, ROUTE_LANES), lambda i: (i, 0)),
                  pl.BlockSpec((1, ROUTE_LANES), lambda i: (0, 0))],
        out_specs=pl.BlockSpec((TOP_K, rr), lambda i: (0, i)),
        out_shape=jax.ShapeDtypeStruct((TOP_K, T), jnp.int32),
        scratch_shapes=[pltpu.VMEM((1, ROUTE_LANES), F32), pltpu.VMEM((rr, rr), BF16)],
        compiler_params=_cparams(("arbitrary",)),
        name="route_rank",
    )(route2d, start)


MOE_ROWS = 512


def _moe_kernel(blk_e_ref, n_used_ref, x_ref, w1_ref, w3_ref, w2_ref, o_ref):
    i = pl.program_id(0)

    @pl.when(i < n_used_ref[0])
    def _():
        half = D_MODEL // 2
        x_lo, x_hi = _unpack_halves(x_ref[...])
        x_lo = x_lo.astype(BF16)
        x_hi = x_hi.astype(BF16)
        w1 = w1_ref[0].astype(BF16)
        w3 = w3_ref[0].astype(BF16)
        a = _dot(x_lo, w1[0:half, :]) + _dot(x_hi, w1[half:, :])
        b = _dot(x_lo, w3[0:half, :]) + _dot(x_hi, w3[half:, :])
        hm = (_silu(a) * b).astype(BF16)
        y = _dot(hm, w2_ref[0].astype(BF16))
        o_ref[...] = _pack_halves(y.astype(BF16).astype(F32))

    @pl.when(i >= n_used_ref[0])
    def _():
        o_ref[...] = jnp.zeros_like(o_ref)


def moe_mlp(xb, blk_e, n_used, w1, w3, w2):
    P, half = xb.shape
    n_blk = P // MOE_ROWS
    wmap = lambda i, be, nu: (be[i], 0, 0)
    grid_spec = pltpu.PrefetchScalarGridSpec(
        num_scalar_prefetch=2,
        grid=(n_blk,),
        in_specs=[
            pl.BlockSpec((MOE_ROWS, half), lambda i, be, nu: (i, 0)),
            pl.BlockSpec((1, D_MODEL, D_EXPERT), wmap),
            pl.BlockSpec((1, D_MODEL, D_EXPERT), wmap),
            pl.BlockSpec((1, D_EXPERT, D_MODEL), wmap),
        ],
        out_specs=pl.BlockSpec((MOE_ROWS, half), lambda i, be, nu: (i, 0)),
    )
    return pl.pallas_call(
        _moe_kernel,
        grid_spec=grid_spec,
        out_shape=jax.ShapeDtypeStruct((P, half), jnp.int32),
        compiler_params=_cparams(("arbitrary",)),
        name="moe_mlp",
    )(blk_e, n_used, xb, w1, w3, w2)


SC_WINDOW = 128


def sc_gather_rows(table, idx):
    n = idx.shape[0]
    width = table.shape[1]
    mesh = plsc.VectorSubcoreMesh(core_axis_name="core", subcore_axis_name="subcore")

    @functools.partial(pl.kernel, out_type=jax.ShapeDtypeStruct((n, width), table.dtype), mesh=mesh,
                       name="sc_gather_rows")
    def gather_kernel(x_hbm, i_hbm, o_hbm):
        def body(i_vmem, o_vmem):
            pltpu.sync_copy(x_hbm.at[i_vmem.at[0]], o_vmem)

        pltpu.emit_pipeline(
            body,
            grid=(n // SC_WINDOW,),
            in_specs=[pl.BlockSpec((1, SC_WINDOW), lambda i: (0, i))],
            out_specs=[pl.BlockSpec((SC_WINDOW, width), lambda i: (i, 0), pipeline_mode=pl.Buffered(1))],
            core_axis_name=("core", "subcore"),
            dimension_semantics=(pltpu.PARALLEL,),
        )(i_hbm, o_hbm)

    return gather_kernel(table, idx.reshape(1, n))


def sc_scatter_rows(rows, idx, n_out):
    n = idx.shape[0]
    t, width = rows.shape
    n_src = t // SC_WINDOW
    mesh = plsc.VectorSubcoreMesh(core_axis_name="core", subcore_axis_name="subcore")

    @functools.partial(pl.kernel, out_type=jax.ShapeDtypeStruct((n_out, width), rows.dtype), mesh=mesh,
                       name="sc_scatter_rows")
    def scatter_kernel(x_hbm, i_hbm, o_hbm):
        def body(x_vmem, i_vmem):
            pltpu.sync_copy(x_vmem, o_hbm.at[i_vmem.at[0]])

        pltpu.emit_pipeline(
            body,
            grid=(n // SC_WINDOW,),
            in_specs=[pl.BlockSpec((SC_WINDOW, width), lambda i: (i % n_src, 0), pipeline_mode=pl.Buffered(1)),
                      pl.BlockSpec((1, SC_WINDOW), lambda i: (0, i))],
            out_specs=[],
            core_axis_name=("core", "subcore"),
            dimension_semantics=(pltpu.PARALLEL,),
        )(x_hbm, i_hbm)

    return scatter_kernel(rows, idx.reshape(1, n))


def _moe_residual(x, ya, yc, route, modp_ref):
    g2 = modp_ref[0, 5:6, :]
    wa = route[:, 2:3]
    wb = route[:, 3:4]
    a_lo, a_hi = _unpack_halves(ya)
    b_lo, b_hi = _unpack_halves(yc)
    y = jnp.concatenate([wa * a_lo + wb * b_lo, wa * a_hi + wb * b_hi], axis=1)
    return x + g2 * y


def _combine_kernel(x_ref, ya_ref, yc_ref, route_ref, mod_ref, o_ref):
    o_ref[0] = _moe_residual(x_ref[0], ya_ref[0], yc_ref[0], route_ref[0], mod_ref)


def combine(x, ya, yc, route, mod, tm):
    B, S, D = x.shape
    row = lambda b, s: (b, s, 0)
    return pl.pallas_call(
        _combine_kernel,
        grid=(B, S // tm),
        in_specs=[
            pl.BlockSpec((1, tm, D), row),
            pl.BlockSpec((1, tm, D // 2), row),
            pl.BlockSpec((1, tm, D // 2), row),
            pl.BlockSpec((1, tm, ROUTE_LANES), row),
            pl.BlockSpec((1, 6, D), lambda b, s: (b, 0, 0)),
        ],
        out_specs=pl.BlockSpec((1, tm, D), row),
        out_shape=jax.ShapeDtypeStruct((B, S, D), F32),
        compiler_params=_cparams(("parallel", "parallel")),
        name="combine",
    )(x, ya, yc, route, mod)


def _prep_params(p):
    f = {}
    w_in = p["w_in"]
    L = w_in.shape[0]
    z = lambda n: jnp.zeros((L, D_MODEL, n), F32)
    f["w_in"] = jnp.concatenate(
        [w_in[:, :, 0:768], w_in[:, :, 800:1056], w_in[:, :, 1056:1312], w_in[:, :, 1312:1440],
         z(MLA_NOPE), w_in[:, :, 1440:1472], z(HEAD_PAD - MLA_QK),
         w_in[:, :, 768:800], z(128 - 2 * GLA_GATE_RANK)], axis=-1).astype(BF16)
    up = jnp.zeros((L, 128, 256), F32)
    up = up.at[:, 0:16, 0:128].set(p["gla_gate_up_f"]).at[:, 16:32, 128:256].set(p["gla_gate_up_b"])
    f["gate_up"] = up.astype(BF16)
    f["gate_bias"] = jnp.concatenate([p["gla_gate_bias_f"], p["gla_gate_bias_b"]], axis=-1)[:, None, :]
    f["lora_g"] = jnp.concatenate([p["mla_q_lora_norm_g"], p["mla_kv_lora_norm_g"]], axis=-1)[:, None, :]
    f["gla_norm_g"] = jnp.tile(p["gla_out_norm_g"], (1, GLA_HEADS))[:, None, :]
    f["norm1_g"] = p["norm1_g"][:, None, :]
    f["norm2_g"] = p["norm2_g"][:, None, :]
    wq = p["mla_w_uq"].reshape(L, MLA_Q_LORA, MLA_HEADS, MLA_QK).transpose(0, 2, 1, 3)
    f["wq"] = jnp.pad(wq, ((0, 0), (0, 0), (0, 0), (0, HEAD_PAD - MLA_QK))).astype(BF16)
    wkv = p["mla_w_ukv"].reshape(L, MLA_KV_LORA, MLA_HEADS, MLA_NOPE + MLA_V).transpose(0, 2, 1, 3)
    f["wk"] = jnp.pad(wkv[..., :MLA_NOPE], ((0, 0), (0, 0), (0, 0), (0, HEAD_PAD - MLA_NOPE))).astype(BF16)
    wv = wkv[..., MLA_NOPE:]
    zv = jnp.zeros_like(wv)
    even = (jnp.arange(MLA_HEADS) % 2 == 0)[None, :, None, None]
    f["wv"] = jnp.where(even, jnp.concatenate([wv, zv], -1), jnp.concatenate([zv, wv], -1)).astype(BF16)
    pad_qk = ((0, 0), (0, HEAD_PAD - MLA_QK))
    f["qg"] = (jnp.pad(p["mla_q_norm_g"], pad_qk) * (MLA_QK ** -0.5 * LOG2E))[:, None, :]
    f["kg"] = jnp.pad(p["mla_k_norm_g"], pad_qk)[:, None, :]
    bound = (jnp.max(jnp.abs(p["mla_q_norm_g"]), axis=-1) * jnp.max(jnp.abs(p["mla_k_norm_g"]), axis=-1)
             * (math.sqrt(MLA_QK) * 1.01 * LOG2E) + 0.1).astype(BF16).astype(F32)
    use_bound = bound <= MAX_STATIC_SHIFT * LOG2E
    f["mla_flag"] = use_bound.astype(jnp.int32)[:, None]
    f["mla_shift"] = jnp.zeros((L, 1, HEAD_PAD), F32).at[:, 0, SHIFT_LANE].set(jnp.where(use_bound, -bound, 0.0))
    f["w_out"] = p["w_out"].astype(BF16)
    rw = jnp.concatenate([p["router_group_w"], p["router_expert_w"]], axis=-1)
    rw = jnp.pad(rw, ((0, 0), (0, 0), (0, ROUTE_LANES - N_GROUPS - N_EXPERTS)))
    rw_hi = rw.astype(BF16)
    rw_lo = (rw - rw_hi.astype(F32)).astype(BF16)
    f["rw"] = jnp.stack([rw_hi, rw_lo], axis=1)
    rb = jnp.concatenate([p["router_group_b"], p["router_expert_b"]], axis=-1)
    f["rb"] = jnp.pad(rb, ((0, 0), (0, ROUTE_LANES - N_GROUPS - N_EXPERTS)))[:, None, :]
    f["w1"] = p["expert_w1"]
    f["w3"] = p["expert_w3"]
    f["w2"] = p["expert_w2"]
    return f


def _const_tables():
    c = np.arange(FNET_W)
    same = (c[:, None] // FNET_GROUP_DIM) == (c[None, :] // FNET_GROUP_DIM)
    ang = 2.0 * np.pi * ((c[:, None] % FNET_GROUP_DIM) * (c[None, :] % FNET_GROUP_DIM) % FNET_GROUP_DIM) / FNET_GROUP_DIM
    dft64 = np.stack([np.where(same, np.cos(ang), 0.0), np.where(same, np.sin(ang), 0.0)]).astype(np.float32)
    vone = np.zeros((MLA_HEADS, 1, HEAD_PAD), np.float32)
    vone[0::2, 0, MLA_V] = 1.0
    vone[1::2, 0, 0] = 1.0
    return jnp.asarray(dft64, BF16), jnp.asarray(vone)


def _seq_tables(S):
    j = lax.broadcasted_iota(jnp.int32, (S, S), 0)
    k = lax.broadcasted_iota(jnp.int32, (S, S), 1)
    ang = ((j * k) % S).astype(F32) * (2.0 * math.pi / S)
    a_dft = jnp.concatenate([jnp.cos(ang), -jnp.sin(ang)], axis=1).astype(BF16)
    half = MLA_ROPE // 2
    freqs = ROPE_THETA ** (-jnp.arange(half, dtype=F32) / half)
    ra = jnp.arange(S, dtype=F32)[:, None] * freqs[None, :]
    cos, sin = jnp.cos(ra), jnp.sin(ra)
    one = lambda n: jnp.ones((S, n), F32)
    zero = lambda n: jnp.zeros((S, n), F32)
    tail = HEAD_PAD - MLA_QK
    rope = jnp.stack([
        jnp.concatenate([one(MLA_NOPE), cos, cos, one(tail)], axis=1),
        jnp.concatenate([zero(MLA_NOPE), -sin, sin, zero(tail)], axis=1),
    ])
    return a_dft, rope


def _dispatch_plan(route2d, counts):
    T = route2d.shape[0]
    R = T * TOP_K
    cnt = counts[0, :N_EXPERTS].astype(jnp.int32)
    padded = (cnt + MOE_ROWS - 1) // MOE_ROWS * MOE_ROWS
    pend = jnp.cumsum(padded)
    pstart = pend - padded
    start = jnp.zeros((1, ROUTE_LANES), F32).at[0, :N_EXPERTS].set(pstart.astype(F32))
    dest = route_rank(route2d, start)
    P = R + N_EXPERTS * MOE_ROWS
    n_blk = P // MOE_ROWS
    blk_first = jnp.arange(n_blk, dtype=jnp.int32) * MOE_ROWS
    blk_e = jnp.minimum(jnp.sum((pend[None, :] <= blk_first[:, None]).astype(jnp.int32), axis=1), N_EXPERTS - 1)
    n_used = (pend[-1] // MOE_ROWS).astype(jnp.int32).reshape(1)
    return dest, blk_e, n_used, P


def _trunk(x, mod_all, f, consts, tables):
    B, S, D = x.shape
    T = B * S
    dft64, vone = consts
    a_dft, rope = tables
    tm = min(S, 512)
    tq = min(S, 512)
    tk = min(S, 512)
    dft_scale = 1.0 / math.sqrt(FNET_GROUP_DIM * S)
    moe = None
    for l in range(DEPTH):
        mod = mod_all[l].reshape(B, 6, D)
        outs = in_proj(x, mod, f["norm1_g"][l], f["w_in"][l], dft64, f["gate_up"][l], f["gate_bias"][l],
                       f["lora_g"][l], tm, moe)
        if moe is not None:
            x, outs = outs[0], outs[1:]
        qkv, og, gate, xcs, cq, ckv, kr = outs
        o_gla = gla(qkv, og, gate, f["gla_norm_g"][l])
        o_fft = dft_seq(a_dft, xcs.reshape(2 * S, B * FNET_W), dft_scale)
        o_mla = mla(f["mla_flag"][l], cq, ckv, kr, rope, f["wq"][l], f["wk"][l], f["wv"][l], f["qg"][l], f["kg"][l],
                    vone, f["mla_shift"][l], tq, tk)
        x, h, route, counts = out_proj(x, o_gla, o_fft, o_mla, mod, f["w_out"][l], f["norm2_g"][l], f["rw"][l],
                                       f["rb"][l], tm)
        dest, blk_e, n_used, n_rows = _dispatch_plan(route.reshape(T, ROUTE_LANES), counts)
        xb = sc_scatter_rows(h.reshape(T, D // 2), dest.reshape(TOP_K * T), n_rows)
        yb = moe_mlp(xb, blk_e, n_used, f["w1"][l], f["w3"][l], f["w2"][l])
        ya = sc_gather_rows(yb, dest[0]).reshape(B, S, D // 2)
        yc = sc_gather_rows(yb, dest[1]).reshape(B, S, D // 2)
        moe = (ya, yc, route, mod)
    return combine(x, *moe, tm)


def kernel(x_prompt, x_sample, c_prompt, c_sample, ada_w, ada_b, norm1_g, norm2_g, w_in, w_out, gla_gate_up_f, gla_gate_bias_f, gla_gate_up_b, gla_gate_bias_b, gla_out_norm_g, mla_q_lora_norm_g, mla_w_uq, mla_kv_lora_norm_g, mla_w_ukv, mla_q_norm_g, mla_k_norm_g, router_group_w, router_group_b, router_expert_w, router_expert_b, expert_w1, expert_w3, expert_w2):
    p = dict(norm1_g=norm1_g, norm2_g=norm2_g, w_in=w_in, w_out=w_out, gla_gate_up_f=gla_gate_up_f,
             gla_gate_bias_f=gla_gate_bias_f, gla_gate_up_b=gla_gate_up_b, gla_gate_bias_b=gla_gate_bias_b,
             gla_out_norm_g=gla_out_norm_g, mla_q_lora_norm_g=mla_q_lora_norm_g, mla_w_uq=mla_w_uq,
             mla_kv_lora_norm_g=mla_kv_lora_norm_g, mla_w_ukv=mla_w_ukv, mla_q_norm_g=mla_q_norm_g,
             mla_k_norm_g=mla_k_norm_g, router_group_w=router_group_w, router_group_b=router_group_b,
             router_expert_w=router_expert_w, router_expert_b=router_expert_b, expert_w1=expert_w1,
             expert_w3=expert_w3, expert_w2=expert_w2)
    f = _prep_params(p)
    consts = _const_tables()
    nb_p = c_prompt.shape[0]
    mod_all = ada_modulation(jnp.concatenate([c_prompt, c_sample], axis=0), ada_w, ada_b)
    y_prompt = _trunk(x_prompt, mod_all[:, :nb_p], f, consts, _seq_tables(x_prompt.shape[1]))
    y_sample = _trunk(x_sample, mod_all[:, nb_p:], f, consts, _seq_tables(x_sample.shape[1]))
    return (y_prompt, y_sample)
```

```python
import functools
import math

import jax
import jax.numpy as jnp
import numpy as np
from jax import lax
from jax.experimental import pallas as pl
from jax.experimental.pallas import tpu as pltpu
from jax.experimental.pallas import tpu_sc as plsc

F32 = jnp.float32
BF16 = jnp.bfloat16

D_MODEL = 1024
DEPTH = 4
EPS = 1e-6

GLA_HEADS = 4
GLA_DK = 32
GLA_DV = 64
GLA_QK_W = GLA_HEADS * GLA_DK
GLA_W = GLA_HEADS * GLA_DV
GLA_GATE_RANK = 16
GLA_GATE_NORMALIZER = 16.0
GLA_CHUNK = 32
GLA_BLOCK = 128

FNET_GROUPS = 4
FNET_GROUP_DIM = 64
FNET_W = 256

MLA_HEADS = 8
MLA_Q_LORA = 256
MLA_KV_LORA = 128
MLA_NOPE = 64
MLA_ROPE = 32
MLA_V = 64
MLA_QK = 96
MLA_W = 512
ROPE_THETA = 10000.0
HEAD_PAD = 128

N_GROUPS = 4
EXPERTS_PER_GROUP = 8
N_EXPERTS = 32
TOP_K = 2
D_EXPERT = 512
ROUTE_LANES = 128

C_Q, C_K, C_V, C_OG, C_F, C_CQ, C_CKV, C_KR, C_GATE = 0, 128, 256, 512, 768, 1024, 1280, 1408, 1536
P_IN_PAD = 1664

VMEM_LIMIT_BYTES = 56 * 1024 * 1024


def _cparams(sem, vmem=None):
    return pltpu.CompilerParams(dimension_semantics=sem, vmem_limit_bytes=vmem or VMEM_LIMIT_BYTES)


def _silu(x):
    return x * (1.0 / (1.0 + jnp.exp(-x)))


def _log_sigmoid(x):
    return -(jnp.maximum(-x, 0.0) + jnp.log1p(jnp.exp(-jnp.abs(x))))


def _dot(a, b):
    return jnp.dot(a, b, preferred_element_type=F32)


def _dot_nt(a, b):
    return lax.dot_general(a, b, (((1,), (1,)), ((), ())), preferred_element_type=F32)


def _split_bf16(x):
    hi = x.astype(BF16)
    lo = (x - hi.astype(F32)).astype(BF16)
    return hi, lo


def _pack_halves(x):
    n = x.shape[-1] // 2
    lo = lax.shift_right_logical(lax.bitcast_convert_type(x[:, :n], jnp.int32), 16)
    hi = lax.bitcast_convert_type(x[:, n:], jnp.int32) & jnp.int32(-65536)
    return hi | lo


def _unpack_halves(w):
    lo = lax.bitcast_convert_type(lax.shift_left(w, 16), F32)
    hi = lax.bitcast_convert_type(w & jnp.int32(-65536), F32)
    return lo, hi


def _ada_kernel(c_ref, w_ref, b_ref, o_ref):
    c = _silu(c_ref[...]).astype(BF16)
    o_ref[0] = _dot(c, w_ref[0].astype(BF16)) + b_ref[0]


def ada_modulation(c_all, ada_w, ada_b):
    nb = c_all.shape[0]
    tn = 1536
    n = ada_w.shape[-1]
    return pl.pallas_call(
        _ada_kernel,
        grid=(DEPTH, n // tn),
        in_specs=[
            pl.BlockSpec((nb, D_MODEL), lambda l, j: (0, 0)),
            pl.BlockSpec((1, D_MODEL, tn), lambda l, j: (l, 0, j)),
            pl.BlockSpec((1, 1, tn), lambda l, j: (l, 0, j)),
        ],
        out_specs=pl.BlockSpec((1, nb, tn), lambda l, j: (l, 0, j)),
        out_shape=jax.ShapeDtypeStruct((DEPTH, nb, n), F32),
        compiler_params=_cparams(("parallel", "parallel")),
        name="ada_modulation",
    )(c_all, ada_w, ada_b.reshape(DEPTH, 1, n))


def _in_proj_body(x, mod_ref, g_ref, w_ref, dft_ref, up_ref, gb_ref, lg_ref,
                  qkv_ref, og_ref, gate_ref, xcs_ref, cq_ref, ckv_ref, kr_ref):
    sh = mod_ref[0, 0:1, :]
    sc = mod_ref[0, 1:2, :]
    r = lax.rsqrt(jnp.mean(x * x, axis=-1, keepdims=True) + EPS)
    h = (x * r * g_ref[...]) * (1.0 + sc) + sh
    u = _dot(h.astype(BF16), w_ref[...])

    q = u[:, C_Q:C_K] * (GLA_DK ** -0.5)
    qkv_ref[:, 0:128] = q.astype(BF16)
    qkv_ref[:, 128:512] = u[:, C_K:C_OG].astype(BF16)
    og_ref[...] = u[:, C_OG:C_F].astype(BF16)

    ug = u[:, C_GATE:C_GATE + 128].astype(BF16)
    gl = _dot(ug, up_ref[...]) + gb_ref[...]
    gate_ref[...] = _log_sigmoid(gl) * (1.0 / GLA_GATE_NORMALIZER)

    uf = u[:, C_F:C_CQ].astype(BF16)
    xcs_ref[0] = _dot(uf, dft_ref[0]).astype(BF16)
    xcs_ref[1] = _dot(uf, dft_ref[1]).astype(BF16)

    cq = u[:, C_CQ:C_CKV]
    rq = lax.rsqrt(jnp.mean(cq * cq, axis=-1, keepdims=True) + EPS)
    cq_ref[...] = (cq * rq * lg_ref[:, 0:256]).astype(BF16)
    ckv = u[:, C_CKV:C_KR]
    rkv = lax.rsqrt(jnp.mean(ckv * ckv, axis=-1, keepdims=True) + EPS)
    ckv_ref[...] = (ckv * rkv * lg_ref[:, 256:384]).astype(BF16)
    kr_ref[...] = u[:, C_KR:C_GATE].astype(BF16)


def _in_proj_kernel(x_ref, mod_ref, g_ref, w_ref, dft_ref, up_ref, gb_ref, lg_ref,
                    qkv_ref, og_ref, gate_ref, xcs_ref, cq_ref, ckv_ref, kr_ref):
    _in_proj_body(x_ref[0], mod_ref, g_ref, w_ref, dft_ref, up_ref, gb_ref, lg_ref,
                  qkv_ref.at[0], og_ref.at[0], gate_ref.at[0], xcs_ref, cq_ref.at[0], ckv_ref.at[0], kr_ref.at[0])


def _in_proj_moe_kernel(x_ref, ya_ref, yc_ref, route_ref, modp_ref, mod_ref, g_ref, w_ref, dft_ref, up_ref, gb_ref,
                        lg_ref, xo_ref, qkv_ref, og_ref, gate_ref, xcs_ref, cq_ref, ckv_ref, kr_ref):
    x = _moe_residual(x_ref[0], ya_ref[0], yc_ref[0], route_ref[0], modp_ref)
    xo_ref[0] = x
    _in_proj_body(x, mod_ref, g_ref, w_ref, dft_ref, up_ref, gb_ref, lg_ref,
                  qkv_ref.at[0], og_ref.at[0], gate_ref.at[0], xcs_ref, cq_ref.at[0], ckv_ref.at[0], kr_ref.at[0])


def in_proj(x, mod, norm_g, w_in, dft64, gate_up, gate_bias, lora_g, tm, moe=None):
    B, S, D = x.shape
    ns = S // tm
    row = lambda b, s: (b, s, 0)
    const2 = lambda b, s: (0, 0)
    modmap = lambda b, s: (b, 0, 0)
    if moe is None:
        body, lead_in, lead_specs, lead_out_specs, lead_out_shapes = _in_proj_kernel, (x,), [pl.BlockSpec((1, tm, D), row)], (), ()
    else:
        ya, yc, route, mod_prev = moe
        body = _in_proj_moe_kernel
        lead_in = (x, ya, yc, route, mod_prev)
        lead_specs = [pl.BlockSpec((1, tm, D), row), pl.BlockSpec((1, tm, D // 2), row),
                      pl.BlockSpec((1, tm, D // 2), row), pl.BlockSpec((1, tm, ROUTE_LANES), row),
                      pl.BlockSpec((1, 6, D), modmap)]
        lead_out_specs = (pl.BlockSpec((1, tm, D), row),)
        lead_out_shapes = (jax.ShapeDtypeStruct((B, S, D), F32),)
    out_shapes = lead_out_shapes + (
        jax.ShapeDtypeStruct((B, S, 512), BF16),
        jax.ShapeDtypeStruct((B, S, GLA_W), BF16),
        jax.ShapeDtypeStruct((B, S, 256), F32),
        jax.ShapeDtypeStruct((2, S, B * FNET_W), BF16),
        jax.ShapeDtypeStruct((B, S, MLA_Q_LORA), BF16),
        jax.ShapeDtypeStruct((B, S, MLA_KV_LORA), BF16),
        jax.ShapeDtypeStruct((B, S, HEAD_PAD), BF16),
    )
    return pl.pallas_call(
        body,
        grid=(B, ns),
        in_specs=lead_specs + [
            pl.BlockSpec((1, 6, D), modmap),
            pl.BlockSpec((1, D), const2),
            pl.BlockSpec((D, P_IN_PAD), const2),
            pl.BlockSpec((2, FNET_W, FNET_W), lambda b, s: (0, 0, 0)),
            pl.BlockSpec((128, 256), const2),
            pl.BlockSpec((1, 256), const2),
            pl.BlockSpec((1, 384), const2),
        ],
        out_specs=lead_out_specs + (
            pl.BlockSpec((1, tm, 512), row),
            pl.BlockSpec((1, tm, GLA_W), row),
            pl.BlockSpec((1, tm, 256), row),
            pl.BlockSpec((2, tm, FNET_W), lambda b, s: (0, s, b)),
            pl.BlockSpec((1, tm, MLA_Q_LORA), row),
            pl.BlockSpec((1, tm, MLA_KV_LORA), row),
            pl.BlockSpec((1, tm, HEAD_PAD), row),
        ),
        out_shape=out_shapes,
        compiler_params=_cparams(("parallel", "parallel")),
        name="in_proj",
    )(*lead_in, mod, norm_g, w_in, dft64, gate_up, gate_bias, lora_g)


def _gla_kernel(qkv_ref, og_ref, gate_ref, ng_ref, o_ref, o_s, st_s, *, seq):
    R, C = GLA_BLOCK, GLA_CHUNK
    n_sub = R // C
    nblk = seq // R
    shift_c = int(math.log2(C))

    row = lax.broadcasted_iota(jnp.int32, (R, R), 0)
    col = lax.broadcasted_iota(jnp.int32, (R, R), 1)
    same = (row >> shift_c) == (col >> shift_c)
    ones_blk = jnp.where(same, 1.0, 0.0)
    lane_qk = lax.broadcasted_iota(jnp.int32, (1, GLA_QK_W), 1)
    lane_v = lax.broadcasted_iota(jnp.int32, (1, GLA_W), 1)
    head_qk = [(lane_qk >> 5) == h for h in range(GLA_HEADS)]
    head_v = [(lane_v >> 6) == h for h in range(GLA_HEADS)]
    st_row = lax.broadcasted_iota(jnp.int32, (GLA_W, GLA_QK_W), 0)
    st_col = lax.broadcasted_iota(jnp.int32, (GLA_W, GLA_QK_W), 1)
    st_mask = (st_row >> 6) == (st_col >> 5)
    sub_row = lax.broadcasted_iota(jnp.int32, (R, 1), 0) >> shift_c
    nrow = lax.broadcasted_iota(jnp.int32, (GLA_W, GLA_W), 0)
    ncol = lax.broadcasted_iota(jnp.int32, (GLA_W, GLA_W), 1)
    head_mean = jnp.where((nrow >> 6) == (ncol >> 6), 1.0 / GLA_DV, 0.0).astype(BF16)

    def direction(backward):
        if backward:
            tri = same & (col >= row)
            att_ok = same & (col > row)
            order = list(range(n_sub - 1, -1, -1))
        else:
            tri = same & (col <= row)
            att_ok = same & (col <= row)
            order = list(range(n_sub))
        cum_lhs = jnp.concatenate([jnp.where(tri, 1.0, 0.0), ones_blk], axis=0).astype(BF16)
        att_ok4 = jnp.concatenate([att_ok] * GLA_HEADS, axis=1)
        return cum_lhs, att_ok4, order, (GLA_QK_W if backward else 0), (1 if backward else 0)

    def block(j, consts):
        cum_lhs, att_ok4, order, gate_off, slot = consts
        r0 = pl.multiple_of(j * R, R)
        qkv = qkv_ref[0, pl.ds(r0, R), :]
        q = qkv[:, 0:128].astype(F32)
        k = qkv[:, 128:256].astype(F32)
        v = qkv[:, 256:512]
        g = gate_ref[0, pl.ds(r0, R), gate_off:gate_off + GLA_QK_W]
        g_hi, g_lo = _split_bf16(g)
        cs = _dot(cum_lhs, jnp.concatenate([g_hi, g_lo], axis=1))
        b = cs[0:R, 0:128] + cs[0:R, 128:256]
        bl = cs[R:2 * R, 0:128] + cs[R:2 * R, 128:256]
        q_dec = q * jnp.exp(b)
        k_inv = (k * jnp.exp(-b)).astype(BF16)
        k_end = k * jnp.exp(bl - b)
        decay = jnp.exp(bl)

        att = jnp.concatenate(
            [_dot_nt(jnp.where(head_qk[h], q_dec, 0.0).astype(BF16), k_inv) for h in range(GLA_HEADS)],
            axis=1)
        att = jnp.where(att_ok4, att, 0.0).astype(BF16)
        v_heads = jnp.concatenate([jnp.where(head_v[h], v, jnp.zeros_like(v)) for h in range(GLA_HEADS)], axis=0)
        o_blk = _dot(att, v_heads)

        v_t = v.astype(F32).T.astype(BF16)
        q_dec_b = q_dec.astype(BF16)
        inter = [None] * n_sub
        for c in order:
            st = st_s[slot]
            inter[c] = _dot_nt(q_dec_b[c * C:(c + 1) * C, :], st.astype(BF16))
            k_c = jnp.where(sub_row == c, k_end, 0.0).astype(BF16)
            d_st = _dot(v_t, k_c)
            st_s[slot] = st * decay[c * C:c * C + 1, :] + jnp.where(st_mask, d_st, 0.0)
        o_s[slot, pl.ds(r0, R), :] = o_blk + jnp.concatenate(inter, axis=0)

    fwd = direction(False)
    bwd = direction(True)
    st_s[...] = jnp.zeros_like(st_s)

    def scan_step(i, carry):
        block(i, fwd)
        block(nblk - 1 - i, bwd)
        return carry

    lax.fori_loop(0, nblk, scan_step, 0)

    def finish(j, carry):
        r0 = pl.multiple_of(j * R, R)
        o = o_s[0, pl.ds(r0, R), :] + o_s[1, pl.ds(r0, R), :]
        sq_hi, sq_lo = _split_bf16(o * o)
        ms = _dot(sq_hi, head_mean) + _dot(sq_lo, head_mean)
        y = o * lax.rsqrt(ms + EPS) * ng_ref[...]
        og = og_ref[0, pl.ds(r0, R), :].astype(F32)
        o_ref[0, pl.ds(r0, R), :] = (y * _silu(og)).astype(o_ref.dtype)
        return carry

    lax.fori_loop(0, nblk, finish, 0)


def gla(qkv, og, gate, norm_g4):
    B, S, _ = qkv.shape
    full = lambda b: (b, 0, 0)
    return pl.pallas_call(
        functools.partial(_gla_kernel, seq=S),
        grid=(B,),
        in_specs=[
            pl.BlockSpec((1, S, 512), full),
            pl.BlockSpec((1, S, GLA_W), full),
            pl.BlockSpec((1, S, 256), full),
            pl.BlockSpec((1, GLA_W), lambda b: (0, 0)),
        ],
        out_specs=pl.BlockSpec((1, S, GLA_W), full),
        out_shape=jax.ShapeDtypeStruct((B, S, GLA_W), BF16),
        scratch_shapes=[pltpu.VMEM((2, S, GLA_W), F32), pltpu.VMEM((2, GLA_W, GLA_QK_W), F32)],
        compiler_params=_cparams(("parallel",)),
        name="gla",
    )(qkv, og, gate, norm_g4)


def _dft_kernel(a_ref, b_ref, o_ref, acc_ref, *, scale):
    kk = pl.program_id(2)

    @pl.when(kk == 0)
    def _():
        acc_ref[...] = jnp.zeros_like(acc_ref)

    acc_ref[...] += _dot(a_ref[...], b_ref[...])

    @pl.when(kk == pl.num_programs(2) - 1)
    def _():
        o_ref[...] = (acc_ref[...] * scale).astype(o_ref.dtype)


def dft_seq(a_dft, xcs, scale):
    m, k = a_dft.shape
    n = xcs.shape[1]
    tm, tn, tk = min(m, 1024), min(n, 1024), min(k, 2048)
    return pl.pallas_call(
        functools.partial(_dft_kernel, scale=scale),
        grid=(m // tm, n // tn, k // tk),
        in_specs=[pl.BlockSpec((tm, tk), lambda i, j, kk: (i, kk)),
                  pl.BlockSpec((tk, tn), lambda i, j, kk: (kk, j))],
        out_specs=pl.BlockSpec((tm, tn), lambda i, j, kk: (i, j)),
        out_shape=jax.ShapeDtypeStruct((m, n), BF16),
        scratch_shapes=[pltpu.VMEM((tm, tn), F32)],
        compiler_params=_cparams(("parallel", "parallel", "arbitrary")),
        name="dft_seq",
    )(a_dft, xcs)


def _rope(x, cos, sin, swap):
    return x * cos + _dot(x.astype(BF16), swap) * sin


SHIFT_LANE = HEAD_PAD - 1
MAX_STATIC_SHIFT = 40.0
LOG2E = math.log2(math.e)


def _mla_kernel(flag_ref, cq_ref, ckv_ref, kr_ref, rope_ref, wq_ref, wk_ref, wv_ref, qg_ref, kg_ref, vone_ref,
                shift_ref, o_ref, kt_s, v_s, q_s, acc_s, *, seq, tq, tk):
    qi = pl.program_id(1)
    inv_qk = 1.0 / MLA_QK
    lane = lax.broadcasted_iota(jnp.int32, (1, HEAD_PAD), 1)
    src = lax.broadcasted_iota(jnp.int32, (HEAD_PAD, HEAD_PAD), 0)
    dst = lax.broadcasted_iota(jnp.int32, (HEAD_PAD, HEAD_PAD), 1)
    half = MLA_ROPE // 2
    lo_half = (dst >= MLA_NOPE) & (dst < MLA_NOPE + half)
    hi_half = (dst >= MLA_NOPE + half) & (dst < MLA_QK)
    swap = jnp.where((lo_half & (src == dst + half)) | (hi_half & (src == dst - half)), 1.0, 0.0).astype(BF16)

    @pl.when(qi == 0)
    def _():
        one_lane = jnp.where(lane == SHIFT_LANE, 1.0, 0.0)

        def rows(i, carry):
            r0 = pl.multiple_of(i * tk, tk)
            ckv = ckv_ref[0, pl.ds(r0, tk), :]
            kr = kr_ref[0, pl.ds(r0, tk), :].astype(F32)
            cos = rope_ref[0, pl.ds(r0, tk), :]
            sin = rope_ref[1, pl.ds(r0, tk), :]
            for h in range(MLA_HEADS):
                kp = _dot(ckv, wk_ref[h]) + kr
                r = lax.rsqrt(jnp.sum(kp * kp, axis=-1, keepdims=True) * inv_qk + EPS)
                kn = _rope(kp * r * kg_ref[...], cos, sin, swap) + one_lane
                kt_s[h, i] = kn.T.astype(BF16)
                v_s[h, pl.ds(r0, tk), :] = (_dot(ckv, wv_ref[h]) + vone_ref[h]).astype(BF16)
            return carry

        lax.fori_loop(0, seq // tk, rows, 0)

    q0 = pl.multiple_of(qi * tq, tq)
    cos = rope_ref[0, pl.ds(q0, tq), :]
    sin = rope_ref[1, pl.ds(q0, tq), :]
    for h in range(MLA_HEADS):
        qp = _dot(cq_ref[0], wq_ref[h])
        r = lax.rsqrt(jnp.sum(qp * qp, axis=-1, keepdims=True) * inv_qk + EPS)
        q_s[h] = (_rope(qp * r * qg_ref[...], cos, sin, swap) + shift_ref[...]).astype(BF16)

    def finish():
        for pair in range(MLA_HEADS // 2):
            outs = []
            for h in (2 * pair, 2 * pair + 1):
                acc = acc_s[h]
                den_lane = MLA_V if h % 2 == 0 else 0
                den = jnp.sum(jnp.where(lane == den_lane, acc, 0.0), axis=-1, keepdims=True)
                outs.append(acc * (1.0 / den))
            both = jnp.where(lane < MLA_V, outs[0], outs[1])
            o_ref[0, :, pair * HEAD_PAD:(pair + 1) * HEAD_PAD] = both.astype(o_ref.dtype)

    @pl.when(flag_ref[0] == 1)
    def _():
        acc_s[...] = jnp.zeros_like(acc_s)

        def kv_step(j, carry):
            k0 = pl.multiple_of(j * tk, tk)
            for h in range(MLA_HEADS):
                p = jnp.exp2(_dot(q_s[h], kt_s[h, j])).astype(BF16)
                acc_s[h] += _dot(p, v_s[h, pl.ds(k0, tk), :])
            return carry

        lax.fori_loop(0, seq // tk, kv_step, 0)
        finish()

    @pl.when(flag_ref[0] == 0)
    def _():
        for h in range(MLA_HEADS):
            def kv_step(j, carry, h=h):
                m, acc = carry
                k0 = pl.multiple_of(j * tk, tk)
                s = _dot(q_s[h], kt_s[h, j])
                m_new = jnp.maximum(m, jnp.max(s, axis=-1, keepdims=True))
                p = jnp.exp2(s - m_new).astype(BF16)
                return m_new, jnp.exp2(m - m_new) * acc + _dot(p, v_s[h, pl.ds(k0, tk), :])

            m0 = jnp.full((tq, 1), -jnp.inf, F32)
            _, acc = lax.fori_loop(0, seq // tk, kv_step, (m0, jnp.zeros((tq, HEAD_PAD), F32)))
            acc_s[h] = acc
        finish()


def mla(flag, cq, ckv, kr, rope, wq, wk, wv, qg, kg, vone, shift, tq, tk):
    B, S, _ = cq.shape
    c3 = lambda b, q: (0, 0, 0)
    c2 = lambda b, q: (0, 0)
    return pl.pallas_call(
        functools.partial(_mla_kernel, seq=S, tq=tq, tk=tk),
        grid=(B, S // tq),
        in_specs=[
            pl.BlockSpec(memory_space=pltpu.SMEM),
            pl.BlockSpec((1, tq, MLA_Q_LORA), lambda b, q: (b, q, 0)),
            pl.BlockSpec((1, S, MLA_KV_LORA), lambda b, q: (b, 0, 0)),
            pl.BlockSpec((1, S, HEAD_PAD), lambda b, q: (b, 0, 0)),
            pl.BlockSpec((2, S, HEAD_PAD), c3),
            pl.BlockSpec((MLA_HEADS, MLA_Q_LORA, HEAD_PAD), c3),
            pl.BlockSpec((MLA_HEADS, MLA_KV_LORA, HEAD_PAD), c3),
            pl.BlockSpec((MLA_HEADS, MLA_KV_LORA, HEAD_PAD), c3),
            pl.BlockSpec((1, HEAD_PAD), c2),
            pl.BlockSpec((1, HEAD_PAD), c2),
            pl.BlockSpec((MLA_HEADS, 1, HEAD_PAD), c3),
            pl.BlockSpec((1, HEAD_PAD), c2),
        ],
        out_specs=pl.BlockSpec((1, tq, MLA_W), lambda b, q: (b, q, 0)),
        out_shape=jax.ShapeDtypeStruct((B, S, MLA_W), BF16),
        scratch_shapes=[pltpu.VMEM((MLA_HEADS, S // tk, HEAD_PAD, tk), BF16),
                        pltpu.VMEM((MLA_HEADS, S, HEAD_PAD), BF16),
                        pltpu.VMEM((MLA_HEADS, tq, HEAD_PAD), BF16),
                        pltpu.VMEM((MLA_HEADS, tq, HEAD_PAD), F32)],
        compiler_params=_cparams(("parallel", "arbitrary")),
        name="mla",
    )(flag, cq, ckv, kr, rope, wq, wk, wv, qg, kg, vone, shift)


def _out_proj_kernel(x_ref, gla_ref, fft_ref, mla_ref, mod_ref, w_ref, g_ref, rw_ref, rb_ref,
                     xo_ref, h_ref, route_ref, cnt_ref):
    mix = (_dot(gla_ref[0], w_ref[0:256, :]) + _dot(fft_ref[...], w_ref[256:512, :])
           + _dot(mla_ref[0], w_ref[512:1024, :]))
    g1 = mod_ref[0, 2:3, :]
    sh = mod_ref[0, 3:4, :]
    sc = mod_ref[0, 4:5, :]
    x = x_ref[0] + g1 * mix
    xo_ref[0] = x
    r = lax.rsqrt(jnp.mean(x * x, axis=-1, keepdims=True) + EPS)
    h = (x * r * g_ref[...]) * (1.0 + sc) + sh
    h_hi, h_lo = _split_bf16(h)
    h_ref[0] = _pack_halves(h_hi.astype(F32))

    logit = _dot(h_hi, rw_ref[0]) + _dot(h_lo, rw_ref[0]) + _dot(h_hi, rw_ref[1]) + rb_ref[...]
    lane = lax.broadcasted_iota(jnp.int32, (1, ROUTE_LANES), 1)
    lane_f = lane.astype(F32)
    neg = -1e30
    is_g = lane < N_GROUPS
    is_e = (lane >= N_GROUPS) & (lane < N_GROUPS + N_EXPERTS)
    lg = jnp.where(is_g, logit, neg)
    g_max = jnp.max(lg, axis=-1, keepdims=True)
    g_den = jnp.sum(jnp.where(is_g, jnp.exp(lg - g_max), 0.0), axis=-1, keepdims=True)
    g_w = 1.0 / g_den
    g_top = jnp.min(jnp.where(is_g & (logit == g_max), lane_f, 1e9), axis=-1, keepdims=True)
    e_grp = ((lane - N_GROUPS) >> 3).astype(F32)
    in_grp = is_e & (e_grp == g_top)
    le = jnp.where(in_grp, logit, neg)
    t1 = jnp.max(le, axis=-1, keepdims=True)
    i1 = jnp.min(jnp.where(in_grp & (le == t1), lane_f, 1e9), axis=-1, keepdims=True)
    le2 = jnp.where(lane_f == i1, neg, le)
    t2 = jnp.max(le2, axis=-1, keepdims=True)
    i2 = jnp.min(jnp.where(in_grp & (le2 == t2), lane_f, 1e9), axis=-1, keepdims=True)
    e21 = jnp.exp(t2 - t1)
    w1 = g_w / (1.0 + e21)
    w2 = w1 * e21
    route = jnp.where(lane == 0, i1 - N_GROUPS,
                      jnp.where(lane == 1, i2 - N_GROUPS,
                                jnp.where(lane == 2, w1, jnp.where(lane == 3, w2, 0.0))))
    route_ref[0] = route

    @pl.when((pl.program_id(0) == 0) & (pl.program_id(1) == 0))
    def _():
        cnt_ref[...] = jnp.zeros_like(cnt_ref)

    picked = jnp.where((lane_f == i1 - N_GROUPS) | (lane_f == i2 - N_GROUPS), 1.0, 0.0)
    cnt_ref[...] += jnp.sum(picked, axis=0, keepdims=True)


def out_proj(x, o_gla, o_fft, o_mla, mod, w_out, norm_g, rw, rb, tm):
    B, S, D = x.shape
    row = lambda b, s: (b, s, 0)
    c2 = lambda b, s: (0, 0)
    return pl.pallas_call(
        _out_proj_kernel,
        grid=(B, S // tm),
        in_specs=[
            pl.BlockSpec((1, tm, D), row),
            pl.BlockSpec((1, tm, GLA_W), row),
            pl.BlockSpec((tm, FNET_W), lambda b, s: (s, b)),
            pl.BlockSpec((1, tm, MLA_W), row),
            pl.BlockSpec((1, 6, D), lambda b, s: (b, 0, 0)),
            pl.BlockSpec((D, D), c2),
            pl.BlockSpec((1, D), c2),
            pl.BlockSpec((2, D, ROUTE_LANES), lambda b, s: (0, 0, 0)),
            pl.BlockSpec((1, ROUTE_LANES), c2),
        ],
        out_specs=(pl.BlockSpec((1, tm, D), row), pl.BlockSpec((1, tm, D // 2), row),
                   pl.BlockSpec((1, tm, ROUTE_LANES), row), pl.BlockSpec((1, ROUTE_LANES), c2)),
        out_shape=(jax.ShapeDtypeStruct((B, S, D), F32), jax.ShapeDtypeStruct((B, S, D // 2), jnp.int32),
                   jax.ShapeDtypeStruct((B, S, ROUTE_LANES), F32), jax.ShapeDtypeStruct((1, ROUTE_LANES), F32)),
        compiler_params=_cparams(("arbitrary", "arbitrary")),
        name="out_proj",
    )(x, o_gla, o_fft, o_mla, mod, w_out, norm_g, rw, rb)


RANK_ROWS = 512


def _rank_kernel(route_ref, start_ref, dest_ref, carry_ref, before_ref):
    n = route_ref.shape[0]

    @pl.when(pl.program_id(0) == 0)
    def _():
        carry_ref[...] = jnp.zeros_like(carry_ref)
        row = lax.broadcasted_iota(jnp.int32, (n, n), 0)
        col = lax.broadcasted_iota(jnp.int32, (n, n), 1)
        before_ref[...] = jnp.where(col < row, 1.0, 0.0).astype(BF16)

    lane_f = lax.broadcasted_iota(jnp.int32, (1, ROUTE_LANES), 1).astype(F32)
    e1 = route_ref[:, 0:1]
    e2 = route_ref[:, 1:2]
    oh1 = jnp.where(lane_f == e1, 1.0, 0.0)
    oh2 = jnp.where(lane_f == e2, 1.0, 0.0)
    both = (oh1 + oh2).astype(BF16)
    pos = _dot(before_ref[...], both) + carry_ref[...] + start_ref[...]
    d1 = jnp.sum(oh1 * pos, axis=-1, keepdims=True)
    d2 = jnp.sum(oh2 * pos, axis=-1, keepdims=True)
    lane = lax.broadcasted_iota(jnp.int32, (1, ROUTE_LANES), 1)
    cols = jnp.where(lane == 0, d1, jnp.where(lane == 1, d2, 0.0))
    dest_ref[...] = cols.T[0:TOP_K, :].astype(jnp.int32)
    carry_ref[...] += jnp.sum(oh1 + oh2, axis=0, keepdims=True)


def route_rank(route2d, start):
    T = route2d.shape[0]
    rr = min(T, RANK_ROWS)
    return pl.pallas_call(
        _rank_kernel,
        grid=(T // rr,),
        in_specs=[pl.BlockSpec((rr, ROUTE_LANES), lambda i: (i, 0)),
                  pl.BlockSpec((1, ROUTE_LANES), lambda i: (0, 0))],
        out_specs=pl.BlockSpec((TOP_K, rr), lambda i: (0, i)),
        out_shape=jax.ShapeDtypeStruct((TOP_K, T), jnp.int32),
        scratch_shapes=[pltpu.VMEM((1, ROUTE_LANES), F32), pltpu.VMEM((rr, rr), BF16)],
        compiler_params=_cparams(("arbitrary",)),
        name="route_rank",
    )(route2d, start)


MOE_ROWS = 512


def _moe_kernel(blk_e_ref, n_used_ref, x_ref, w1_ref, w3_ref, w2_ref, o_ref):
    i = pl.program_id(0)

    @pl.when(i < n_used_ref[0])
    def _():
        half = D_MODEL // 2
        x_lo, x_hi = _unpack_halves(x_ref[...])
        x_lo = x_lo.astype(BF16)
        x_hi = x_hi.astype(BF16)
        w1 = w1_ref[0, 0].astype(BF16)
        w3 = w3_ref[0, 0].astype(BF16)
        a = _dot(x_lo, w1[0:half, :]) + _dot(x_hi, w1[half:, :])
        b = _dot(x_lo, w3[0:half, :]) + _dot(x_hi, w3[half:, :])
        hm = (_silu(a) * b).astype(BF16)
        y = _dot(hm, w2_ref[0, 0].astype(BF16))
        o_ref[...] = _pack_halves(y.astype(BF16).astype(F32))

    @pl.when(i >= n_used_ref[0])
    def _():
        o_ref[...] = jnp.zeros_like(o_ref)


def moe_mlp(xb, blk_e, n_used, w1, w3, w2, layer):
    P, half = xb.shape
    n_blk = P // MOE_ROWS
    wmap = lambda i, be, nu: (layer, be[i], 0, 0)
    grid_spec = pltpu.PrefetchScalarGridSpec(
        num_scalar_prefetch=2,
        grid=(n_blk,),
        in_specs=[
            pl.BlockSpec((MOE_ROWS, half), lambda i, be, nu: (i, 0)),
            pl.BlockSpec((1, 1, D_MODEL, D_EXPERT), wmap),
            pl.BlockSpec((1, 1, D_MODEL, D_EXPERT), wmap),
            pl.BlockSpec((1, 1, D_EXPERT, D_MODEL), wmap),
        ],
        out_specs=pl.BlockSpec((MOE_ROWS, half), lambda i, be, nu: (i, 0)),
    )
    return pl.pallas_call(
        _moe_kernel,
        grid_spec=grid_spec,
        out_shape=jax.ShapeDtypeStruct((P, half), jnp.int32),
        compiler_params=_cparams(("arbitrary",)),
        name="moe_mlp",
    )(blk_e, n_used, xb, w1, w3, w2)


SC_WINDOW = 128


def sc_gather_rows(table, idx):
    n = idx.shape[0]
    width = table.shape[1]
    mesh = plsc.VectorSubcoreMesh(core_axis_name="core", subcore_axis_name="subcore")

    @functools.partial(pl.kernel, out_type=jax.ShapeDtypeStruct((n, width), table.dtype), mesh=mesh,
                       name="sc_gather_rows")
    def gather_kernel(x_hbm, i_hbm, o_hbm):
        def body(i_vmem, o_vmem):
            pltpu.sync_copy(x_hbm.at[i_vmem.at[0]], o_vmem)

        pltpu.emit_pipeline(
            body,
            grid=(n // SC_WINDOW,),
            in_specs=[pl.BlockSpec((1, SC_WINDOW), lambda i: (0, i))],
            out_specs=[pl.BlockSpec((SC_WINDOW, width), lambda i: (i, 0), pipeline_mode=pl.Buffered(1))],
            core_axis_name=("core", "subcore"),
            dimension_semantics=(pltpu.PARALLEL,),
        )(i_hbm, o_hbm)

    return gather_kernel(table, idx.reshape(1, n))


def sc_scatter_rows(rows, idx, n_out):
    n = idx.shape[0]
    t, width = rows.shape
    n_src = t // SC_WINDOW
    mesh = plsc.VectorSubcoreMesh(core_axis_name="core", subcore_axis_name="subcore")

    @functools.partial(pl.kernel, out_type=jax.ShapeDtypeStruct((n_out, width), rows.dtype), mesh=mesh,
                       name="sc_scatter_rows")
    def scatter_kernel(x_hbm, i_hbm, o_hbm):
        def body(x_vmem, i_vmem):
            pltpu.sync_copy(x_vmem, o_hbm.at[i_vmem.at[0]])

        pltpu.emit_pipeline(
            body,
            grid=(n // SC_WINDOW,),
            in_specs=[pl.BlockSpec((SC_WINDOW, width), lambda i: (i % n_src, 0), pipeline_mode=pl.Buffered(1)),
                      pl.BlockSpec((1, SC_WINDOW), lambda i: (0, i))],
            out_specs=[],
            core_axis_name=("core", "subcore"),
            dimension_semantics=(pltpu.PARALLEL,),
        )(x_hbm, i_hbm)

    return scatter_kernel(rows, idx.reshape(1, n))


def _moe_residual(x, ya, yc, route, modp_ref):
    g2 = modp_ref[0, 5:6, :]
    wa = route[:, 2:3]
    wb = route[:, 3:4]
    a_lo, a_hi = _unpack_halves(ya)
    b_lo, b_hi = _unpack_halves(yc)
    y = jnp.concatenate([wa * a_lo + wb * b_lo, wa * a_hi + wb * b_hi], axis=1)
    return x + g2 * y


def _combine_kernel(x_ref, ya_ref, yc_ref, route_ref, mod_ref, o_ref):
    o_ref[0] = _moe_residual(x_ref[0], ya_ref[0], yc_ref[0], route_ref[0], mod_ref)


def combine(x, ya, yc, route, mod, tm):
    B, S, D = x.shape
    row = lambda b, s: (b, s, 0)
    return pl.pallas_call(
        _combine_kernel,
        grid=(B, S // tm),
        in_specs=[
            pl.BlockSpec((1, tm, D), row),
            pl.BlockSpec((1, tm, D // 2), row),
            pl.BlockSpec((1, tm, D // 2), row),
            pl.BlockSpec((1, tm, ROUTE_LANES), row),
            pl.BlockSpec((1, 6, D), lambda b, s: (b, 0, 0)),
        ],
        out_specs=pl.BlockSpec((1, tm, D), row),
        out_shape=jax.ShapeDtypeStruct((B, S, D), F32),
        compiler_params=_cparams(("parallel", "parallel")),
        name="combine",
    )(x, ya, yc, route, mod)


def _prep_params(p):
    f = {}
    w_in = p["w_in"]
    L = w_in.shape[0]
    z = lambda n: jnp.zeros((L, D_MODEL, n), F32)
    f["w_in"] = jnp.concatenate(
        [w_in[:, :, 0:768], w_in[:, :, 800:1056], w_in[:, :, 1056:1312], w_in[:, :, 1312:1440],
         z(MLA_NOPE), w_in[:, :, 1440:1472], z(HEAD_PAD - MLA_QK),
         w_in[:, :, 768:800], z(128 - 2 * GLA_GATE_RANK)], axis=-1).astype(BF16)
    up = jnp.zeros((L, 128, 256), F32)
    up = up.at[:, 0:16, 0:128].set(p["gla_gate_up_f"]).at[:, 16:32, 128:256].set(p["gla_gate_up_b"])
    f["gate_up"] = up.astype(BF16)
    f["gate_bias"] = jnp.concatenate([p["gla_gate_bias_f"], p["gla_gate_bias_b"]], axis=-1)[:, None, :]
    f["lora_g"] = jnp.concatenate([p["mla_q_lora_norm_g"], p["mla_kv_lora_norm_g"]], axis=-1)[:, None, :]
    f["gla_norm_g"] = jnp.tile(p["gla_out_norm_g"], (1, GLA_HEADS))[:, None, :]
    f["norm1_g"] = p["norm1_g"][:, None, :]
    f["norm2_g"] = p["norm2_g"][:, None, :]
    wq = p["mla_w_uq"].reshape(L, MLA_Q_LORA, MLA_HEADS, MLA_QK).transpose(0, 2, 1, 3)
    f["wq"] = jnp.pad(wq, ((0, 0), (0, 0), (0, 0), (0, HEAD_PAD - MLA_QK))).astype(BF16)
    wkv = p["mla_w_ukv"].reshape(L, MLA_KV_LORA, MLA_HEADS, MLA_NOPE + MLA_V).transpose(0, 2, 1, 3)
    f["wk"] = jnp.pad(wkv[..., :MLA_NOPE], ((0, 0), (0, 0), (0, 0), (0, HEAD_PAD - MLA_NOPE))).astype(BF16)
    wv = wkv[..., MLA_NOPE:]
    zv = jnp.zeros_like(wv)
    even = (jnp.arange(MLA_HEADS) % 2 == 0)[None, :, None, None]
    f["wv"] = jnp.where(even, jnp.concatenate([wv, zv], -1), jnp.concatenate([zv, wv], -1)).astype(BF16)
    pad_qk = ((0, 0), (0, HEAD_PAD - MLA_QK))
    f["qg"] = (jnp.pad(p["mla_q_norm_g"], pad_qk) * (MLA_QK ** -0.5 * LOG2E))[:, None, :]
    f["kg"] = jnp.pad(p["mla_k_norm_g"], pad_qk)[:, None, :]
    bound = (jnp.max(jnp.abs(p["mla_q_norm_g"]), axis=-1) * jnp.max(jnp.abs(p["mla_k_norm_g"]), axis=-1)
             * (math.sqrt(MLA_QK) * 1.01 * LOG2E) + 0.1).astype(BF16).astype(F32)
    use_bound = bound <= MAX_STATIC_SHIFT * LOG2E
    f["mla_flag"] = use_bound.astype(jnp.int32)[:, None]
    f["mla_shift"] = jnp.zeros((L, 1, HEAD_PAD), F32).at[:, 0, SHIFT_LANE].set(jnp.where(use_bound, -bound, 0.0))
    f["w_out"] = p["w_out"].astype(BF16)
    rw = jnp.concatenate([p["router_group_w"], p["router_expert_w"]], axis=-1)
    rw = jnp.pad(rw, ((0, 0), (0, 0), (0, ROUTE_LANES - N_GROUPS - N_EXPERTS)))
    rw_hi = rw.astype(BF16)
    rw_lo = (rw - rw_hi.astype(F32)).astype(BF16)
    f["rw"] = jnp.stack([rw_hi, rw_lo], axis=1)
    rb = jnp.concatenate([p["router_group_b"], p["router_expert_b"]], axis=-1)
    f["rb"] = jnp.pad(rb, ((0, 0), (0, ROUTE_LANES - N_GROUPS - N_EXPERTS)))[:, None, :]
    f["w1"] = p["expert_w1"]
    f["w3"] = p["expert_w3"]
    f["w2"] = p["expert_w2"]
    return f


def _const_tables():
    c = np.arange(FNET_W)
    same = (c[:, None] // FNET_GROUP_DIM) == (c[None, :] // FNET_GROUP_DIM)
    ang = 2.0 * np.pi * ((c[:, None] % FNET_GROUP_DIM) * (c[None, :] % FNET_GROUP_DIM) % FNET_GROUP_DIM) / FNET_GROUP_DIM
    dft64 = np.stack([np.where(same, np.cos(ang), 0.0), np.where(same, np.sin(ang), 0.0)]).astype(np.float32)
    vone = np.zeros((MLA_HEADS, 1, HEAD_PAD), np.float32)
    vone[0::2, 0, MLA_V] = 1.0
    vone[1::2, 0, 0] = 1.0
    return jnp.asarray(dft64, BF16), jnp.asarray(vone)


def _seq_tables(S):
    j = lax.broadcasted_iota(jnp.int32, (S, S), 0)
    k = lax.broadcasted_iota(jnp.int32, (S, S), 1)
    ang = ((j * k) % S).astype(F32) * (2.0 * math.pi / S)
    a_dft = jnp.concatenate([jnp.cos(ang), -jnp.sin(ang)], axis=1).astype(BF16)
    half = MLA_ROPE // 2
    freqs = ROPE_THETA ** (-jnp.arange(half, dtype=F32) / half)
    ra = jnp.arange(S, dtype=F32)[:, None] * freqs[None, :]
    cos, sin = jnp.cos(ra), jnp.sin(ra)
    one = lambda n: jnp.ones((S, n), F32)
    zero = lambda n: jnp.zeros((S, n), F32)
    tail = HEAD_PAD - MLA_QK
    rope = jnp.stack([
        jnp.concatenate([one(MLA_NOPE), cos, cos, one(tail)], axis=1),
        jnp.concatenate([zero(MLA_NOPE), -sin, sin, zero(tail)], axis=1),
    ])
    return a_dft, rope


def _dispatch_plan(route2d, counts):
    T = route2d.shape[0]
    R = T * TOP_K
    cnt = counts[0, :N_EXPERTS].astype(jnp.int32)
    padded = (cnt + MOE_ROWS - 1) // MOE_ROWS * MOE_ROWS
    pend = jnp.cumsum(padded)
    pstart = pend - padded
    start = jnp.zeros((1, ROUTE_LANES), F32).at[0, :N_EXPERTS].set(pstart.astype(F32))
    dest = route_rank(route2d, start)
    P = R + N_EXPERTS * MOE_ROWS
    n_blk = P // MOE_ROWS
    blk_first = jnp.arange(n_blk, dtype=jnp.int32) * MOE_ROWS
    blk_e = jnp.minimum(jnp.sum((pend[None, :] <= blk_first[:, None]).astype(jnp.int32), axis=1), N_EXPERTS - 1)
    n_used = (pend[-1] // MOE_ROWS).astype(jnp.int32).reshape(1)
    return dest, blk_e, n_used, P


def _trunk(x, mod_all, f, consts, tables):
    B, S, D = x.shape
    T = B * S
    dft64, vone = consts
    a_dft, rope = tables
    tm = min(S, 512)
    tq = min(S, 512)
    tk = min(S, 512)
    dft_scale = 1.0 / math.sqrt(FNET_GROUP_DIM * S)
    moe = None
    for l in range(DEPTH):
        mod = mod_all[l].reshape(B, 6, D)
        outs = in_proj(x, mod, f["norm1_g"][l], f["w_in"][l], dft64, f["gate_up"][l], f["gate_bias"][l],
                       f["lora_g"][l], tm, moe)
        if moe is not None:
            x, outs = outs[0], outs[1:]
        qkv, og, gate, xcs, cq, ckv, kr = outs
        o_gla = gla(qkv, og, gate, f["gla_norm_g"][l])
        o_fft = dft_seq(a_dft, xcs.reshape(2 * S, B * FNET_W), dft_scale)
        o_mla = mla(f["mla_flag"][l], cq, ckv, kr, rope, f["wq"][l], f["wk"][l], f["wv"][l], f["qg"][l], f["kg"][l],
                    vone, f["mla_shift"][l], tq, tk)
        x, h, route, counts = out_proj(x, o_gla, o_fft, o_mla, mod, f["w_out"][l], f["norm2_g"][l], f["rw"][l],
                                       f["rb"][l], tm)
        dest, blk_e, n_used, n_rows = _dispatch_plan(route.reshape(T, ROUTE_LANES), counts)
        xb = sc_scatter_rows(h.reshape(T, D // 2), dest.reshape(TOP_K * T), n_rows)
        yb = moe_mlp(xb, blk_e, n_used, f["w1"], f["w3"], f["w2"], l)
        ya = sc_gather_rows(yb, dest[0]).reshape(B, S, D // 2)
        yc = sc_gather_rows(yb, dest[1]).reshape(B, S, D // 2)
        moe = (ya, yc, route, mod)
    return combine(x, *moe, tm)


def kernel(x_prompt, x_sample, c_prompt, c_sample, ada_w, ada_b, norm1_g, norm2_g, w_in, w_out, gla_gate_up_f, gla_gate_bias_f, gla_gate_up_b, gla_gate_bias_b, gla_out_norm_g, mla_q_lora_norm_g, mla_w_uq, mla_kv_lora_norm_g, mla_w_ukv, mla_q_norm_g, mla_k_norm_g, router_group_w, router_group_b, router_expert_w, router_expert_b, expert_w1, expert_w3, expert_w2):
    p = dict(norm1_g=norm1_g, norm2_g=norm2_g, w_in=w_in, w_out=w_out, gla_gate_up_f=gla_gate_up_f,
             gla_gate_bias_f=gla_gate_bias_f, gla_gate_up_b=gla_gate_up_b, gla_gate_bias_b=gla_gate_bias_b,
             gla_out_norm_g=gla_out_norm_g, mla_q_lora_norm_g=mla_q_lora_norm_g, mla_w_uq=mla_w_uq,
             mla_kv_lora_norm_g=mla_kv_lora_norm_g, mla_w_ukv=mla_w_ukv, mla_q_norm_g=mla_q_norm_g,
             mla_k_norm_g=mla_k_norm_g, router_group_w=router_group_w, router_group_b=router_group_b,
             router_expert_w=router_expert_w, router_expert_b=router_expert_b, expert_w1=expert_w1,
             expert_w3=expert_w3, expert_w2=expert_w2)
    f = _prep_params(p)
    consts = _const_tables()
    nb_p = c_prompt.shape[0]
    mod_all = ada_modulation(jnp.concatenate([c_prompt, c_sample], axis=0), ada_w, ada_b)
    y_prompt = _trunk(x_prompt, mod_all[:, :nb_p], f, consts, _seq_tables(x_prompt.shape[1]))
    y_sample = _trunk(x_sample, mod_all[:, nb_p:], f, consts, _seq_tables(x_sample.shape[1]))
    return (y_prompt, y_sample)
```

```python
import functools
import math

import jax
import jax.numpy as jnp
import numpy as np
from jax import lax
from jax.experimental import pallas as pl
from jax.experimental.pallas import tpu as pltpu
from jax.experimental.pallas import tpu_sc as plsc

F32 = jnp.float32
BF16 = jnp.bfloat16

D_MODEL = 1024
DEPTH = 4
EPS = 1e-6

GLA_HEADS = 4
GLA_DK = 32
GLA_DV = 64
GLA_QK_W = GLA_HEADS * GLA_DK
GLA_W = GLA_HEADS * GLA_DV
GLA_GATE_RANK = 16
GLA_GATE_NORMALIZER = 16.0
GLA_CHUNK = 32
GLA_BLOCK = 128

FNET_GROUPS = 4
FNET_GROUP_DIM = 64
FNET_W = 256

MLA_HEADS = 8
MLA_Q_LORA = 256
MLA_KV_LORA = 128
MLA_NOPE = 64
MLA_ROPE = 32
MLA_V = 64
MLA_QK = 96
MLA_W = 512
ROPE_THETA = 10000.0
HEAD_PAD = 128

N_GROUPS = 4
EXPERTS_PER_GROUP = 8
N_EXPERTS = 32
TOP_K = 2
D_EXPERT = 512
ROUTE_LANES = 128

C_Q, C_K, C_V, C_OG, C_F, C_CQ, C_CKV, C_KR, C_GATE = 0, 128, 256, 512, 768, 1024, 1280, 1408, 1536
P_IN_PAD = 1664

VMEM_LIMIT_BYTES = 56 * 1024 * 1024


def _cparams(sem, vmem=None):
    return pltpu.CompilerParams(dimension_semantics=sem, vmem_limit_bytes=vmem or VMEM_LIMIT_BYTES)


def _silu(x):
    return x * (1.0 / (1.0 + jnp.exp(-x)))


def _log_sigmoid(x):
    return -(jnp.maximum(-x, 0.0) + jnp.log1p(jnp.exp(-jnp.abs(x))))


def _dot(a, b):
    return jnp.dot(a, b, preferred_element_type=F32)


def _dot_nt(a, b):
    return lax.dot_general(a, b, (((1,), (1,)), ((), ())), preferred_element_type=F32)


def _split_bf16(x):
    hi = x.astype(BF16)
    lo = (x - hi.astype(F32)).astype(BF16)
    return hi, lo


def _pack_halves(x):
    n = x.shape[-1] // 2
    lo = lax.shift_right_logical(lax.bitcast_convert_type(x[:, :n], jnp.int32), 16)
    hi = lax.bitcast_convert_type(x[:, n:], jnp.int32) & jnp.int32(-65536)
    return hi | lo


def _unpack_halves(w):
    lo = lax.bitcast_convert_type(lax.shift_left(w, 16), F32)
    hi = lax.bitcast_convert_type(w & jnp.int32(-65536), F32)
    return lo, hi


def _ada_kernel(c_ref, w_ref, b_ref, o_ref):
    c = _silu(c_ref[...]).astype(BF16)
    o_ref[0] = _dot(c, w_ref[0].astype(BF16)) + b_ref[0]


def ada_modulation(c_all, ada_w, ada_b):
    nb = c_all.shape[0]
    tn = 1536
    n = ada_w.shape[-1]
    return pl.pallas_call(
        _ada_kernel,
        grid=(DEPTH, n // tn),
        in_specs=[
            pl.BlockSpec((nb, D_MODEL), lambda l, j: (0, 0)),
            pl.BlockSpec((1, D_MODEL, tn), lambda l, j: (l, 0, j)),
            pl.BlockSpec((1, 1, tn), lambda l, j: (l, 0, j)),
        ],
        out_specs=pl.BlockSpec((1, nb, tn), lambda l, j: (l, 0, j)),
        out_shape=jax.ShapeDtypeStruct((DEPTH, nb, n), F32),
        compiler_params=_cparams(("parallel", "parallel")),
        name="ada_modulation",
    )(c_all, ada_w, ada_b.reshape(DEPTH, 1, n))


def _in_proj_body(x, mod_ref, g_ref, w_ref, dft_ref, up_ref, gb_ref, lg_ref,
                  qkv_ref, og_ref, gate_ref, xcs_ref, cq_ref, ckv_ref, kr_ref):
    sh = mod_ref[0, 0:1, :]
    sc = mod_ref[0, 1:2, :]
    r = lax.rsqrt(jnp.mean(x * x, axis=-1, keepdims=True) + EPS)
    h = (x * r * g_ref[...]) * (1.0 + sc) + sh
    u = _dot(h.astype(BF16), w_ref[...])

    q = u[:, C_Q:C_K] * (GLA_DK ** -0.5)
    qkv_ref[:, 0:128] = q.astype(BF16)
    qkv_ref[:, 128:512] = u[:, C_K:C_OG].astype(BF16)
    og_ref[...] = u[:, C_OG:C_F].astype(BF16)

    ug = u[:, C_GATE:C_GATE + 128].astype(BF16)
    gl = _dot(ug, up_ref[...]) + gb_ref[...]
    gate_ref[...] = _log_sigmoid(gl) * (1.0 / GLA_GATE_NORMALIZER)

    uf = u[:, C_F:C_CQ].astype(BF16)
    xcs_ref[0] = _dot(uf, dft_ref[0]).astype(BF16)
    xcs_ref[1] = _dot(uf, dft_ref[1]).astype(BF16)

    cq = u[:, C_CQ:C_CKV]
    rq = lax.rsqrt(jnp.mean(cq * cq, axis=-1, keepdims=True) + EPS)
    cq_ref[...] = (cq * rq * lg_ref[:, 0:256]).astype(BF16)
    ckv = u[:, C_CKV:C_KR]
    rkv = lax.rsqrt(jnp.mean(ckv * ckv, axis=-1, keepdims=True) + EPS)
    ckv_ref[...] = (ckv * rkv * lg_ref[:, 256:384]).astype(BF16)
    kr_ref[...] = u[:, C_KR:C_GATE].astype(BF16)


def _in_proj_kernel(x_ref, mod_ref, g_ref, w_ref, dft_ref, up_ref, gb_ref, lg_ref,
                    qkv_ref, og_ref, gate_ref, xcs_ref, cq_ref, ckv_ref, kr_ref):
    _in_proj_body(x_ref[0], mod_ref, g_ref, w_ref, dft_ref, up_ref, gb_ref, lg_ref,
                  qkv_ref.at[0], og_ref.at[0], gate_ref.at[0], xcs_ref, cq_ref.at[0], ckv_ref.at[0], kr_ref.at[0])


def _in_proj_moe_kernel(x_ref, ya_ref, yc_ref, route_ref, modp_ref, mod_ref, g_ref, w_ref, dft_ref, up_ref, gb_ref,
                        lg_ref, xo_ref, qkv_ref, og_ref, gate_ref, xcs_ref, cq_ref, ckv_ref, kr_ref):
    x = _moe_residual(x_ref[0], ya_ref[0], yc_ref[0], route_ref[0], modp_ref)
    xo_ref[0] = x
    _in_proj_body(x, mod_ref, g_ref, w_ref, dft_ref, up_ref, gb_ref, lg_ref,
                  qkv_ref.at[0], og_ref.at[0], gate_ref.at[0], xcs_ref, cq_ref.at[0], ckv_ref.at[0], kr_ref.at[0])


def in_proj(x, mod, norm_g, w_in, dft64, gate_up, gate_bias, lora_g, tm, moe=None):
    B, S, D = x.shape
    ns = S // tm
    row = lambda b, s: (b, s, 0)
    const2 = lambda b, s: (0, 0)
    modmap = lambda b, s: (b, 0, 0)
    if moe is None:
        body, lead_in, lead_specs, lead_out_specs, lead_out_shapes = _in_proj_kernel, (x,), [pl.BlockSpec((1, tm, D), row)], (), ()
    else:
        ya, yc, route, mod_prev = moe
        body = _in_proj_moe_kernel
        lead_in = (x, ya, yc, route, mod_prev)
        lead_specs = [pl.BlockSpec((1, tm, D), row), pl.BlockSpec((1, tm, D // 2), row),
                      pl.BlockSpec((1, tm, D // 2), row), pl.BlockSpec((1, tm, ROUTE_LANES), row),
                      pl.BlockSpec((1, 6, D), modmap)]
        lead_out_specs = (pl.BlockSpec((1, tm, D), row),)
        lead_out_shapes = (jax.ShapeDtypeStruct((B, S, D), F32),)
    out_shapes = lead_out_shapes + (
        jax.ShapeDtypeStruct((B, S, 512), BF16),
        jax.ShapeDtypeStruct((B, S, GLA_W), BF16),
        jax.ShapeDtypeStruct((B, S, 256), F32),
        jax.ShapeDtypeStruct((2, S, B * FNET_W), BF16),
        jax.ShapeDtypeStruct((B, S, MLA_Q_LORA), BF16),
        jax.ShapeDtypeStruct((B, S, MLA_KV_LORA), BF16),
        jax.ShapeDtypeStruct((B, S, HEAD_PAD), BF16),
    )
    return pl.pallas_call(
        body,
        grid=(B, ns),
        in_specs=lead_specs + [
            pl.BlockSpec((1, 6, D), modmap),
            pl.BlockSpec((1, D), const2),
            pl.BlockSpec((D, P_IN_PAD), const2),
            pl.BlockSpec((2, FNET_W, FNET_W), lambda b, s: (0, 0, 0)),
            pl.BlockSpec((128, 256), const2),
            pl.BlockSpec((1, 256), const2),
            pl.BlockSpec((1, 384), const2),
        ],
        out_specs=lead_out_specs + (
            pl.BlockSpec((1, tm, 512), row),
            pl.BlockSpec((1, tm, GLA_W), row),
            pl.BlockSpec((1, tm, 256), row),
            pl.BlockSpec((2, tm, FNET_W), lambda b, s: (0, s, b)),
            pl.BlockSpec((1, tm, MLA_Q_LORA), row),
            pl.BlockSpec((1, tm, MLA_KV_LORA), row),
            pl.BlockSpec((1, tm, HEAD_PAD), row),
        ),
        out_shape=out_shapes,
        compiler_params=_cparams(("parallel", "parallel")),
        name="in_proj",
    )(*lead_in, mod, norm_g, w_in, dft64, gate_up, gate_bias, lora_g)


def _gla_kernel(qkv_ref, og_ref, gate_ref, ng_ref, o_ref, o_s, st_s, *, seq):
    R, C = GLA_BLOCK, GLA_CHUNK
    n_sub = R // C
    nblk = seq // R
    shift_c = int(math.log2(C))

    row = lax.broadcasted_iota(jnp.int32, (R, R), 0)
    col = lax.broadcasted_iota(jnp.int32, (R, R), 1)
    same = (row >> shift_c) == (col >> shift_c)
    ones_blk = jnp.where(same, 1.0, 0.0)
    lane_qk = lax.broadcasted_iota(jnp.int32, (1, GLA_QK_W), 1)
    lane_v = lax.broadcasted_iota(jnp.int32, (1, GLA_W), 1)
    head_qk = [(lane_qk >> 5) == h for h in range(GLA_HEADS)]
    head_v = [(lane_v >> 6) == h for h in range(GLA_HEADS)]
    st_row = lax.broadcasted_iota(jnp.int32, (GLA_W, GLA_QK_W), 0)
    st_col = lax.broadcasted_iota(jnp.int32, (GLA_W, GLA_QK_W), 1)
    st_mask = (st_row >> 6) == (st_col >> 5)
    sub_row = lax.broadcasted_iota(jnp.int32, (R, 1), 0) >> shift_c
    nrow = lax.broadcasted_iota(jnp.int32, (GLA_W, GLA_W), 0)
    ncol = lax.broadcasted_iota(jnp.int32, (GLA_W, GLA_W), 1)
    head_mean = jnp.where((nrow >> 6) == (ncol >> 6), 1.0 / GLA_DV, 0.0).astype(BF16)

    def direction(backward):
        if backward:
            tri = same & (col >= row)
            att_ok = same & (col > row)
            order = list(range(n_sub - 1, -1, -1))
        else:
            tri = same & (col <= row)
            att_ok = same & (col <= row)
            order = list(range(n_sub))
        cum_lhs = jnp.concatenate([jnp.where(tri, 1.0, 0.0), ones_blk], axis=0).astype(BF16)
        att_ok4 = jnp.concatenate([att_ok] * GLA_HEADS, axis=1)
        return cum_lhs, att_ok4, order, (GLA_QK_W if backward else 0), (1 if backward else 0)

    def block(j, consts):
        cum_lhs, att_ok4, order, gate_off, slot = consts
        r0 = pl.multiple_of(j * R, R)
        qkv = qkv_ref[0, pl.ds(r0, R), :]
        q = qkv[:, 0:128].astype(F32)
        k = qkv[:, 128:256].astype(F32)
        v = qkv[:, 256:512]
        g = gate_ref[0, pl.ds(r0, R), gate_off:gate_off + GLA_QK_W]
        g_hi, g_lo = _split_bf16(g)
        cs = _dot(cum_lhs, jnp.concatenate([g_hi, g_lo], axis=1))
        b = cs[0:R, 0:128] + cs[0:R, 128:256]
        bl = cs[R:2 * R, 0:128] + cs[R:2 * R, 128:256]
        q_dec = q * jnp.exp(b)
        k_inv = (k * jnp.exp(-b)).astype(BF16)
        k_end = k * jnp.exp(bl - b)
        decay = jnp.exp(bl)

        att = jnp.concatenate(
            [_dot_nt(jnp.where(head_qk[h], q_dec, 0.0).astype(BF16), k_inv) for h in range(GLA_HEADS)],
            axis=1)
        att = jnp.where(att_ok4, att, 0.0).astype(BF16)
        v_heads = jnp.concatenate([jnp.where(head_v[h], v, jnp.zeros_like(v)) for h in range(GLA_HEADS)], axis=0)
        o_blk = _dot(att, v_heads)

        v_t = v.astype(F32).T.astype(BF16)
        q_dec_b = q_dec.astype(BF16)
        inter = [None] * n_sub
        for c in order:
            st = st_s[slot]
            inter[c] = _dot_nt(q_dec_b[c * C:(c + 1) * C, :], st.astype(BF16))
            k_c = jnp.where(sub_row == c, k_end, 0.0).astype(BF16)
            d_st = _dot(v_t, k_c)
            st_s[slot] = st * decay[c * C:c * C + 1, :] + jnp.where(st_mask, d_st, 0.0)
        o_s[slot, pl.ds(r0, R), :] = o_blk + jnp.concatenate(inter, axis=0)

    fwd = direction(False)
    bwd = direction(True)
    st_s[...] = jnp.zeros_like(st_s)

    def scan_step(i, carry):
        block(i, fwd)
        block(nblk - 1 - i, bwd)
        return carry

    lax.fori_loop(0, nblk, scan_step, 0)

    def finish(j, carry):
        r0 = pl.multiple_of(j * R, R)
        o = o_s[0, pl.ds(r0, R), :] + o_s[1, pl.ds(r0, R), :]
        sq_hi, sq_lo = _split_bf16(o * o)
        ms = _dot(sq_hi, head_mean) + _dot(sq_lo, head_mean)
        y = o * lax.rsqrt(ms + EPS) * ng_ref[...]
        og = og_ref[0, pl.ds(r0, R), :].astype(F32)
        o_ref[0, pl.ds(r0, R), :] = (y * _silu(og)).astype(o_ref.dtype)
        return carry

    lax.fori_loop(0, nblk, finish, 0)


def gla(qkv, og, gate, norm_g4):
    B, S, _ = qkv.shape
    full = lambda b: (b, 0, 0)
    return pl.pallas_call(
        functools.partial(_gla_kernel, seq=S),
        grid=(B,),
        in_specs=[
            pl.BlockSpec((1, S, 512), full),
            pl.BlockSpec((1, S, GLA_W), full),
            pl.BlockSpec((1, S, 256), full),
            pl.BlockSpec((1, GLA_W), lambda b: (0, 0)),
        ],
        out_specs=pl.BlockSpec((1, S, GLA_W), full),
        out_shape=jax.ShapeDtypeStruct((B, S, GLA_W), BF16),
        scratch_shapes=[pltpu.VMEM((2, S, GLA_W), F32), pltpu.VMEM((2, GLA_W, GLA_QK_W), F32)],
        compiler_params=_cparams(("parallel",)),
        name="gla",
    )(qkv, og, gate, norm_g4)


def _dft_kernel(a_ref, b_ref, o_ref, acc_ref, *, scale):
    kk = pl.program_id(2)

    @pl.when(kk == 0)
    def _():
        acc_ref[...] = jnp.zeros_like(acc_ref)

    acc_ref[...] += _dot(a_ref[...], b_ref[...])

    @pl.when(kk == pl.num_programs(2) - 1)
    def _():
        o_ref[...] = (acc_ref[...] * scale).astype(o_ref.dtype)


def dft_seq(a_dft, xcs, scale):
    m, k = a_dft.shape
    n = xcs.shape[1]
    tm, tn, tk = min(m, 1024), min(n, 1024), min(k, 2048)
    return pl.pallas_call(
        functools.partial(_dft_kernel, scale=scale),
        grid=(m // tm, n // tn, k // tk),
        in_specs=[pl.BlockSpec((tm, tk), lambda i, j, kk: (i, kk)),
                  pl.BlockSpec((tk, tn), lambda i, j, kk: (kk, j))],
        out_specs=pl.BlockSpec((tm, tn), lambda i, j, kk: (i, j)),
        out_shape=jax.ShapeDtypeStruct((m, n), BF16),
        scratch_shapes=[pltpu.VMEM((tm, tn), F32)],
        compiler_params=_cparams(("parallel", "parallel", "arbitrary")),
        name="dft_seq",
    )(a_dft, xcs)


def _rope(x, cos, sin, swap):
    return x * cos + _dot(x.astype(BF16), swap) * sin


SHIFT_LANE = HEAD_PAD - 1
MAX_STATIC_SHIFT = 40.0
LOG2E = math.log2(math.e)


def _mla_kernel(flag_ref, cq_ref, ckv_ref, kr_ref, rope_ref, wq_ref, wk_ref, wv_ref, qg_ref, kg_ref, vone_ref,
                shift_ref, o_ref, kt_s, v_s, q_s, acc_s, *, seq, tq, tk):
    qi = pl.program_id(1)
    inv_qk = 1.0 / MLA_QK
    lane = lax.broadcasted_iota(jnp.int32, (1, HEAD_PAD), 1)
    src = lax.broadcasted_iota(jnp.int32, (HEAD_PAD, HEAD_PAD), 0)
    dst = lax.broadcasted_iota(jnp.int32, (HEAD_PAD, HEAD_PAD), 1)
    half = MLA_ROPE // 2
    lo_half = (dst >= MLA_NOPE) & (dst < MLA_NOPE + half)
    hi_half = (dst >= MLA_NOPE + half) & (dst < MLA_QK)
    swap = jnp.where((lo_half & (src == dst + half)) | (hi_half & (src == dst - half)), 1.0, 0.0).astype(BF16)

    @pl.when(qi == 0)
    def _():
        one_lane = jnp.where(lane == SHIFT_LANE, 1.0, 0.0)

        def rows(i, carry):
            r0 = pl.multiple_of(i * tk, tk)
            ckv = ckv_ref[0, pl.ds(r0, tk), :]
            kr = kr_ref[0, pl.ds(r0, tk), :].astype(F32)
            cos = rope_ref[0, pl.ds(r0, tk), :]
            sin = rope_ref[1, pl.ds(r0, tk), :]
            for h in range(MLA_HEADS):
                kp = _dot(ckv, wk_ref[h]) + kr
                r = lax.rsqrt(jnp.sum(kp * kp, axis=-1, keepdims=True) * inv_qk + EPS)
                kn = _rope(kp * r * kg_ref[...], cos, sin, swap) + one_lane
                kt_s[h, i] = kn.T.astype(BF16)
                v_s[h, pl.ds(r0, tk), :] = (_dot(ckv, wv_ref[h]) + vone_ref[h]).astype(BF16)
            return carry

        lax.fori_loop(0, seq // tk, rows, 0)

    q0 = pl.multiple_of(qi * tq, tq)
    cos = rope_ref[0, pl.ds(q0, tq), :]
    sin = rope_ref[1, pl.ds(q0, tq), :]
    for h in range(MLA_HEADS):
        qp = _dot(cq_ref[0], wq_ref[h])
        r = lax.rsqrt(jnp.sum(qp * qp, axis=-1, keepdims=True) * inv_qk + EPS)
        q_s[h] = (_rope(qp * r * qg_ref[...], cos, sin, swap) + shift_ref[...]).astype(BF16)

    def finish():
        for pair in range(MLA_HEADS // 2):
            outs = []
            for h in (2 * pair, 2 * pair + 1):
                acc = acc_s[h]
                den_lane = MLA_V if h % 2 == 0 else 0
                den = jnp.sum(jnp.where(lane == den_lane, acc, 0.0), axis=-1, keepdims=True)
                outs.append(acc * (1.0 / den))
            both = jnp.where(lane < MLA_V, outs[0], outs[1])
            o_ref[0, :, pair * HEAD_PAD:(pair + 1) * HEAD_PAD] = both.astype(o_ref.dtype)

    @pl.when(flag_ref[0] == 1)
    def _():
        acc_s[...] = jnp.zeros_like(acc_s)

        def kv_step(j, carry):
            k0 = pl.multiple_of(j * tk, tk)
            for h in range(MLA_HEADS):
                p = jnp.exp2(_dot(q_s[h], kt_s[h, j])).astype(BF16)
                acc_s[h] += _dot(p, v_s[h, pl.ds(k0, tk), :])
            return carry

        lax.fori_loop(0, seq // tk, kv_step, 0)
        finish()

    @pl.when(flag_ref[0] == 0)
    def _():
        for h in range(MLA_HEADS):
            def kv_step(j, carry, h=h):
                m, acc = carry
                k0 = pl.multiple_of(j * tk, tk)
                s = _dot(q_s[h], kt_s[h, j])
                m_new = jnp.maximum(m, jnp.max(s, axis=-1, keepdims=True))
                p = jnp.exp2(s - m_new).astype(BF16)
                return m_new, jnp.exp2(m - m_new) * acc + _dot(p, v_s[h, pl.ds(k0, tk), :])

            m0 = jnp.full((tq, 1), -jnp.inf, F32)
            _, acc = lax.fori_loop(0, seq // tk, kv_step, (m0, jnp.zeros((tq, HEAD_PAD), F32)))
            acc_s[h] = acc
        finish()


def mla(flag, cq, ckv, kr, rope, wq, wk, wv, qg, kg, vone, shift, tq, tk):
    B, S, _ = cq.shape
    c3 = lambda b, q: (0, 0, 0)
    c2 = lambda b, q: (0, 0)
    return pl.pallas_call(
        functools.partial(_mla_kernel, seq=S, tq=tq, tk=tk),
        grid=(B, S // tq),
        in_specs=[
            pl.BlockSpec(memory_space=pltpu.SMEM),
            pl.BlockSpec((1, tq, MLA_Q_LORA), lambda b, q: (b, q, 0)),
            pl.BlockSpec((1, S, MLA_KV_LORA), lambda b, q: (b, 0, 0)),
            pl.BlockSpec((1, S, HEAD_PAD), lambda b, q: (b, 0, 0)),
            pl.BlockSpec((2, S, HEAD_PAD), c3),
            pl.BlockSpec((MLA_HEADS, MLA_Q_LORA, HEAD_PAD), c3),
            pl.BlockSpec((MLA_HEADS, MLA_KV_LORA, HEAD_PAD), c3),
            pl.BlockSpec((MLA_HEADS, MLA_KV_LORA, HEAD_PAD), c3),
            pl.BlockSpec((1, HEAD_PAD), c2),
            pl.BlockSpec((1, HEAD_PAD), c2),
            pl.BlockSpec((MLA_HEADS, 1, HEAD_PAD), c3),
            pl.BlockSpec((1, HEAD_PAD), c2),
        ],
        out_specs=pl.BlockSpec((1, tq, MLA_W), lambda b, q: (b, q, 0)),
        out_shape=jax.ShapeDtypeStruct((B, S, MLA_W), BF16),
        scratch_shapes=[pltpu.VMEM((MLA_HEADS, S // tk, HEAD_PAD, tk), BF16),
                        pltpu.VMEM((MLA_HEADS, S, HEAD_PAD), BF16),
                        pltpu.VMEM((MLA_HEADS, tq, HEAD_PAD), BF16),
                        pltpu.VMEM((MLA_HEADS, tq, HEAD_PAD), F32)],
        compiler_params=_cparams(("parallel", "arbitrary")),
        name="mla",
    )(flag, cq, ckv, kr, rope, wq, wk, wv, qg, kg, vone, shift)


def _out_proj_kernel(x_ref, gla_ref, fft_ref, mla_ref, mod_ref, w_ref, g_ref, rw_ref, rb_ref,
                     xo_ref, h_ref, route_ref, cnt_ref):
    mix = (_dot(gla_ref[0], w_ref[0:256, :]) + _dot(fft_ref[...], w_ref[256:512, :])
           + _dot(mla_ref[0], w_ref[512:1024, :]))
    g1 = mod_ref[0, 2:3, :]
    sh = mod_ref[0, 3:4, :]
    sc = mod_ref[0, 4:5, :]
    x = x_ref[0] + g1 * mix
    xo_ref[0] = x
    r = lax.rsqrt(jnp.mean(x * x, axis=-1, keepdims=True) + EPS)
    h = (x * r * g_ref[...]) * (1.0 + sc) + sh
    h_hi, h_lo = _split_bf16(h)
    h_ref[0] = _pack_halves(h_hi.astype(F32))

    logit = _dot(h_hi, rw_ref[0]) + _dot(h_lo, rw_ref[0]) + _dot(h_hi, rw_ref[1]) + rb_ref[...]
    lane = lax.broadcasted_iota(jnp.int32, (1, ROUTE_LANES), 1)
    lane_f = lane.astype(F32)
    neg = -1e30
    is_g = lane < N_GROUPS
    is_e = (lane >= N_GROUPS) & (lane < N_GROUPS + N_EXPERTS)
    lg = jnp.where(is_g, logit, neg)
    g_max = jnp.max(lg, axis=-1, keepdims=True)
    g_den = jnp.sum(jnp.where(is_g, jnp.exp(lg - g_max), 0.0), axis=-1, keepdims=True)
    g_w = 1.0 / g_den
    g_top = jnp.min(jnp.where(is_g & (logit == g_max), lane_f, 1e9), axis=-1, keepdims=True)
    e_grp = ((lane - N_GROUPS) >> 3).astype(F32)
    in_grp = is_e & (e_grp == g_top)
    le = jnp.where(in_grp, logit, neg)
    t1 = jnp.max(le, axis=-1, keepdims=True)
    i1 = jnp.min(jnp.where(in_grp & (le == t1), lane_f, 1e9), axis=-1, keepdims=True)
    le2 = jnp.where(lane_f == i1, neg, le)
    t2 = jnp.max(le2, axis=-1, keepdims=True)
    i2 = jnp.min(jnp.where(in_grp & (le2 == t2), lane_f, 1e9), axis=-1, keepdims=True)
    e21 = jnp.exp(t2 - t1)
    w1 = g_w / (1.0 + e21)
    w2 = w1 * e21
    route = jnp.where(lane == 0, i1 - N_GROUPS,
                      jnp.where(lane == 1, i2 - N_GROUPS,
                                jnp.where(lane == 2, w1, jnp.where(lane == 3, w2, 0.0))))
    route_ref[0] = route

    @pl.when((pl.program_id(0) == 0) & (pl.program_id(1) == 0))
    def _():
        cnt_ref[...] = jnp.zeros_like(cnt_ref)

    picked = jnp.where((lane_f == i1 - N_GROUPS) | (lane_f == i2 - N_GROUPS), 1.0, 0.0)
    cnt_ref[...] += jnp.sum(picked, axis=0, keepdims=True)


def out_proj(x, o_gla, o_fft, o_mla, mod, w_out, norm_g, rw, rb, tm):
    B, S, D = x.shape
    row = lambda b, s: (b, s, 0)
    c2 = lambda b, s: (0, 0)
    return pl.pallas_call(
        _out_proj_kernel,
        grid=(B, S // tm),
        in_specs=[
            pl.BlockSpec((1, tm, D), row),
            pl.BlockSpec((1, tm, GLA_W), row),
            pl.BlockSpec((tm, FNET_W), lambda b, s: (s, b)),
            pl.BlockSpec((1, tm, MLA_W), row),
            pl.BlockSpec((1, 6, D), lambda b, s: (b, 0, 0)),
            pl.BlockSpec((D, D), c2),
            pl.BlockSpec((1, D), c2),
            pl.BlockSpec((2, D, ROUTE_LANES), lambda b, s: (0, 0, 0)),
            pl.BlockSpec((1, ROUTE_LANES), c2),
        ],
        out_specs=(pl.BlockSpec((1, tm, D), row), pl.BlockSpec((1, tm, D // 2), row),
                   pl.BlockSpec((1, tm, ROUTE_LANES), row), pl.BlockSpec((1, ROUTE_LANES), c2)),
        out_shape=(jax.ShapeDtypeStruct((B, S, D), F32), jax.ShapeDtypeStruct((B, S, D // 2), jnp.int32),
                   jax.ShapeDtypeStruct((B, S, ROUTE_LANES), F32), jax.ShapeDtypeStruct((1, ROUTE_LANES), F32)),
        compiler_params=_cparams(("arbitrary", "arbitrary")),
        name="out_proj",
    )(x, o_gla, o_fft, o_mla, mod, w_out, norm_g, rw, rb)


RANK_ROWS = 1024


def _rank_kernel(route_ref, start_ref, dest_ref, carry_ref, before_ref):
    n = route_ref.shape[0]

    @pl.when(pl.program_id(0) == 0)
    def _():
        carry_ref[...] = jnp.zeros_like(carry_ref)
        row = lax.broadcasted_iota(jnp.int32, (n, n), 0)
        col = lax.broadcasted_iota(jnp.int32, (n, n), 1)
        before_ref[...] = jnp.where(col < row, 1.0, 0.0).astype(BF16)

    lane_f = lax.broadcasted_iota(jnp.int32, (1, ROUTE_LANES), 1).astype(F32)
    e1 = route_ref[:, 0:1]
    e2 = route_ref[:, 1:2]
    oh1 = jnp.where(lane_f == e1, 1.0, 0.0)
    oh2 = jnp.where(lane_f == e2, 1.0, 0.0)
    both = (oh1 + oh2).astype(BF16)
    pos = _dot(before_ref[...], both) + carry_ref[...] + start_ref[...]
    d1 = jnp.sum(oh1 * pos, axis=-1, keepdims=True)
    d2 = jnp.sum(oh2 * pos, axis=-1, keepdims=True)
    lane = lax.broadcasted_iota(jnp.int32, (1, ROUTE_LANES), 1)
    cols = jnp.where(lane == 0, d1, jnp.where(lane == 1, d2, 0.0))
    dest_ref[...] = cols.T[0:TOP_K, :].astype(jnp.int32)
    carry_ref[...] += jnp.sum(oh1 + oh2, axis=0, keepdims=True)


def route_rank(route2d, start):
    T = route2d.shape[0]
    rr = min(T, RANK_ROWS)
    return pl.pallas_call(
        _rank_kernel,
        grid=(T // rr,),
        in_specs=[pl.BlockSpec((rr, ROUTE_LANES), lambda i: (i, 0)),
                  pl.BlockSpec((1, ROUTE_LANES), lambda i: (0, 0))],
        out_specs=pl.BlockSpec((TOP_K, rr), lambda i: (0, i)),
        out_shape=jax.ShapeDtypeStruct((TOP_K, T), jnp.int32),
        scratch_shapes=[pltpu.VMEM((1, ROUTE_LANES), F32), pltpu.VMEM((rr, rr), BF16)],
        compiler_params=_cparams(("arbitrary",)),
        name="route_rank",
    )(route2d, start)


MOE_ROWS = 512


def _moe_kernel(blk_e_ref, n_used_ref, x_ref, w1_ref, w3_ref, w2_ref, o_ref):
    i = pl.program_id(0)

    @pl.when(i < n_used_ref[0])
    def _():
        half = D_MODEL // 2
        x_lo, x_hi = _unpack_halves(x_ref[...])
        x_lo = x_lo.astype(BF16)
        x_hi = x_hi.astype(BF16)
        w1 = w1_ref[0, 0].astype(BF16)
        w3 = w3_ref[0, 0].astype(BF16)
        a = _dot(x_lo, w1[0:half, :]) + _dot(x_hi, w1[half:, :])
        b = _dot(x_lo, w3[0:half, :]) + _dot(x_hi, w3[half:, :])
        hm = (_silu(a) * b).astype(BF16)
        y = _dot(hm, w2_ref[0, 0].astype(BF16))
        o_ref[...] = _pack_halves(y.astype(BF16).astype(F32))

    @pl.when(i >= n_used_ref[0])
    def _():
        o_ref[...] = jnp.zeros_like(o_ref)


def moe_mlp(xb, blk_e, n_used, w1, w3, w2, layer):
    P, half = xb.shape
    n_blk = P // MOE_ROWS
    wmap = lambda i, be, nu: (layer, be[i], 0, 0)
    grid_spec = pltpu.PrefetchScalarGridSpec(
        num_scalar_prefetch=2,
        grid=(n_blk,),
        in_specs=[
            pl.BlockSpec((MOE_ROWS, half), lambda i, be, nu: (i, 0)),
            pl.BlockSpec((1, 1, D_MODEL, D_EXPERT), wmap),
            pl.BlockSpec((1, 1, D_MODEL, D_EXPERT), wmap),
            pl.BlockSpec((1, 1, D_EXPERT, D_MODEL), wmap),
        ],
        out_specs=pl.BlockSpec((MOE_ROWS, half), lambda i, be, nu: (i, 0)),
    )
    return pl.pallas_call(
        _moe_kernel,
        grid_spec=grid_spec,
        out_shape=jax.ShapeDtypeStruct((P, half), jnp.int32),
        compiler_params=_cparams(("arbitrary",)),
        name="moe_mlp",
    )(blk_e, n_used, xb, w1, w3, w2)


SC_WINDOW = 128


def sc_gather_rows(table, idx):
    n = idx.shape[0]
    width = table.shape[1]
    mesh = plsc.VectorSubcoreMesh(core_axis_name="core", subcore_axis_name="subcore")

    @functools.partial(pl.kernel, out_type=jax.ShapeDtypeStruct((n, width), table.dtype), mesh=mesh,
                       name="sc_gather_rows")
    def gather_kernel(x_hbm, i_hbm, o_hbm):
        def body(i_vmem, o_vmem):
            pltpu.sync_copy(x_hbm.at[i_vmem.at[0]], o_vmem)

        pltpu.emit_pipeline(
            body,
            grid=(n // SC_WINDOW,),
            in_specs=[pl.BlockSpec((1, SC_WINDOW), lambda i: (0, i))],
            out_specs=[pl.BlockSpec((SC_WINDOW, width), lambda i: (i, 0), pipeline_mode=pl.Buffered(1))],
            core_axis_name=("core", "subcore"),
            dimension_semantics=(pltpu.PARALLEL,),
        )(i_hbm, o_hbm)

    return gather_kernel(table, idx.reshape(1, n))


def sc_scatter_rows(rows, idx, n_out):
    n = idx.shape[0]
    t, width = rows.shape
    n_src = t // SC_WINDOW
    mesh = plsc.VectorSubcoreMesh(core_axis_name="core", subcore_axis_name="subcore")

    @functools.partial(pl.kernel, out_type=jax.ShapeDtypeStruct((n_out, width), rows.dtype), mesh=mesh,
                       name="sc_scatter_rows")
    def scatter_kernel(x_hbm, i_hbm, o_hbm):
        def body(x_vmem, i_vmem):
            pltpu.sync_copy(x_vmem, o_hbm.at[i_vmem.at[0]])

        pltpu.emit_pipeline(
            body,
            grid=(n // SC_WINDOW,),
            in_specs=[pl.BlockSpec((SC_WINDOW, width), lambda i: (i % n_src, 0), pipeline_mode=pl.Buffered(1)),
                      pl.BlockSpec((1, SC_WINDOW), lambda i: (0, i))],
            out_specs=[],
            core_axis_name=("core", "subcore"),
            dimension_semantics=(pltpu.PARALLEL,),
        )(x_hbm, i_hbm)

    return scatter_kernel(rows, idx.reshape(1, n))


def _moe_residual(x, ya, yc, route, modp_ref):
    g2 = modp_ref[0, 5:6, :]
    wa = route[:, 2:3]
    wb = route[:, 3:4]
    a_lo, a_hi = _unpack_halves(ya)
    b_lo, b_hi = _unpack_halves(yc)
    y = jnp.concatenate([wa * a_lo + wb * b_lo, wa * a_hi + wb * b_hi], axis=1)
    return x + g2 * y


def _combine_kernel(x_ref, ya_ref, yc_ref, route_ref, mod_ref, o_ref):
    o_ref[0] = _moe_residual(x_ref[0], ya_ref[0], yc_ref[0], route_ref[0], mod_ref)


def combine(x, ya, yc, route, mod, tm):
    B, S, D = x.shape
    row = lambda b, s: (b, s, 0)
    return pl.pallas_call(
        _combine_kernel,
        grid=(B, S // tm),
        in_specs=[
            pl.BlockSpec((1, tm, D), row),
            pl.BlockSpec((1, tm, D // 2), row),
            pl.BlockSpec((1, tm, D // 2), row),
            pl.BlockSpec((1, tm, ROUTE_LANES), row),
            pl.BlockSpec((1, 6, D), lambda b, s: (b, 0, 0)),
        ],
        out_specs=pl.BlockSpec((1, tm, D), row),
        out_shape=jax.ShapeDtypeStruct((B, S, D), F32),
        compiler_params=_cparams(("parallel", "parallel")),
        name="combine",
    )(x, ya, yc, route, mod)


def _prep_params(p):
    f = {}
    w_in = p["w_in"]
    L = w_in.shape[0]
    z = lambda n: jnp.zeros((L, D_MODEL, n), F32)
    f["w_in"] = jnp.concatenate(
        [w_in[:, :, 0:768], w_in[:, :, 800:1056], w_in[:, :, 1056:1312], w_in[:, :, 1312:1440],
         z(MLA_NOPE), w_in[:, :, 1440:1472], z(HEAD_PAD - MLA_QK),
         w_in[:, :, 768:800], z(128 - 2 * GLA_GATE_RANK)], axis=-1).astype(BF16)
    up = jnp.zeros((L, 128, 256), F32)
    up = up.at[:, 0:16, 0:128].set(p["gla_gate_up_f"]).at[:, 16:32, 128:256].set(p["gla_gate_up_b"])
    f["gate_up"] = up.astype(BF16)
    f["gate_bias"] = jnp.concatenate([p["gla_gate_bias_f"], p["gla_gate_bias_b"]], axis=-1)[:, None, :]
    f["lora_g"] = jnp.concatenate([p["mla_q_lora_norm_g"], p["mla_kv_lora_norm_g"]], axis=-1)[:, None, :]
    f["gla_norm_g"] = jnp.tile(p["gla_out_norm_g"], (1, GLA_HEADS))[:, None, :]
    f["norm1_g"] = p["norm1_g"][:, None, :]
    f["norm2_g"] = p["norm2_g"][:, None, :]
    wq = p["mla_w_uq"].reshape(L, MLA_Q_LORA, MLA_HEADS, MLA_QK).transpose(0, 2, 1, 3)
    f["wq"] = jnp.pad(wq, ((0, 0), (0, 0), (0, 0), (0, HEAD_PAD - MLA_QK))).astype(BF16)
    wkv = p["mla_w_ukv"].reshape(L, MLA_KV_LORA, MLA_HEADS, MLA_NOPE + MLA_V).transpose(0, 2, 1, 3)
    f["wk"] = jnp.pad(wkv[..., :MLA_NOPE], ((0, 0), (0, 0), (0, 0), (0, HEAD_PAD - MLA_NOPE))).astype(BF16)
    wv = wkv[..., MLA_NOPE:]
    zv = jnp.zeros_like(wv)
    even = (jnp.arange(MLA_HEADS) % 2 == 0)[None, :, None, None]
    f["wv"] = jnp.where(even, jnp.concatenate([wv, zv], -1), jnp.concatenate([zv, wv], -1)).astype(BF16)
    pad_qk = ((0, 0), (0, HEAD_PAD - MLA_QK))
    f["qg"] = (jnp.pad(p["mla_q_norm_g"], pad_qk) * (MLA_QK ** -0.5 * LOG2E))[:, None, :]
    f["kg"] = jnp.pad(p["mla_k_norm_g"], pad_qk)[:, None, :]
    bound = (jnp.max(jnp.abs(p["mla_q_norm_g"]), axis=-1) * jnp.max(jnp.abs(p["mla_k_norm_g"]), axis=-1)
             * (math.sqrt(MLA_QK) * 1.01 * LOG2E) + 0.1).astype(BF16).astype(F32)
    use_bound = bound <= MAX_STATIC_SHIFT * LOG2E
    f["mla_flag"] = use_bound.astype(jnp.int32)[:, None]
    f["mla_shift"] = jnp.zeros((L, 1, HEAD_PAD), F32).at[:, 0, SHIFT_LANE].set(jnp.where(use_bound, -bound, 0.0))
    f["w_out"] = p["w_out"].astype(BF16)
    rw = jnp.concatenate([p["router_group_w"], p["router_expert_w"]], axis=-1)
    rw = jnp.pad(rw, ((0, 0), (0, 0), (0, ROUTE_LANES - N_GROUPS - N_EXPERTS)))
    rw_hi = rw.astype(BF16)
    rw_lo = (rw - rw_hi.astype(F32)).astype(BF16)
    f["rw"] = jnp.stack([rw_hi, rw_lo], axis=1)
    rb = jnp.concatenate([p["router_group_b"], p["router_expert_b"]], axis=-1)
    f["rb"] = jnp.pad(rb, ((0, 0), (0, ROUTE_LANES - N_GROUPS - N_EXPERTS)))[:, None, :]
    f["w1"] = p["expert_w1"]
    f["w3"] = p["expert_w3"]
    f["w2"] = p["expert_w2"]
    return f


def _const_tables():
    c = np.arange(FNET_W)
    same = (c[:, None] // FNET_GROUP_DIM) == (c[None, :] // FNET_GROUP_DIM)
    ang = 2.0 * np.pi * ((c[:, None] % FNET_GROUP_DIM) * (c[None, :] % FNET_GROUP_DIM) % FNET_GROUP_DIM) / FNET_GROUP_DIM
    dft64 = np.stack([np.where(same, np.cos(ang), 0.0), np.where(same, np.sin(ang), 0.0)]).astype(np.float32)
    vone = np.zeros((MLA_HEADS, 1, HEAD_PAD), np.float32)
    vone[0::2, 0, MLA_V] = 1.0
    vone[1::2, 0, 0] = 1.0
    return jnp.asarray(dft64, BF16), jnp.asarray(vone)


def _seq_tables(S):
    j = lax.broadcasted_iota(jnp.int32, (S, S), 0)
    k = lax.broadcasted_iota(jnp.int32, (S, S), 1)
    ang = ((j * k) % S).astype(F32) * (2.0 * math.pi / S)
    a_dft = jnp.concatenate([jnp.cos(ang), -jnp.sin(ang)], axis=1).astype(BF16)
    half = MLA_ROPE // 2
    freqs = ROPE_THETA ** (-jnp.arange(half, dtype=F32) / half)
    ra = jnp.arange(S, dtype=F32)[:, None] * freqs[None, :]
    cos, sin = jnp.cos(ra), jnp.sin(ra)
    one = lambda n: jnp.ones((S, n), F32)
    zero = lambda n: jnp.zeros((S, n), F32)
    tail = HEAD_PAD - MLA_QK
    rope = jnp.stack([
        jnp.concatenate([one(MLA_NOPE), cos, cos, one(tail)], axis=1),
        jnp.concatenate([zero(MLA_NOPE), -sin, sin, zero(tail)], axis=1),
    ])
    return a_dft, rope


def _dispatch_plan(route2d, counts):
    T = route2d.shape[0]
    R = T * TOP_K
    cnt = counts[0, :N_EXPERTS].astype(jnp.int32)
    padded = (cnt + MOE_ROWS - 1) // MOE_ROWS * MOE_ROWS
    pend = jnp.cumsum(padded)
    pstart = pend - padded
    start = jnp.zeros((1, ROUTE_LANES), F32).at[0, :N_EXPERTS].set(pstart.astype(F32))
    dest = route_rank(route2d, start)
    P = R + N_EXPERTS * MOE_ROWS
    n_blk = P // MOE_ROWS
    blk_first = jnp.arange(n_blk, dtype=jnp.int32) * MOE_ROWS
    blk_e = jnp.minimum(jnp.sum((pend[None, :] <= blk_first[:, None]).astype(jnp.int32), axis=1), N_EXPERTS - 1)
    n_used = (pend[-1] // MOE_ROWS).astype(jnp.int32).reshape(1)
    return dest, blk_e, n_used, P


def _trunk(x, mod_all, f, consts, tables):
    B, S, D = x.shape
    T = B * S
    dft64, vone = consts
    a_dft, rope = tables
    tm = min(S, 512)
    tq = min(S, 512)
    tk = min(S, 1024)
    dft_scale = 1.0 / math.sqrt(FNET_GROUP_DIM * S)
    moe = None
    for l in range(DEPTH):
        mod = mod_all[l].reshape(B, 6, D)
        outs = in_proj(x, mod, f["norm1_g"][l], f["w_in"][l], dft64, f["gate_up"][l], f["gate_bias"][l],
                       f["lora_g"][l], tm, moe)
        if moe is not None:
            x, outs = outs[0], outs[1:]
        qkv, og, gate, xcs, cq, ckv, kr = outs
        o_gla = gla(qkv, og, gate, f["gla_norm_g"][l])
        o_fft = dft_seq(a_dft, xcs.reshape(2 * S, B * FNET_W), dft_scale)
        o_mla = mla(f["mla_flag"][l], cq, ckv, kr, rope, f["wq"][l], f["wk"][l], f["wv"][l], f["qg"][l], f["kg"][l],
                    vone, f["mla_shift"][l], tq, tk)
        x, h, route, counts = out_proj(x, o_gla, o_fft, o_mla, mod, f["w_out"][l], f["norm2_g"][l], f["rw"][l],
                                       f["rb"][l], tm)
        dest, blk_e, n_used, n_rows = _dispatch_plan(route.reshape(T, ROUTE_LANES), counts)
        xb = sc_scatter_rows(h.reshape(T, D // 2), dest.reshape(TOP_K * T), n_rows)
        yb = moe_mlp(xb, blk_e, n_used, f["w1"], f["w3"], f["w2"], l)
        ya = sc_gather_rows(yb, dest[0]).reshape(B, S, D // 2)
        yc = sc_gather_rows(yb, dest[1]).reshape(B, S, D // 2)
        moe = (ya, yc, route, mod)
    return combine(x, *moe, tm)


def kernel(x_prompt, x_sample, c_prompt, c_sample, ada_w, ada_b, norm1_g, norm2_g, w_in, w_out, gla_gate_up_f, gla_gate_bias_f, gla_gate_up_b, gla_gate_bias_b, gla_out_norm_g, mla_q_lora_norm_g, mla_w_uq, mla_kv_lora_norm_g, mla_w_ukv, mla_q_norm_g, mla_k_norm_g, router_group_w, router_group_b, router_expert_w, router_expert_b, expert_w1, expert_w3, expert_w2):
    p = dict(norm1_g=norm1_g, norm2_g=norm2_g, w_in=w_in, w_out=w_out, gla_gate_up_f=gla_gate_up_f,
             gla_gate_bias_f=gla_gate_bias_f, gla_gate_up_b=gla_gate_up_b, gla_gate_bias_b=gla_gate_bias_b,
             gla_out_norm_g=gla_out_norm_g, mla_q_lora_norm_g=mla_q_lora_norm_g, mla_w_uq=mla_w_uq,
             mla_kv_lora_norm_g=mla_kv_lora_norm_g, mla_w_ukv=mla_w_ukv, mla_q_norm_g=mla_q_norm_g,
             mla_k_norm_g=mla_k_norm_g, router_group_w=router_group_w, router_group_b=router_group_b,
             router_expert_w=router_expert_w, router_expert_b=router_expert_b, expert_w1=expert_w1,
             expert_w3=expert_w3, expert_w2=expert_w2)
    f = _prep_params(p)
    consts = _const_tables()
    nb_p = c_prompt.shape[0]
    mod_all = ada_modulation(jnp.concatenate([c_prompt, c_sample], axis=0), ada_w, ada_b)
    y_prompt = _trunk(x_prompt, mod_all[:, :nb_p], f, consts, _seq_tables(x_prompt.shape[1]))
    y_sample = _trunk(x_sample, mod_all[:, nb_p:], f, consts, _seq_tables(x_sample.shape[1]))
    return (y_prompt, y_sample)
```

```python
import functools
import math

import jax
import jax.numpy as jnp
import numpy as np
from jax import lax
from jax.experimental import pallas as pl
from jax.experimental.pallas import tpu as pltpu
from jax.experimental.pallas import tpu_sc as plsc

F32 = jnp.float32
BF16 = jnp.bfloat16

D_MODEL = 1024
DEPTH = 4
EPS = 1e-6

GLA_HEADS = 4
GLA_DK = 32
GLA_DV = 64
GLA_QK_W = GLA_HEADS * GLA_DK
GLA_W = GLA_HEADS * GLA_DV
GLA_GATE_RANK = 16
GLA_GATE_NORMALIZER = 16.0
GLA_CHUNK = 32
GLA_BLOCK = 128

FNET_GROUPS = 4
FNET_GROUP_DIM = 64
FNET_W = 256

MLA_HEADS = 8
MLA_Q_LORA = 256
MLA_KV_LORA = 128
MLA_NOPE = 64
MLA_ROPE = 32
MLA_V = 64
MLA_QK = 96
MLA_W = 512
ROPE_THETA = 10000.0
HEAD_PAD = 128

N_GROUPS = 4
EXPERTS_PER_GROUP = 8
N_EXPERTS = 32
TOP_K = 2
D_EXPERT = 512
ROUTE_LANES = 128

C_Q, C_K, C_V, C_OG, C_F, C_CQ, C_CKV, C_KR, C_GATE = 0, 128, 256, 512, 768, 1024, 1280, 1408, 1536
P_IN_PAD = 1664

VMEM_LIMIT_BYTES = 56 * 1024 * 1024


def _cparams(sem, vmem=None):
    return pltpu.CompilerParams(dimension_semantics=sem, vmem_limit_bytes=vmem or VMEM_LIMIT_BYTES)


def _silu(x):
    return x * (1.0 / (1.0 + jnp.exp(-x)))


def _log_sigmoid(x):
    return -(jnp.maximum(-x, 0.0) + jnp.log1p(jnp.exp(-jnp.abs(x))))


def _dot(a, b):
    return jnp.dot(a, b, preferred_element_type=F32)


def _dot_nt(a, b):
    return lax.dot_general(a, b, (((1,), (1,)), ((), ())), preferred_element_type=F32)


def _split_bf16(x):
    hi = x.astype(BF16)
    lo = (x - hi.astype(F32)).astype(BF16)
    return hi, lo


def _pack_halves(x):
    n = x.shape[-1] // 2
    lo = lax.shift_right_logical(lax.bitcast_convert_type(x[:, :n], jnp.int32), 16)
    hi = lax.bitcast_convert_type(x[:, n:], jnp.int32) & jnp.int32(-65536)
    return hi | lo


def _unpack_halves(w):
    lo = lax.bitcast_convert_type(lax.shift_left(w, 16), F32)
    hi = lax.bitcast_convert_type(w & jnp.int32(-65536), F32)
    return lo, hi


def _ada_kernel(c_ref, w_ref, b_ref, o_ref):
    c = _silu(c_ref[...]).astype(BF16)
    o_ref[0] = _dot(c, w_ref[0].astype(BF16)) + b_ref[0]


def ada_modulation(c_all, ada_w, ada_b):
    nb = c_all.shape[0]
    tn = 1536
    n = ada_w.shape[-1]
    return pl.pallas_call(
        _ada_kernel,
        grid=(DEPTH, n // tn),
        in_specs=[
            pl.BlockSpec((nb, D_MODEL), lambda l, j: (0, 0)),
            pl.BlockSpec((1, D_MODEL, tn), lambda l, j: (l, 0, j)),
            pl.BlockSpec((1, 1, tn), lambda l, j: (l, 0, j)),
        ],
        out_specs=pl.BlockSpec((1, nb, tn), lambda l, j: (l, 0, j)),
        out_shape=jax.ShapeDtypeStruct((DEPTH, nb, n), F32),
        compiler_params=_cparams(("parallel", "parallel")),
        name="ada_modulation",
    )(c_all, ada_w, ada_b.reshape(DEPTH, 1, n))


def _in_proj_body(x, mod_ref, g_ref, w_ref, dft_ref, up_ref, gb_ref, lg_ref,
                  qkv_ref, og_ref, gate_ref, xcs_ref, cq_ref, ckv_ref, kr_ref):
    sh = mod_ref[0, 0:1, :]
    sc = mod_ref[0, 1:2, :]
    r = lax.rsqrt(jnp.mean(x * x, axis=-1, keepdims=True) + EPS)
    h = (x * r * g_ref[...]) * (1.0 + sc) + sh
    u = _dot(h.astype(BF16), w_ref[...])

    q = u[:, C_Q:C_K] * (GLA_DK ** -0.5)
    qkv_ref[:, 0:128] = q.astype(BF16)
    qkv_ref[:, 128:512] = u[:, C_K:C_OG].astype(BF16)
    og_ref[...] = u[:, C_OG:C_F].astype(BF16)

    ug = u[:, C_GATE:C_GATE + 128].astype(BF16)
    gl = _dot(ug, up_ref[...]) + gb_ref[...]
    gate_ref[...] = _log_sigmoid(gl) * (1.0 / GLA_GATE_NORMALIZER)

    uf = u[:, C_F:C_CQ].astype(BF16)
    xcs_ref[0] = _dot(uf, dft_ref[0]).astype(BF16)
    xcs_ref[1] = _dot(uf, dft_ref[1]).astype(BF16)

    cq = u[:, C_CQ:C_CKV]
    rq = lax.rsqrt(jnp.mean(cq * cq, axis=-1, keepdims=True) + EPS)
    cq_ref[...] = (cq * rq * lg_ref[:, 0:256]).astype(BF16)
    ckv = u[:, C_CKV:C_KR]
    rkv = lax.rsqrt(jnp.mean(ckv * ckv, axis=-1, keepdims=True) + EPS)
    ckv_ref[...] = (ckv * rkv * lg_ref[:, 256:384]).astype(BF16)
    kr_ref[...] = u[:, C_KR:C_GATE].astype(BF16)


def _in_proj_kernel(x_ref, mod_ref, g_ref, w_ref, dft_ref, up_ref, gb_ref, lg_ref,
                    qkv_ref, og_ref, gate_ref, xcs_ref, cq_ref, ckv_ref, kr_ref):
    _in_proj_body(x_ref[0], mod_ref, g_ref, w_ref, dft_ref, up_ref, gb_ref, lg_ref,
                  qkv_ref.at[0], og_ref.at[0], gate_ref.at[0], xcs_ref, cq_ref.at[0], ckv_ref.at[0], kr_ref.at[0])


def _in_proj_moe_kernel(x_ref, ya_ref, yc_ref, route_ref, modp_ref, mod_ref, g_ref, w_ref, dft_ref, up_ref, gb_ref,
                        lg_ref, xo_ref, qkv_ref, og_ref, gate_ref, xcs_ref, cq_ref, ckv_ref, kr_ref):
    x = _moe_residual(x_ref[0], ya_ref[0], yc_ref[0], route_ref[0], modp_ref)
    xo_ref[0] = x
    _in_proj_body(x, mod_ref, g_ref, w_ref, dft_ref, up_ref, gb_ref, lg_ref,
                  qkv_ref.at[0], og_ref.at[0], gate_ref.at[0], xcs_ref, cq_ref.at[0], ckv_ref.at[0], kr_ref.at[0])


def in_proj(x, mod, norm_g, w_in, dft64, gate_up, gate_bias, lora_g, tm, moe=None):
    B, S, D = x.shape
    ns = S // tm
    row = lambda b, s: (b, s, 0)
    const2 = lambda b, s: (0, 0)
    modmap = lambda b, s: (b, 0, 0)
    if moe is None:
        body, lead_in, lead_specs, lead_out_specs, lead_out_shapes = _in_proj_kernel, (x,), [pl.BlockSpec((1, tm, D), row)], (), ()
    else:
        ya, yc, route, mod_prev = moe
        body = _in_proj_moe_kernel
        lead_in = (x, ya, yc, route, mod_prev)
        lead_specs = [pl.BlockSpec((1, tm, D), row), pl.BlockSpec((1, tm, D // 2), row),
                      pl.BlockSpec((1, tm, D // 2), row), pl.BlockSpec((1, tm, ROUTE_LANES), row),
                      pl.BlockSpec((1, 6, D), modmap)]
        lead_out_specs = (pl.BlockSpec((1, tm, D), row),)
        lead_out_shapes = (jax.ShapeDtypeStruct((B, S, D), F32),)
    out_shapes = lead_out_shapes + (
        jax.ShapeDtypeStruct((B, S, 512), BF16),
        jax.ShapeDtypeStruct((B, S, GLA_W), BF16),
        jax.ShapeDtypeStruct((B, S, 256), F32),
        jax.ShapeDtypeStruct((2, S, B * FNET_W), BF16),
        jax.ShapeDtypeStruct((B, S, MLA_Q_LORA), BF16),
        jax.ShapeDtypeStruct((B, S, MLA_KV_LORA), BF16),
        jax.ShapeDtypeStruct((B, S, HEAD_PAD), BF16),
    )
    return pl.pallas_call(
        body,
        grid=(B, ns),
        in_specs=lead_specs + [
            pl.BlockSpec((1, 6, D), modmap),
            pl.BlockSpec((1, D), const2),
            pl.BlockSpec((D, P_IN_PAD), const2),
            pl.BlockSpec((2, FNET_W, FNET_W), lambda b, s: (0, 0, 0)),
            pl.BlockSpec((128, 256), const2),
            pl.BlockSpec((1, 256), const2),
            pl.BlockSpec((1, 384), const2),
        ],
        out_specs=lead_out_specs + (
            pl.BlockSpec((1, tm, 512), row),
            pl.BlockSpec((1, tm, GLA_W), row),
            pl.BlockSpec((1, tm, 256), row),
            pl.BlockSpec((2, tm, FNET_W), lambda b, s: (0, s, b)),
            pl.BlockSpec((1, tm, MLA_Q_LORA), row),
            pl.BlockSpec((1, tm, MLA_KV_LORA), row),
            pl.BlockSpec((1, tm, HEAD_PAD), row),
        ),
        out_shape=out_shapes,
        compiler_params=_cparams(("parallel", "parallel")),
        name="in_proj",
    )(*lead_in, mod, norm_g, w_in, dft64, gate_up, gate_bias, lora_g)


def _gla_kernel(qkv_ref, og_ref, gate_ref, ng_ref, o_ref, o_s, st_s, *, seq):
    R, C = GLA_BLOCK, GLA_CHUNK
    n_sub = R // C
    nblk = seq // R
    shift_c = int(math.log2(C))

    row = lax.broadcasted_iota(jnp.int32, (R, R), 0)
    col = lax.broadcasted_iota(jnp.int32, (R, R), 1)
    same = (row >> shift_c) == (col >> shift_c)
    ones_blk = jnp.where(same, 1.0, 0.0)
    lane_qk = lax.broadcasted_iota(jnp.int32, (1, GLA_QK_W), 1)
    lane_v = lax.broadcasted_iota(jnp.int32, (1, GLA_W), 1)
    head_qk = [(lane_qk >> 5) == h for h in range(GLA_HEADS)]
    head_v = [(lane_v >> 6) == h for h in range(GLA_HEADS)]
    st_row = lax.broadcasted_iota(jnp.int32, (GLA_W, GLA_QK_W), 0)
    st_col = lax.broadcasted_iota(jnp.int32, (GLA_W, GLA_QK_W), 1)
    st_mask = (st_row >> 6) == (st_col >> 5)
    sub_row = lax.broadcasted_iota(jnp.int32, (R, 1), 0) >> shift_c
    nrow = lax.broadcasted_iota(jnp.int32, (GLA_W, GLA_W), 0)
    ncol = lax.broadcasted_iota(jnp.int32, (GLA_W, GLA_W), 1)
    head_mean = jnp.where((nrow >> 6) == (ncol >> 6), 1.0 / GLA_DV, 0.0).astype(BF16)

    def direction(backward):
        if backward:
            tri = same & (col >= row)
            att_ok = same & (col > row)
            order = list(range(n_sub - 1, -1, -1))
        else:
            tri = same & (col <= row)
            att_ok = same & (col <= row)
            order = list(range(n_sub))
        cum_lhs = jnp.concatenate([jnp.where(tri, 1.0, 0.0), ones_blk], axis=0).astype(BF16)
        att_ok4 = jnp.concatenate([att_ok] * GLA_HEADS, axis=1)
        return cum_lhs, att_ok4, order, (GLA_QK_W if backward else 0), (1 if backward else 0)

    def block(j, consts):
        cum_lhs, att_ok4, order, gate_off, slot = consts
        r0 = pl.multiple_of(j * R, R)
        qkv = qkv_ref[0, pl.ds(r0, R), :]
        q = qkv[:, 0:128].astype(F32)
        k = qkv[:, 128:256].astype(F32)
        v = qkv[:, 256:512]
        g = gate_ref[0, pl.ds(r0, R), gate_off:gate_off + GLA_QK_W]
        g_hi, g_lo = _split_bf16(g)
        cs = _dot(cum_lhs, jnp.concatenate([g_hi, g_lo], axis=1))
        b = cs[0:R, 0:128] + cs[0:R, 128:256]
        bl = cs[R:2 * R, 0:128] + cs[R:2 * R, 128:256]
        q_dec = q * jnp.exp(b)
        k_inv = (k * jnp.exp(-b)).astype(BF16)
        k_end = k * jnp.exp(bl - b)
        decay = jnp.exp(bl)

        att = jnp.concatenate(
            [_dot_nt(jnp.where(head_qk[h], q_dec, 0.0).astype(BF16), k_inv) for h in range(GLA_HEADS)],
            axis=1)
        att = jnp.where(att_ok4, att, 0.0).astype(BF16)
        v_heads = jnp.concatenate([jnp.where(head_v[h], v, jnp.zeros_like(v)) for h in range(GLA_HEADS)], axis=0)
        o_blk = _dot(att, v_heads)

        v_t = v.astype(F32).T.astype(BF16)
        q_dec_b = q_dec.astype(BF16)
        inter = [None] * n_sub
        for c in order:
            st = st_s[slot]
            inter[c] = _dot_nt(q_dec_b[c * C:(c + 1) * C, :], st.astype(BF16))
            k_c = jnp.where(sub_row == c, k_end, 0.0).astype(BF16)
            d_st = _dot(v_t, k_c)
            st_s[slot] = st * decay[c * C:c * C + 1, :] + jnp.where(st_mask, d_st, 0.0)
        o_s[slot, pl.ds(r0, R), :] = o_blk + jnp.concatenate(inter, axis=0)

    fwd = direction(False)
    bwd = direction(True)
    st_s[...] = jnp.zeros_like(st_s)

    def scan_step(i, carry):
        block(i, fwd)
        block(nblk - 1 - i, bwd)
        return carry

    lax.fori_loop(0, nblk, scan_step, 0)

    def finish(j, carry):
        r0 = pl.multiple_of(j * R, R)
        o = o_s[0, pl.ds(r0, R), :] + o_s[1, pl.ds(r0, R), :]
        sq_hi, sq_lo = _split_bf16(o * o)
        ms = _dot(sq_hi, head_mean) + _dot(sq_lo, head_mean)
        y = o * lax.rsqrt(ms + EPS) * ng_ref[...]
        og = og_ref[0, pl.ds(r0, R), :].astype(F32)
        o_ref[0, pl.ds(r0, R), :] = (y * _silu(og)).astype(o_ref.dtype)
        return carry

    lax.fori_loop(0, nblk, finish, 0)


def gla(qkv, og, gate, norm_g4):
    B, S, _ = qkv.shape
    full = lambda b: (b, 0, 0)
    return pl.pallas_call(
        functools.partial(_gla_kernel, seq=S),
        grid=(B,),
        in_specs=[
            pl.BlockSpec((1, S, 512), full),
            pl.BlockSpec((1, S, GLA_W), full),
            pl.BlockSpec((1, S, 256), full),
            pl.BlockSpec((1, GLA_W), lambda b: (0, 0)),
        ],
        out_specs=pl.BlockSpec((1, S, GLA_W), full),
        out_shape=jax.ShapeDtypeStruct((B, S, GLA_W), BF16),
        scratch_shapes=[pltpu.VMEM((2, S, GLA_W), F32), pltpu.VMEM((2, GLA_W, GLA_QK_W), F32)],
        compiler_params=_cparams(("parallel",)),
        name="gla",
    )(qkv, og, gate, norm_g4)


def _dft_kernel(a_ref, b_ref, o_ref, acc_ref, *, scale):
    kk = pl.program_id(2)

    @pl.when(kk == 0)
    def _():
        acc_ref[...] = jnp.zeros_like(acc_ref)

    acc_ref[...] += _dot(a_ref[...], b_ref[...])

    @pl.when(kk == pl.num_programs(2) - 1)
    def _():
        o_ref[...] = (acc_ref[...] * scale).astype(o_ref.dtype)


def dft_seq(a_dft, xcs, scale):
    m, k = a_dft.shape
    n = xcs.shape[1]
    tm, tn, tk = min(m, 1024), min(n, 1024), min(k, 2048)
    return pl.pallas_call(
        functools.partial(_dft_kernel, scale=scale),
        grid=(m // tm, n // tn, k // tk),
        in_specs=[pl.BlockSpec((tm, tk), lambda i, j, kk: (i, kk)),
                  pl.BlockSpec((tk, tn), lambda i, j, kk: (kk, j))],
        out_specs=pl.BlockSpec((tm, tn), lambda i, j, kk: (i, j)),
        out_shape=jax.ShapeDtypeStruct((m, n), BF16),
        scratch_shapes=[pltpu.VMEM((tm, tn), F32)],
        compiler_params=_cparams(("parallel", "parallel", "arbitrary")),
        name="dft_seq",
    )(a_dft, xcs)


def _rope(x, cos, sin, swap):
    return x * cos + _dot(x.astype(BF16), swap) * sin


SHIFT_LANE = HEAD_PAD - 1
MAX_STATIC_SHIFT = 40.0
LOG2E = math.log2(math.e)


def _mla_kernel(flag_ref, cq_ref, ckv_ref, kr_ref, rope_ref, wq_ref, wk_ref, wv_ref, qg_ref, kg_ref, vone_ref,
                shift_ref, o_ref, kt_s, v_s, q_s, acc_s, *, seq, tq, tk):
    qi = pl.program_id(1)
    inv_qk = 1.0 / MLA_QK
    lane = lax.broadcasted_iota(jnp.int32, (1, HEAD_PAD), 1)
    src = lax.broadcasted_iota(jnp.int32, (HEAD_PAD, HEAD_PAD), 0)
    dst = lax.broadcasted_iota(jnp.int32, (HEAD_PAD, HEAD_PAD), 1)
    half = MLA_ROPE // 2
    lo_half = (dst >= MLA_NOPE) & (dst < MLA_NOPE + half)
    hi_half = (dst >= MLA_NOPE + half) & (dst < MLA_QK)
    swap = jnp.where((lo_half & (src == dst + half)) | (hi_half & (src == dst - half)), 1.0, 0.0).astype(BF16)

    @pl.when(qi == 0)
    def _():
        one_lane = jnp.where(lane == SHIFT_LANE, 1.0, 0.0)

        def rows(i, carry):
            r0 = pl.multiple_of(i * tk, tk)
            ckv = ckv_ref[0, pl.ds(r0, tk), :]
            kr = kr_ref[0, pl.ds(r0, tk), :].astype(F32)
            cos = rope_ref[0, pl.ds(r0, tk), :]
            sin = rope_ref[1, pl.ds(r0, tk), :]
            for h in range(MLA_HEADS):
                kp = _dot(ckv, wk_ref[h]) + kr
                r = lax.rsqrt(jnp.sum(kp * kp, axis=-1, keepdims=True) * inv_qk + EPS)
                kn = _rope(kp * r * kg_ref[...], cos, sin, swap) + one_lane
                kt_s[h, i] = kn.T.astype(BF16)
                v_s[h, pl.ds(r0, tk), :] = (_dot(ckv, wv_ref[h]) + vone_ref[h]).astype(BF16)
            return carry

        lax.fori_loop(0, seq // tk, rows, 0)

    q0 = pl.multiple_of(qi * tq, tq)
    cos = rope_ref[0, pl.ds(q0, tq), :]
    sin = rope_ref[1, pl.ds(q0, tq), :]
    for h in range(MLA_HEADS):
        qp = _dot(cq_ref[0], wq_ref[h])
        r = lax.rsqrt(jnp.sum(qp * qp, axis=-1, keepdims=True) * inv_qk + EPS)
        q_s[h] = (_rope(qp * r * qg_ref[...], cos, sin, swap) + shift_ref[...]).astype(BF16)

    def finish():
        for pair in range(MLA_HEADS // 2):
            outs = []
            for h in (2 * pair, 2 * pair + 1):
                acc = acc_s[h]
                den_lane = MLA_V if h % 2 == 0 else 0
                den = jnp.sum(jnp.where(lane == den_lane, acc, 0.0), axis=-1, keepdims=True)
                outs.append(acc * (1.0 / den))
            both = jnp.where(lane < MLA_V, outs[0], outs[1])
            o_ref[0, :, pair * HEAD_PAD:(pair + 1) * HEAD_PAD] = both.astype(o_ref.dtype)

    @pl.when(flag_ref[0] == 1)
    def _():
        acc_s[...] = jnp.zeros_like(acc_s)

        def kv_step(j, carry):
            k0 = pl.multiple_of(j * tk, tk)
            for h in range(MLA_HEADS):
                p = jnp.exp2(_dot(q_s[h], kt_s[h, j])).astype(BF16)
                acc_s[h] += _dot(p, v_s[h, pl.ds(k0, tk), :])
            return carry

        lax.fori_loop(0, seq // tk, kv_step, 0)
        finish()

    @pl.when(flag_ref[0] == 0)
    def _():
        for h in range(MLA_HEADS):
            def kv_step(j, carry, h=h):
                m, acc = carry
                k0 = pl.multiple_of(j * tk, tk)
                s = _dot(q_s[h], kt_s[h, j])
                m_new = jnp.maximum(m, jnp.max(s, axis=-1, keepdims=True))
                p = jnp.exp2(s - m_new).astype(BF16)
                return m_new, jnp.exp2(m - m_new) * acc + _dot(p, v_s[h, pl.ds(k0, tk), :])

            m0 = jnp.full((tq, 1), -jnp.inf, F32)
            _, acc = lax.fori_loop(0, seq // tk, kv_step, (m0, jnp.zeros((tq, HEAD_PAD), F32)))
            acc_s[h] = acc
        finish()


def mla(flag, cq, ckv, kr, rope, wq, wk, wv, qg, kg, vone, shift, tq, tk):
    B, S, _ = cq.shape
    c3 = lambda b, q: (0, 0, 0)
    c2 = lambda b, q: (0, 0)
    return pl.pallas_call(
        functools.partial(_mla_kernel, seq=S, tq=tq, tk=tk),
        grid=(B, S // tq),
        in_specs=[
            pl.BlockSpec(memory_space=pltpu.SMEM),
            pl.BlockSpec((1, tq, MLA_Q_LORA), lambda b, q: (b, q, 0)),
            pl.BlockSpec((1, S, MLA_KV_LORA), lambda b, q: (b, 0, 0)),
            pl.BlockSpec((1, S, HEAD_PAD), lambda b, q: (b, 0, 0)),
            pl.BlockSpec((2, S, HEAD_PAD), c3),
            pl.BlockSpec((MLA_HEADS, MLA_Q_LORA, HEAD_PAD), c3),
            pl.BlockSpec((MLA_HEADS, MLA_KV_LORA, HEAD_PAD), c3),
            pl.BlockSpec((MLA_HEADS, MLA_KV_LORA, HEAD_PAD), c3),
            pl.BlockSpec((1, HEAD_PAD), c2),
            pl.BlockSpec((1, HEAD_PAD), c2),
            pl.BlockSpec((MLA_HEADS, 1, HEAD_PAD), c3),
            pl.BlockSpec((1, HEAD_PAD), c2),
        ],
        out_specs=pl.BlockSpec((1, tq, MLA_W), lambda b, q: (b, q, 0)),
        out_shape=jax.ShapeDtypeStruct((B, S, MLA_W), BF16),
        scratch_shapes=[pltpu.VMEM((MLA_HEADS, S // tk, HEAD_PAD, tk), BF16),
                        pltpu.VMEM((MLA_HEADS, S, HEAD_PAD), BF16),
                        pltpu.VMEM((MLA_HEADS, tq, HEAD_PAD), BF16),
                        pltpu.VMEM((MLA_HEADS, tq, HEAD_PAD), F32)],
        compiler_params=_cparams(("parallel", "arbitrary")),
        name="mla",
    )(flag, cq, ckv, kr, rope, wq, wk, wv, qg, kg, vone, shift)


def _out_proj_kernel(x_ref, gla_ref, fft_ref, mla_ref, mod_ref, w_ref, g_ref, rw_ref, rb_ref,
                     xo_ref, h_ref, route_ref, cnt_ref):
    mix = (_dot(gla_ref[0], w_ref[0:256, :]) + _dot(fft_ref[...], w_ref[256:512, :])
           + _dot(mla_ref[0], w_ref[512:1024, :]))
    g1 = mod_ref[0, 2:3, :]
    sh = mod_ref[0, 3:4, :]
    sc = mod_ref[0, 4:5, :]
    x = x_ref[0] + g1 * mix
    xo_ref[0] = x
    r = lax.rsqrt(jnp.mean(x * x, axis=-1, keepdims=True) + EPS)
    h = (x * r * g_ref[...]) * (1.0 + sc) + sh
    h_hi, h_lo = _split_bf16(h)
    h_ref[0] = _pack_halves(h_hi.astype(F32))

    logit = _dot(h_hi, rw_ref[0]) + _dot(h_lo, rw_ref[0]) + _dot(h_hi, rw_ref[1]) + rb_ref[...]
    lane = lax.broadcasted_iota(jnp.int32, (1, ROUTE_LANES), 1)
    lane_f = lane.astype(F32)
    neg = -1e30
    is_g = lane < N_GROUPS
    is_e = (lane >= N_GROUPS) & (lane < N_GROUPS + N_EXPERTS)
    lg = jnp.where(is_g, logit, neg)
    g_max = jnp.max(lg, axis=-1, keepdims=True)
    g_den = jnp.sum(jnp.where(is_g, jnp.exp(lg - g_max), 0.0), axis=-1, keepdims=True)
    g_w = 1.0 / g_den
    g_top = jnp.min(jnp.where(is_g & (logit == g_max), lane_f, 1e9), axis=-1, keepdims=True)
    e_grp = ((lane - N_GROUPS) >> 3).astype(F32)
    in_grp = is_e & (e_grp == g_top)
    le = jnp.where(in_grp, logit, neg)
    t1 = jnp.max(le, axis=-1, keepdims=True)
    i1 = jnp.min(jnp.where(in_grp & (le == t1), lane_f, 1e9), axis=-1, keepdims=True)
    le2 = jnp.where(lane_f == i1, neg, le)
    t2 = jnp.max(le2, axis=-1, keepdims=True)
    i2 = jnp.min(jnp.where(in_grp & (le2 == t2), lane_f, 1e9), axis=-1, keepdims=True)
    e21 = jnp.exp(t2 - t1)
    w1 = g_w / (1.0 + e21)
    w2 = w1 * e21
    route = jnp.where(lane == 0, i1 - N_GROUPS,
                      jnp.where(lane == 1, i2 - N_GROUPS,
                                jnp.where(lane == 2, w1, jnp.where(lane == 3, w2, 0.0))))
    route_ref[0] = route

    @pl.when((pl.program_id(0) == 0) & (pl.program_id(1) == 0))
    def _():
        cnt_ref[...] = jnp.zeros_like(cnt_ref)

    picked = jnp.where((lane_f == i1 - N_GROUPS) | (lane_f == i2 - N_GROUPS), 1.0, 0.0)
    cnt_ref[...] += jnp.sum(picked, axis=0, keepdims=True)


def out_proj(x, o_gla, o_fft, o_mla, mod, w_out, norm_g, rw, rb, tm):
    B, S, D = x.shape
    row = lambda b, s: (b, s, 0)
    c2 = lambda b, s: (0, 0)
    return pl.pallas_call(
        _out_proj_kernel,
        grid=(B, S // tm),
        in_specs=[
            pl.BlockSpec((1, tm, D), row),
            pl.BlockSpec((1, tm, GLA_W), row),
            pl.BlockSpec((tm, FNET_W), lambda b, s: (s, b)),
            pl.BlockSpec((1, tm, MLA_W), row),
            pl.BlockSpec((1, 6, D), lambda b, s: (b, 0, 0)),
            pl.BlockSpec((D, D), c2),
            pl.BlockSpec((1, D), c2),
            pl.BlockSpec((2, D, ROUTE_LANES), lambda b, s: (0, 0, 0)),
            pl.BlockSpec((1, ROUTE_LANES), c2),
        ],
        out_specs=(pl.BlockSpec((1, tm, D), row), pl.BlockSpec((1, tm, D // 2), row),
                   pl.BlockSpec((1, tm, ROUTE_LANES), row), pl.BlockSpec((1, ROUTE_LANES), c2)),
        out_shape=(jax.ShapeDtypeStruct((B, S, D), F32), jax.ShapeDtypeStruct((B, S, D // 2), jnp.int32),
                   jax.ShapeDtypeStruct((B, S, ROUTE_LANES), F32), jax.ShapeDtypeStruct((1, ROUTE_LANES), F32)),
        compiler_params=_cparams(("arbitrary", "arbitrary")),
        name="out_proj",
    )(x, o_gla, o_fft, o_mla, mod, w_out, norm_g, rw, rb)


RANK_ROWS = 1024


def _rank_kernel(route_ref, start_ref, dest_ref, carry_ref, before_ref):
    n = route_ref.shape[0]

    @pl.when(pl.program_id(0) == 0)
    def _():
        carry_ref[...] = jnp.zeros_like(carry_ref)
        row = lax.broadcasted_iota(jnp.int32, (n, n), 0)
        col = lax.broadcasted_iota(jnp.int32, (n, n), 1)
        before_ref[...] = jnp.where(col < row, 1.0, 0.0).astype(BF16)

    lane_f = lax.broadcasted_iota(jnp.int32, (1, ROUTE_LANES), 1).astype(F32)
    e1 = route_ref[:, 0:1]
    e2 = route_ref[:, 1:2]
    oh1 = jnp.where(lane_f == e1, 1.0, 0.0)
    oh2 = jnp.where(lane_f == e2, 1.0, 0.0)
    both = (oh1 + oh2).astype(BF16)
    pos = _dot(before_ref[...], both) + carry_ref[...] + start_ref[...]
    d1 = jnp.sum(oh1 * pos, axis=-1, keepdims=True)
    d2 = jnp.sum(oh2 * pos, axis=-1, keepdims=True)
    lane = lax.broadcasted_iota(jnp.int32, (1, ROUTE_LANES), 1)
    cols = jnp.where(lane == 0, d1, jnp.where(lane == 1, d2, 0.0))
    dest_ref[...] = cols.T[0:TOP_K, :].astype(jnp.int32)
    carry_ref[...] += jnp.sum(oh1 + oh2, axis=0, keepdims=True)


def route_rank(route2d, start):
    T = route2d.shape[0]
    rr = min(T, RANK_ROWS)
    return pl.pallas_call(
        _rank_kernel,
        grid=(T // rr,),
        in_specs=[pl.BlockSpec((rr, ROUTE_LANES), lambda i: (i, 0)),
                  pl.BlockSpec((1, ROUTE_LANES), lambda i: (0, 0))],
        out_specs=pl.BlockSpec((TOP_K, rr), lambda i: (0, i)),
        out_shape=jax.ShapeDtypeStruct((TOP_K, T), jnp.int32),
        scratch_shapes=[pltpu.VMEM((1, ROUTE_LANES), F32), pltpu.VMEM((rr, rr), BF16)],
        compiler_params=_cparams(("arbitrary",)),
        name="route_rank",
    )(route2d, start)


MOE_ROWS = 512


def _moe_kernel(blk_e_ref, n_used_ref, x_ref, w1_ref, w3_ref, w2_ref, o_ref):
    i = pl.program_id(0)

    @pl.when(i < n_used_ref[0])
    def _():
        half = D_MODEL // 2
        x_lo, x_hi = _unpack_halves(x_ref[...])
        x_lo = x_lo.astype(BF16)
        x_hi = x_hi.astype(BF16)
        w1 = w1_ref[0, 0].astype(BF16)
        w3 = w3_ref[0, 0].astype(BF16)
        a = _dot(x_lo, w1[0:half, :]) + _dot(x_hi, w1[half:, :])
        b = _dot(x_lo, w3[0:half, :]) + _dot(x_hi, w3[half:, :])
        hm = (_silu(a) * b).astype(BF16)
        y = _dot(hm, w2_ref[0, 0].astype(BF16))
        o_ref[...] = _pack_halves(y.astype(BF16).astype(F32))

    @pl.when(i >= n_used_ref[0])
    def _():
        o_ref[...] = jnp.zeros_like(o_ref)


def moe_mlp(xb, blk_e, n_used, w1, w3, w2, layer):
    P, half = xb.shape
    n_blk = P // MOE_ROWS
    wmap = lambda i, be, nu: (layer, be[i], 0, 0)
    grid_spec = pltpu.PrefetchScalarGridSpec(
        num_scalar_prefetch=2,
        grid=(n_blk,),
        in_specs=[
            pl.BlockSpec((MOE_ROWS, half), lambda i, be, nu: (i, 0)),
            pl.BlockSpec((1, 1, D_MODEL, D_EXPERT), wmap),
            pl.BlockSpec((1, 1, D_MODEL, D_EXPERT), wmap),
            pl.BlockSpec((1, 1, D_EXPERT, D_MODEL), wmap),
        ],
        out_specs=pl.BlockSpec((MOE_ROWS, half), lambda i, be, nu: (i, 0)),
    )
    return pl.pallas_call(
        _moe_kernel,
        grid_spec=grid_spec,
        out_shape=jax.ShapeDtypeStruct((P, half), jnp.int32),
        compiler_params=_cparams(("arbitrary",)),
        name="moe_mlp",
    )(blk_e, n_used, xb, w1, w3, w2)


SC_WINDOW = 128


def sc_gather_rows(table, idx):
    n = idx.shape[0]
    width = table.shape[1]
    mesh = plsc.VectorSubcoreMesh(core_axis_name="core", subcore_axis_name="subcore")

    @functools.partial(pl.kernel, out_type=jax.ShapeDtypeStruct((n, width), table.dtype), mesh=mesh,
                       name="sc_gather_rows")
    def gather_kernel(x_hbm, i_hbm, o_hbm):
        def body(i_vmem, o_vmem):
            pltpu.sync_copy(x_hbm.at[i_vmem.at[0]], o_vmem)

        pltpu.emit_pipeline(
            body,
            grid=(n // SC_WINDOW,),
            in_specs=[pl.BlockSpec((1, SC_WINDOW), lambda i: (0, i))],
            out_specs=[pl.BlockSpec((SC_WINDOW, width), lambda i: (i, 0), pipeline_mode=pl.Buffered(1))],
            core_axis_name=("core", "subcore"),
            dimension_semantics=(pltpu.PARALLEL,),
        )(i_hbm, o_hbm)

    return gather_kernel(table, idx.reshape(1, n))


def sc_scatter_rows(rows, idx, n_out):
    n = idx.shape[0]
    t, width = rows.shape
    n_src = t // SC_WINDOW
    mesh = plsc.VectorSubcoreMesh(core_axis_name="core", subcore_axis_name="subcore")

    @functools.partial(pl.kernel, out_type=jax.ShapeDtypeStruct((n_out, width), rows.dtype), mesh=mesh,
                       name="sc_scatter_rows")
    def scatter_kernel(x_hbm, i_hbm, o_hbm):
        def body(x_vmem, i_vmem):
            pltpu.sync_copy(x_vmem, o_hbm.at[i_vmem.at[0]])

        pltpu.emit_pipeline(
            body,
            grid=(n // SC_WINDOW,),
            in_specs=[pl.BlockSpec((SC_WINDOW, width), lambda i: (i % n_src, 0), pipeline_mode=pl.Buffered(1)),
                      pl.BlockSpec((1, SC_WINDOW), lambda i: (0, i))],
            out_specs=[],
            core_axis_name=("core", "subcore"),
            dimension_semantics=(pltpu.PARALLEL,),
        )(x_hbm, i_hbm)

    return scatter_kernel(rows, idx.reshape(1, n))


def _moe_residual(x, ya, yc, route, modp_ref):
    g2 = modp_ref[0, 5:6, :]
    wa = route[:, 2:3]
    wb = route[:, 3:4]
    a_lo, a_hi = _unpack_halves(ya)
    b_lo, b_hi = _unpack_halves(yc)
    y = jnp.concatenate([wa * a_lo + wb * b_lo, wa * a_hi + wb * b_hi], axis=1)
    return x + g2 * y


def _combine_kernel(x_ref, ya_ref, yc_ref, route_ref, mod_ref, o_ref):
    o_ref[0] = _moe_residual(x_ref[0], ya_ref[0], yc_ref[0], route_ref[0], mod_ref)


def combine(x, ya, yc, route, mod, tm):
    B, S, D = x.shape
    row = lambda b, s: (b, s, 0)
    return pl.pallas_call(
        _combine_kernel,
        grid=(B, S // tm),
        in_specs=[
            pl.BlockSpec((1, tm, D), row),
            pl.BlockSpec((1, tm, D // 2), row),
            pl.BlockSpec((1, tm, D // 2), row),
            pl.BlockSpec((1, tm, ROUTE_LANES), row),
            pl.BlockSpec((1, 6, D), lambda b, s: (b, 0, 0)),
        ],
        out_specs=pl.BlockSpec((1, tm, D), row),
        out_shape=jax.ShapeDtypeStruct((B, S, D), F32),
        compiler_params=_cparams(("parallel", "parallel")),
        name="combine",
    )(x, ya, yc, route, mod)


def _prep_params(p):
    f = {}
    w_in = p["w_in"]
    L = w_in.shape[0]
    z = lambda n: jnp.zeros((L, D_MODEL, n), F32)
    f["w_in"] = jnp.concatenate(
        [w_in[:, :, 0:768], w_in[:, :, 800:1056], w_in[:, :, 1056:1312], w_in[:, :, 1312:1440],
         z(MLA_NOPE), w_in[:, :, 1440:1472], z(HEAD_PAD - MLA_QK),
         w_in[:, :, 768:800], z(128 - 2 * GLA_GATE_RANK)], axis=-1).astype(BF16)
    up = jnp.zeros((L, 128, 256), F32)
    up = up.at[:, 0:16, 0:128].set(p["gla_gate_up_f"]).at[:, 16:32, 128:256].set(p["gla_gate_up_b"])
    f["gate_up"] = up.astype(BF16)
    f["gate_bias"] = jnp.concatenate([p["gla_gate_bias_f"], p["gla_gate_bias_b"]], axis=-1)[:, None, :]
    f["lora_g"] = jnp.concatenate([p["mla_q_lora_norm_g"], p["mla_kv_lora_norm_g"]], axis=-1)[:, None, :]
    f["gla_norm_g"] = jnp.tile(p["gla_out_norm_g"], (1, GLA_HEADS))[:, None, :]
    f["norm1_g"] = p["norm1_g"][:, None, :]
    f["norm2_g"] = p["norm2_g"][:, None, :]
    wq = p["mla_w_uq"].reshape(L, MLA_Q_LORA, MLA_HEADS, MLA_QK).transpose(0, 2, 1, 3)
    f["wq"] = jnp.pad(wq, ((0, 0), (0, 0), (0, 0), (0, HEAD_PAD - MLA_QK))).astype(BF16)
    wkv = p["mla_w_ukv"].reshape(L, MLA_KV_LORA, MLA_HEADS, MLA_NOPE + MLA_V).transpose(0, 2, 1, 3)
    f["wk"] = jnp.pad(wkv[..., :MLA_NOPE], ((0, 0), (0, 0), (0, 0), (0, HEAD_PAD - MLA_NOPE))).astype(BF16)
    wv = wkv[..., MLA_NOPE:]
    zv = jnp.zeros_like(wv)
    even = (jnp.arange(MLA_HEADS) % 2 == 0)[None, :, None, None]
    f["wv"] = jnp.where(even, jnp.concatenate([wv, zv], -1), jnp.concatenate([zv, wv], -1)).astype(BF16)
    pad_qk = ((0, 0), (0, HEAD_PAD - MLA_QK))
    f["qg"] = (jnp.pad(p["mla_q_norm_g"], pad_qk) * (MLA_QK ** -0.5 * LOG2E))[:, None, :]
    f["kg"] = jnp.pad(p["mla_k_norm_g"], pad_qk)[:, None, :]
    bound = (jnp.max(jnp.abs(p["mla_q_norm_g"]), axis=-1) * jnp.max(jnp.abs(p["mla_k_norm_g"]), axis=-1)
             * (math.sqrt(MLA_QK) * 1.01 * LOG2E) + 0.1).astype(BF16).astype(F32)
    use_bound = bound <= MAX_STATIC_SHIFT * LOG2E
    f["mla_flag"] = use_bound.astype(jnp.int32)[:, None]
    f["mla_shift"] = jnp.zeros((L, 1, HEAD_PAD), F32).at[:, 0, SHIFT_LANE].set(jnp.where(use_bound, -bound, 0.0))
    f["w_out"] = p["w_out"].astype(BF16)
    rw = jnp.concatenate([p["router_group_w"], p["router_expert_w"]], axis=-1)
    rw = jnp.pad(rw, ((0, 0), (0, 0), (0, ROUTE_LANES - N_GROUPS - N_EXPERTS)))
    rw_hi = rw.astype(BF16)
    rw_lo = (rw - rw_hi.astype(F32)).astype(BF16)
    f["rw"] = jnp.stack([rw_hi, rw_lo], axis=1)
    rb = jnp.concatenate([p["router_group_b"], p["router_expert_b"]], axis=-1)
    f["rb"] = jnp.pad(rb, ((0, 0), (0, ROUTE_LANES - N_GROUPS - N_EXPERTS)))[:, None, :]
    f["w1"] = p["expert_w1"]
    f["w3"] = p["expert_w3"]
    f["w2"] = p["expert_w2"]
    return f


def _const_tables():
    c = np.arange(FNET_W)
    same = (c[:, None] // FNET_GROUP_DIM) == (c[None, :] // FNET_GROUP_DIM)
    ang = 2.0 * np.pi * ((c[:, None] % FNET_GROUP_DIM) * (c[None, :] % FNET_GROUP_DIM) % FNET_GROUP_DIM) / FNET_GROUP_DIM
    dft64 = np.stack([np.where(same, np.cos(ang), 0.0), np.where(same, np.sin(ang), 0.0)]).astype(np.float32)
    vone = np.zeros((MLA_HEADS, 1, HEAD_PAD), np.float32)
    vone[0::2, 0, MLA_V] = 1.0
    vone[1::2, 0, 0] = 1.0
    return jnp.asarray(dft64, BF16), jnp.asarray(vone)


def _seq_tables(S):
    j = lax.broadcasted_iota(jnp.int32, (S, S), 0)
    k = lax.broadcasted_iota(jnp.int32, (S, S), 1)
    ang = ((j * k) % S).astype(F32) * (2.0 * math.pi / S)
    a_dft = jnp.concatenate([jnp.cos(ang), -jnp.sin(ang)], axis=1).astype(BF16)
    half = MLA_ROPE // 2
    freqs = ROPE_THETA ** (-jnp.arange(half, dtype=F32) / half)
    ra = jnp.arange(S, dtype=F32)[:, None] * freqs[None, :]
    cos, sin = jnp.cos(ra), jnp.sin(ra)
    one = lambda n: jnp.ones((S, n), F32)
    zero = lambda n: jnp.zeros((S, n), F32)
    tail = HEAD_PAD - MLA_QK
    rope = jnp.stack([
        jnp.concatenate([one(MLA_NOPE), cos, cos, one(tail)], axis=1),
        jnp.concatenate([zero(MLA_NOPE), -sin, sin, zero(tail)], axis=1),
    ])
    return a_dft, rope


def _dispatch_plan(route2d, counts):
    T = route2d.shape[0]
    R = T * TOP_K
    cnt = counts[0, :N_EXPERTS].astype(jnp.int32)
    padded = (cnt + MOE_ROWS - 1) // MOE_ROWS * MOE_ROWS
    pend = jnp.cumsum(padded)
    pstart = pend - padded
    start = jnp.zeros((1, ROUTE_LANES), F32).at[0, :N_EXPERTS].set(pstart.astype(F32))
    dest = route_rank(route2d, start)
    P = R + N_EXPERTS * MOE_ROWS
    n_blk = P // MOE_ROWS
    blk_first = jnp.arange(n_blk, dtype=jnp.int32) * MOE_ROWS
    blk_e = jnp.minimum(jnp.sum((pend[None, :] <= blk_first[:, None]).astype(jnp.int32), axis=1), N_EXPERTS - 1)
    n_used = (pend[-1] // MOE_ROWS).astype(jnp.int32).reshape(1)
    return dest, blk_e, n_used, P


def _trunk(x, mod_all, f, consts, tables):
    B, S, D = x.shape
    T = B * S
    dft64, vone = consts
    a_dft, rope = tables
    tm = min(S, 512)
    tq = min(S, 512)
    tk = min(S, 1024)
    dft_scale = 1.0 / math.sqrt(FNET_GROUP_DIM * S)
    moe = None
    for l in range(DEPTH):
        mod = mod_all[l].reshape(B, 6, D)
        outs = in_proj(x, mod, f["norm1_g"][l], f["w_in"][l], dft64, f["gate_up"][l], f["gate_bias"][l],
                       f["lora_g"][l], tm, moe)
        if moe is not None:
            x, outs = outs[0], outs[1:]
        qkv, og, gate, xcs, cq, ckv, kr = outs
        o_gla = gla(qkv, og, gate, f["gla_norm_g"][l])
        o_fft = dft_seq(a_dft, xcs.reshape(2 * S, B * FNET_W), dft_scale)
        o_mla = mla(f["mla_flag"][l], cq, ckv, kr, rope, f["wq"][l], f["wk"][l], f["wv"][l], f["qg"][l], f["kg"][l],
                    vone, f["mla_shift"][l], tq, tk)
        x, h, route, counts = out_proj(x, o_gla, o_fft, o_mla, mod, f["w_out"][l], f["norm2_g"][l], f["rw"][l],
                                       f["rb"][l], tm)
        dest, blk_e, n_used, n_rows = _dispatch_plan(route.reshape(T, ROUTE_LANES), counts)
        xb = sc_scatter_rows(h.reshape(T, D // 2), dest.reshape(TOP_K * T), n_rows)
        yield
        yb = moe_mlp(xb, blk_e, n_used, f["w1"], f["w3"], f["w2"], l)
        ya = sc_gather_rows(yb, dest[0]).reshape(B, S, D // 2)
        yc = sc_gather_rows(yb, dest[1]).reshape(B, S, D // 2)
        moe = (ya, yc, route, mod)
        yield
    return combine(x, *moe, tm)


def _run_interleaved(gens):
    results = [None] * len(gens)
    active = list(range(len(gens)))
    while active:
        for i in list(active):
            try:
                next(gens[i])
            except StopIteration as done:
                results[i] = done.value
                active.remove(i)
    return results


def kernel(x_prompt, x_sample, c_prompt, c_sample, ada_w, ada_b, norm1_g, norm2_g, w_in, w_out, gla_gate_up_f, gla_gate_bias_f, gla_gate_up_b, gla_gate_bias_b, gla_out_norm_g, mla_q_lora_norm_g, mla_w_uq, mla_kv_lora_norm_g, mla_w_ukv, mla_q_norm_g, mla_k_norm_g, router_group_w, router_group_b, router_expert_w, router_expert_b, expert_w1, expert_w3, expert_w2):
    p = dict(norm1_g=norm1_g, norm2_g=norm2_g, w_in=w_in, w_out=w_out, gla_gate_up_f=gla_gate_up_f,
             gla_gate_bias_f=gla_gate_bias_f, gla_gate_up_b=gla_gate_up_b, gla_gate_bias_b=gla_gate_bias_b,
             gla_out_norm_g=gla_out_norm_g, mla_q_lora_norm_g=mla_q_lora_norm_g, mla_w_uq=mla_w_uq,
             mla_kv_lora_norm_g=mla_kv_lora_norm_g, mla_w_ukv=mla_w_ukv, mla_q_norm_g=mla_q_norm_g,
             mla_k_norm_g=mla_k_norm_g, router_group_w=router_group_w, router_group_b=router_group_b,
             router_expert_w=router_expert_w, router_expert_b=router_expert_b, expert_w1=expert_w1,
             expert_w3=expert_w3, expert_w2=expert_w2)
    f = _prep_params(p)
    consts = _const_tables()
    nb_p = c_prompt.shape[0]
    mod_all = ada_modulation(jnp.concatenate([c_prompt, c_sample], axis=0), ada_w, ada_b)
    y_prompt, y_sample = _run_interleaved([
        _trunk(x_prompt, mod_all[:, :nb_p], f, consts, _seq_tables(x_prompt.shape[1])),
        _trunk(x_sample, mod_all[:, nb_p:], f, consts, _seq_tables(x_sample.shape[1]))])
    return (y_prompt, y_sample)
```
